```python
import jax, jax.numpy as jnp
from jax import lax
import numpy as np

D_MODEL = 2048
BATCH = 1
SEQ = 8192
DEPTH = 2

CHUNK = 64
D_MIX = D_MODEL
A_WIDTH = D_MIX // 4
A_HEAD_DIM = 128
A_HEADS = A_WIDTH // A_HEAD_DIM
A_BLOCK = 128
B_WIDTH = D_MIX // 4
POOL_WINDOWS = (2, 4, 8, 16)
POOL_GROUPS = len(POOL_WINDOWS)
B_GROUP = B_WIDTH // POOL_GROUPS
C_WIDTH = D_MIX - A_WIDTH - B_WIDTH
C_HEAD_DIM = 128
C_HEADS = C_WIDTH // C_HEAD_DIM
Q_BLOCK = 128
N_IN = 2 * A_WIDTH + B_WIDTH + 3 * C_WIDTH + C_HEADS
SPLITS = (A_WIDTH, 2 * A_WIDTH, 2 * A_WIDTH + B_WIDTH,
          2 * A_WIDTH + B_WIDTH + C_WIDTH, 2 * A_WIDTH + B_WIDTH + 2 * C_WIDTH,
          2 * A_WIDTH + B_WIDTH + 3 * C_WIDTH)
N_EXPERTS = 16
N_GROUPS = 4
EXPERTS_PER_GROUP = N_EXPERTS // N_GROUPS
TOPK_GROUP = 1
TOP_K = 2
D_EXPERT = D_MODEL // 2
MOE_BLOCK = 128
EPS = 1e-6

kernel_name = 'hybrid_gmlp_pool_fox_moe_encoder'


def rms_norm(x, gain):
    xf = x.astype(jnp.float32)
    y = xf * lax.rsqrt(jnp.mean(xf * xf, axis=-1, keepdims=True) + EPS)
    return (y * gain.astype(jnp.float32)).astype(x.dtype)


def gmlp_mixer(u, v, ws, bs, vg):
    B, S, _ = u.shape
    u = jax.nn.gelu(u)
    v = rms_norm(jax.nn.gelu(v).reshape(B, S, A_HEADS, A_HEAD_DIM), vg)
    vb = v.reshape(B, S // A_BLOCK, A_BLOCK, A_HEADS, A_HEAD_DIM)
    cid = jnp.arange(A_BLOCK) // CHUNK
    mask = cid[None, :] <= cid[:, None]
    wm = jnp.where(mask[None], ws, 0).astype(v.dtype)
    sp = jnp.einsum('hts,bnshd->bnthd', wm, vb) + bs.T.astype(v.dtype)[None, None, :, :, None]
    return u * sp.reshape(B, S, A_WIDTH)


def pool_mixer(p, bw, bscale):
    B, S, _ = p.shape
    pf = p.astype(jnp.float32)
    cs = jnp.concatenate([jnp.zeros((B, 1, B_WIDTH), jnp.float32), jnp.cumsum(pf, axis=1)], axis=1)
    t1 = jnp.arange(1, S + 1)
    outs = []
    for g, w in enumerate(POOL_WINDOWS):
        sl = slice(g * B_GROUP, (g + 1) * B_GROUP)
        csg = cs[..., sl]
        lag = jnp.pad(csg, ((0, 0), (w, 0), (0, 0)))[:, :S + 1]
        win_sum = csg[:, 1:] - lag[:, 1:]
        cnt = jnp.minimum(t1, w).astype(jnp.float32)[None, :, None]
        outs.append(win_sum / cnt - pf[..., sl])
    d = jnp.stack(outs, axis=2).astype(p.dtype)
    y = jnp.einsum('bsgc,gce->bsge', d, bw)
    return y.reshape(B, S, B_WIDTH) * bscale


def forgetting_attention(q, k, v, fz, qg, kg, bf):
    B, S, _ = q.shape
    q = rms_norm(q.reshape(B, S, C_HEADS, C_HEAD_DIM), qg)
    k = rms_norm(k.reshape(B, S, C_HEADS, C_HEAD_DIM), kg)
    v = v.reshape(B, S, C_HEADS, C_HEAD_DIM)
    log_f = jax.nn.log_sigmoid(fz.astype(jnp.float32) + bf.astype(jnp.float32))
    F = jnp.cumsum(log_f, axis=1)
    Fk = jnp.transpose(F, (0, 2, 1))[:, :, None, :]
    nq = S // Q_BLOCK
    qb = q.reshape(B, nq, Q_BLOCK, C_HEADS, C_HEAD_DIM).transpose(1, 0, 2, 3, 4)
    Fq = F.reshape(B, nq, Q_BLOCK, C_HEADS).transpose(1, 0, 2, 3)
    kpos = jnp.arange(S)
    scale = C_HEAD_DIM ** -0.5

    def block(args):
        qi, Fi, i = args
        qpos = i * Q_BLOCK + jnp.arange(Q_BLOCK)
        s = jnp.einsum('bqhd,bkhd->bhqk', qi, k, preferred_element_type=jnp.float32) * scale
        s = s + jnp.transpose(Fi, (0, 2, 1))[..., None] - Fk
        s = jnp.where((kpos[None, :] <= qpos[:, None])[None, None], s, -jnp.inf)
        pr = jax.nn.softmax(s, axis=-1).astype(v.dtype)
        return jnp.einsum('bhqk,bkhd->bqhd', pr, v)

    out = lax.map(block, (qb, Fq, jnp.arange(nq)))
    return out.transpose(1, 0, 2, 3, 4).reshape(B, S, C_WIDTH)


def token_mixer(h, w_in, a_ws, a_bs, a_vg, b_w, b_scale, c_qg, c_kg, c_bf, w_out):
    z = h @ w_in
    u, v, p, q, k, va, fz = jnp.split(z, SPLITS, axis=-1)
    y_a = gmlp_mixer(u, v, a_ws, a_bs, a_vg)
    y_b = pool_mixer(p, b_w, b_scale)
    y_c = forgetting_attention(q, k, va, fz, c_qg, c_kg, c_bf)
    return jnp.concatenate([y_a, y_b, y_c], axis=-1) @ w_out


def route(ht, w_router, b_router):
    T = ht.shape[0]
    logits = jnp.einsum('td,de->te', ht, w_router, preferred_element_type=jnp.float32)
    scores = jax.nn.sigmoid(logits)
    sel = scores + b_router.astype(jnp.float32)
    grp = sel.reshape(T, N_GROUPS, EXPERTS_PER_GROUP)
    grp_score = lax.top_k(grp, TOP_K)[0].sum(-1)
    _, gidx = lax.top_k(grp_score, TOPK_GROUP)
    gmask = jnp.any(gidx[..., None] == jnp.arange(N_GROUPS), axis=1)
    sel = jnp.where(jnp.repeat(gmask, EXPERTS_PER_GROUP, axis=1), sel, -jnp.inf)
    _, eidx = lax.top_k(sel, TOP_K)
    gw = jnp.take_along_axis(scores, eidx, axis=1)
    gw = gw / jnp.sum(gw, axis=-1, keepdims=True)
    return eidx, gw


def moe_ffn(h, w_router, b_router, w_gate, w_up, w_down):
    B, S, D = h.shape
    T = B * S
    ht = h.reshape(T, D)
    eidx, gw = route(ht, w_router, b_router)
    A = T * TOP_K
    flat_e = eidx.reshape(A)
    flat_tok = jnp.repeat(jnp.arange(T, dtype=jnp.int32), TOP_K)
    flat_w = gw.reshape(A)
    order = jnp.argsort(flat_e)
    sorted_e = flat_e[order]
    counts = jnp.bincount(flat_e, length=N_EXPERTS)
    padded = (counts + MOE_BLOCK - 1) // MOE_BLOCK * MOE_BLOCK
    ends = jnp.cumsum(padded)
    pad_start = ends - padded
    start = jnp.cumsum(counts) - counts
    dest = pad_start[sorted_e] + jnp.arange(A) - start[sorted_e]
    P = A + N_EXPERTS * MOE_BLOCK
    n_blocks = P // MOE_BLOCK
    row_tok = jnp.zeros((P,), jnp.int32).at[dest].set(flat_tok[order])
    row_gate = jnp.zeros((P,), jnp.float32).at[dest].set(flat_w[order])
    block_start = jnp.arange(n_blocks) * MOE_BLOCK
    block_expert = jnp.minimum(jnp.searchsorted(ends, block_start, side='right'), N_EXPERTS - 1).astype(jnp.int32)
    xs = ht[row_tok].reshape(n_blocks, MOE_BLOCK, D)

    def run(args):
        xb, e = args
        hid = jax.nn.silu(xb @ w_gate[e]) * (xb @ w_up[e])
        return hid @ w_down[e]

    yb = lax.map(run, (xs, block_expert)).reshape(P, D)
    out = jnp.zeros((T, D), h.dtype).at[row_tok].add((yb * row_gate[:, None]).astype(h.dtype))
    return out.reshape(B, S, D)


def setup_inputs(seed: int = 0) -> dict:
    key = jax.random.key(seed)
    ks = jax.random.split(key, 24)
    f32 = jnp.float32
    L, D = DEPTH, D_MODEL

    def nrm(k, shape, scale):
        return jax.random.normal(k, shape, f32) * scale

    return {
        'x': nrm(ks[0], (BATCH, SEQ, D), 1.0),
        'c': nrm(ks[1], (BATCH, D), 1.0),
        'w_ada': nrm(ks[2], (L, D, 6 * D), 0.5 * D ** -0.5),
        'b_ada': nrm(ks[3], (L, 6 * D), 0.02),
        'g_mix': 1.0 + nrm(ks[4], (L, D), 0.02),
        'g_ffn': 1.0 + nrm(ks[5], (L, D), 0.02),
        'w_in': nrm(ks[6], (L, D, N_IN), D ** -0.5),
        'a_ws': nrm(ks[7], (L, A_HEADS, A_BLOCK, A_BLOCK), 0.5 * A_BLOCK ** -0.5),
        'a_bs': 1.0 + nrm(ks[8], (L, A_HEADS, A_BLOCK), 0.02),
        'a_vg': 1.0 + nrm(ks[9], (L, A_HEADS, A_HEAD_DIM), 0.02),
        'b_w': nrm(ks[10], (L, POOL_GROUPS, B_GROUP, B_GROUP), B_GROUP ** -0.5),
        'b_scale': 1.0 + nrm(ks[11], (L, B_WIDTH), 0.02),
        'c_qg': 1.0 + nrm(ks[12], (L, C_HEAD_DIM), 0.02),
        'c_kg': 1.0 + nrm(ks[13], (L, C_HEAD_DIM), 0.02),
        'c_bf': jnp.linspace(0.0, 5.0, C_HEADS, dtype=f32)[None, :] + nrm(ks[14], (L, C_HEADS), 0.1),
        'w_out': nrm(ks[15], (L, D_MIX, D), D_MIX ** -0.5),
        'w_router': nrm(ks[16], (D, N_EXPERTS), D ** -0.5),
        'b_router': nrm(ks[17], (N_EXPERTS,), 0.01),
        'e_gate': nrm(ks[18], (L, N_EXPERTS, D, D_EXPERT), D ** -0.5),
        'e_up': nrm(ks[19], (L, N_EXPERTS, D, D_EXPERT), D ** -0.5),
        'e_down': nrm(ks[20], (L, N_EXPERTS, D_EXPERT, D), D_EXPERT ** -0.5),
    }


def reference(x, c, w_ada, b_ada, g_mix, g_ffn, w_in, a_ws, a_bs, a_vg, b_w, b_scale,
              c_qg, c_kg, c_bf, w_out, w_router, b_router, e_gate, e_up, e_down):
    c_act = jax.nn.silu(c)
    for l in range(DEPTH):
        mod = c_act @ w_ada[l] + b_ada[l]
        sh1, sc1, g1, sh2, sc2, g2 = [m[:, None, :] for m in jnp.split(mod, 6, axis=-1)]
        h = rms_norm(x, g_mix[l]) * (1 + sc1) + sh1
        y = token_mixer(h, w_in[l], a_ws[l], a_bs[l], a_vg[l], b_w[l], b_scale[l],
                        c_qg[l], c_kg[l], c_bf[l], w_out[l])
        x = x + g1 * y
        h = rms_norm(x, g_ffn[l]) * (1 + sc2) + sh2
        x = x + g2 * moe_ffn(h, w_router, b_router, e_gate[l], e_up[l], e_down[l])
    return x
```

```python
import functools

import jax
import jax.numpy as jnp
from jax import lax
from jax.experimental import pallas as pl
from jax.experimental.pallas import tpu as pltpu

F32 = jnp.float32
BF16 = jnp.bfloat16

D = 2048
SEQ = 8192
DEPTH = 2
CHUNK = 64
A_WIDTH = 512
A_HEADS = 4
A_BLOCK = 128
B_WIDTH = 512
POOL_WINDOWS = (2, 4, 8, 16)
POOL_HALO = 16
C_WIDTH = 1024
C_HEADS = 8
HEAD = 128
N_EXPERTS = 16
N_GROUPS = 4
GROUP = 4
D_EXPERT = 1024
EPS = 1e-6

TM = 256
TQ = 512
TK = 512
MOE_BLK = 256
N_ASSIGN = 2 * SEQ
MOE_ROWS = N_ASSIGN + N_EXPERTS * MOE_BLK
MOE_NB = MOE_ROWS // MOE_BLK

VMEM_LIMIT = 56 * 1024 * 1024


def _nt_dot(a, b):
    return lax.dot_general(a, b, (((1,), (1,)), ((), ())), preferred_element_type=F32)


def _rms_mod(x, gain, scale, shift):
    ms = jnp.mean(x * x, axis=-1, keepdims=True)
    return (x * lax.rsqrt(ms + EPS) * gain) * (1.0 + scale) + shift


def _ada_kernel(c_ref, w_ref, b_ref, o_ref):
    ca = jax.nn.silu(c_ref[...])
    r = jnp.dot(ca, w_ref[...], preferred_element_type=F32, precision=lax.Precision.HIGHEST)
    o_ref[...] = r[0:1, :] + b_ref[...]


def _ada(c, w_ada, b_ada):
    tn = 1024
    c8 = jnp.broadcast_to(c, (8, D))
    return pl.pallas_call(
        _ada_kernel,
        out_shape=jax.ShapeDtypeStruct((DEPTH, 1, 6 * D), F32),
        grid=(DEPTH, 6 * D // tn),
        in_specs=[
            pl.BlockSpec((8, D), lambda l, j: (0, 0)),
            pl.BlockSpec((None, D, tn), lambda l, j: (l, 0, j)),
            pl.BlockSpec((None, 1, tn), lambda l, j: (l, 0, j)),
        ],
        out_specs=pl.BlockSpec((None, 1, tn), lambda l, j: (l, 0, j)),
        compiler_params=pltpu.CompilerParams(
            dimension_semantics=("arbitrary", "arbitrary"), vmem_limit_bytes=VMEM_LIMIT),
        name="ada",
    )(c8, w_ada, b_ada.reshape(DEPTH, 1, 6 * D))


def _mix_ab_kernel(x_ref, gain_ref, sc_ref, sh_ref, w_ref, ws_ref, bsb_ref, vg_ref, bw_ref,
                   bscale_ref, y_ref, pext_ref):
    i = pl.program_id(0)
    h = _rms_mod(x_ref[...], gain_ref[...], sc_ref[...], sh_ref[...])
    z = jnp.dot(h.astype(BF16), w_ref[...], preferred_element_type=F32)
    u = jax.nn.gelu(z[:, :A_WIDTH])
    v = jax.nn.gelu(z[:, A_WIDTH:2 * A_WIDTH])
    p = z[:, 2 * A_WIDTH:]

    cid_t = lax.broadcasted_iota(jnp.int32, (A_BLOCK, A_BLOCK), 0) // CHUNK
    cid_s = lax.broadcasted_iota(jnp.int32, (A_BLOCK, A_BLOCK), 1) // CHUNK
    mask = cid_s <= cid_t
    for hh in range(A_HEADS):
        cs = slice(hh * HEAD, (hh + 1) * HEAD)
        vh = v[:, cs]
        vn = vh * lax.rsqrt(jnp.mean(vh * vh, axis=-1, keepdims=True) + EPS) * vg_ref[:, cs]
        vnb = vn.astype(BF16)
        wm = jnp.where(mask, ws_ref[hh], 0.0).astype(BF16)
        for n in range(TM // A_BLOCK):
            rs = slice(n * A_BLOCK, (n + 1) * A_BLOCK)
            sp = jnp.dot(wm, vnb[rs], preferred_element_type=F32) + bsb_ref[:, cs]
            y_ref[rs, cs] = (u[rs, cs] * sp).astype(BF16)

    @pl.when(i == 0)
    def _():
        pext_ref[0:POOL_HALO, :] = jnp.zeros((POOL_HALO, B_WIDTH), F32)

    pext_ref[POOL_HALO:POOL_HALO + TM, :] = p
    t1 = i * TM + lax.broadcasted_iota(jnp.int32, (TM, 1), 0) + 1
    for g, w in enumerate(POOL_WINDOWS):
        cs = slice(g * HEAD, (g + 1) * HEAD)
        acc = p[:, cs]
        for j in range(1, w):
            acc = acc + pext_ref[POOL_HALO - j:POOL_HALO - j + TM, cs]
        cnt = jnp.minimum(t1, w).astype(F32)
        d = acc / cnt - p[:, cs]
        yb = jnp.dot(d.astype(BF16), bw_ref[g].astype(BF16), preferred_element_type=F32)
        y_ref[:, A_WIDTH + g * HEAD:A_WIDTH + (g + 1) * HEAD] = (yb * bscale_ref[:, cs]).astype(BF16)
    pext_ref[0:POOL_HALO, :] = pext_ref[TM:TM + POOL_HALO, :]


def _mix_ab(x, gain, sc, sh, w_uvp, a_ws, bsb, vg, b_w, bscale):
    row = lambda i: (0, 0)
    return pl.pallas_call(
        _mix_ab_kernel,
        out_shape=jax.ShapeDtypeStruct((SEQ, A_WIDTH + B_WIDTH), BF16),
        grid=(SEQ // TM,),
        in_specs=[
            pl.BlockSpec((TM, D), lambda i: (i, 0)),
            pl.BlockSpec((1, D), row), pl.BlockSpec((1, D), row), pl.BlockSpec((1, D), row),
            pl.BlockSpec((D, 2 * A_WIDTH + B_WIDTH), row),
            pl.BlockSpec((A_HEADS, A_BLOCK, A_BLOCK), lambda i: (0, 0, 0)),
            pl.BlockSpec((A_BLOCK, A_WIDTH), row),
            pl.BlockSpec((1, A_WIDTH), row),
            pl.BlockSpec((len(POOL_WINDOWS), HEAD, HEAD), lambda i: (0, 0, 0)),
            pl.BlockSpec((1, B_WIDTH), row),
        ],
        out_specs=pl.BlockSpec((TM, A_WIDTH + B_WIDTH), lambda i: (i, 0)),
        scratch_shapes=[pltpu.VMEM((TM + POOL_HALO, B_WIDTH), F32)],
        compiler_params=pltpu.CompilerParams(
            dimension_semantics=("arbitrary",), vmem_limit_bytes=VMEM_LIMIT),
        name="mix_ab",
    )(x, gain, sc, sh, w_uvp, a_ws, bsb, vg, b_w, bscale)


def _mix_c_kernel(x_ref, gain_ref, sc_ref, sh_ref, w_ref, wf_ref, qg_ref, kg_ref, bf_ref,
                  q_ref, k_ref, v_ref, ft_ref, carry_ref):
    i = pl.program_id(0)
    h = _rms_mod(x_ref[...], gain_ref[...], sc_ref[...], sh_ref[...])
    hb = h.astype(BF16)
    z = jnp.dot(hb, w_ref[...], preferred_element_type=F32)
    qscale = HEAD ** -0.5
    for hh in range(C_HEADS):
        cs = slice(hh * HEAD, (hh + 1) * HEAD)
        qh = z[:, cs]
        qn = qh * lax.rsqrt(jnp.mean(qh * qh, axis=-1, keepdims=True) + EPS) * qg_ref[...]
        q_ref[:, cs] = (qn * qscale).astype(BF16)
        kh = z[:, C_WIDTH + hh * HEAD:C_WIDTH + (hh + 1) * HEAD]
        kn = kh * lax.rsqrt(jnp.mean(kh * kh, axis=-1, keepdims=True) + EPS) * kg_ref[...]
        k_ref[:, cs] = kn.astype(BF16)
    v_ref[...] = z[:, 2 * C_WIDTH:].astype(BF16)

    @pl.when(i == 0)
    def _():
        carry_ref[...] = jnp.zeros((C_HEADS, HEAD), F32)

    fz = _nt_dot(wf_ref[...], hb)
    logf = jax.nn.log_sigmoid(fz + bf_ref[...])
    r = lax.broadcasted_iota(jnp.int32, (TM, TM), 0)
    c = lax.broadcasted_iota(jnp.int32, (TM, TM), 1)
    upper = (r <= c).astype(F32)
    f_cum = jnp.dot(logf, upper, preferred_element_type=F32,
                    precision=lax.Precision.HIGHEST) + carry_ref[:, 0:1]
    ft_ref[...] = f_cum
    carry_ref[...] = jnp.broadcast_to(f_cum[:, TM - 1:TM], (C_HEADS, HEAD))


def _mix_c(x, gain, sc, sh, w_qkv, wf_t, qg, kg, bf):
    row = lambda i: (0, 0)
    tile = pl.BlockSpec((TM, C_WIDTH), lambda i: (i, 0))
    return pl.pallas_call(
        _mix_c_kernel,
        out_shape=(
            jax.ShapeDtypeStruct((SEQ, C_WIDTH), BF16),
            jax.ShapeDtypeStruct((SEQ, C_WIDTH), BF16),
            jax.ShapeDtypeStruct((SEQ, C_WIDTH), BF16),
            jax.ShapeDtypeStruct((C_HEADS, SEQ), F32),
        ),
        grid=(SEQ // TM,),
        in_specs=[
            pl.BlockSpec((TM, D), lambda i: (i, 0)),
            pl.BlockSpec((1, D), row), pl.BlockSpec((1, D), row), pl.BlockSpec((1, D), row),
            pl.BlockSpec((D, 3 * C_WIDTH), row),
            pl.BlockSpec((C_HEADS, D), row),
            pl.BlockSpec((1, HEAD), row), pl.BlockSpec((1, HEAD), row),
            pl.BlockSpec((C_HEADS, 1), row),
        ],
        out_specs=(tile, tile, tile, pl.BlockSpec((C_HEADS, TM), lambda i: (0, i))),
        scratch_shapes=[pltpu.VMEM((C_HEADS, HEAD), F32)],
        compiler_params=pltpu.CompilerParams(
            dimension_semantics=("arbitrary",), vmem_limit_bytes=VMEM_LIMIT),
        name="mix_c",
    )(x, gain, sc, sh, w_qkv, wf_t, qg, kg, bf)


def _attn_kernel(q_ref, k_ref, v_ref, f_ref, o_ref):
    i = pl.program_id(1)
    q = q_ref[...]
    f0 = jnp.max(f_ref[:, pl.ds(pl.multiple_of(i * TQ, TQ), TQ)], axis=-1, keepdims=True)

    def scores(j):
        off = pl.multiple_of(j * TK, TK)
        kj = k_ref[pl.ds(off, TK), :]
        fk = f_ref[:, pl.ds(off, TK)] - f0
        return _nt_dot(q, kj) - fk, off

    def update(s, off, m, l, acc):
        m_new = jnp.maximum(m, jnp.max(s, axis=-1, keepdims=True))
        alpha = jnp.exp(m - m_new)
        p = jnp.exp(s - m_new)
        l_new = alpha * l + jnp.sum(p, axis=-1, keepdims=True)
        pv = jnp.dot(p.astype(BF16), v_ref[pl.ds(off, TK), :], preferred_element_type=F32)
        return m_new, l_new, alpha * acc + pv

    def body(j, carry):
        s, off = scores(j)
        return update(s, off, *carry)

    init = (jnp.full((TQ, 1), -jnp.inf, F32), jnp.zeros((TQ, 1), F32), jnp.zeros((TQ, HEAD), F32))
    m, l, acc = lax.fori_loop(0, i, body, init)
    s, off = scores(i)
    qpos = lax.broadcasted_iota(jnp.int32, (TQ, TK), 0)
    kpos = lax.broadcasted_iota(jnp.int32, (TQ, TK), 1)
    s = jnp.where(kpos <= qpos, s, -jnp.inf)
    m, l, acc = update(s, off, m, l, acc)
    o_ref[...] = (acc / l).astype(BF16)


def _attn(q, k, v, ft):
    return pl.pallas_call(
        _attn_kernel,
        out_shape=jax.ShapeDtypeStruct((SEQ, C_WIDTH), BF16),
        grid=(C_HEADS, SEQ // TQ),
        in_specs=[
            pl.BlockSpec((TQ, HEAD), lambda h, i: (i, h)),
            pl.BlockSpec((SEQ, HEAD), lambda h, i: (0, h)),
            pl.BlockSpec((SEQ, HEAD), lambda h, i: (0, h)),
            pl.BlockSpec((None, 1, SEQ), lambda h, i: (h, 0, 0)),
        ],
        out_specs=pl.BlockSpec((TQ, HEAD), lambda h, i: (i, h)),
        compiler_params=pltpu.CompilerParams(
            dimension_semantics=("arbitrary", "arbitrary"), vmem_limit_bytes=VMEM_LIMIT),
        name="attn",
    )(q, k, v, ft.reshape(C_HEADS, 1, SEQ))


def _first_max(vals):
    best, idx = vals[0], jnp.zeros(vals[0].shape, jnp.int32)
    for j in range(1, len(vals)):
        better = vals[j] > best
        idx = jnp.where(better, j, idx)
        best = jnp.where(better, vals[j], best)
    return idx, best


def _pick(idx, vals):
    out = vals[-1]
    for j in range(len(vals) - 2, -1, -1):
        out = jnp.where(idx == j, vals[j], out)
    return out


def _out_kernel(yab_ref, yc_ref, x_ref, w_ref, g1_ref, gain_ref, sc_ref, sh_ref, wr_ref, br_ref,
                x1_ref, h2_ref, info_ref, cnt_ref, carry_ref):
    i = pl.program_id(0)
    y = jnp.dot(yab_ref[...], w_ref[0:A_WIDTH + B_WIDTH, :], preferred_element_type=F32)
    y = y + jnp.dot(yc_ref[...], w_ref[A_WIDTH + B_WIDTH:, :], preferred_element_type=F32)
    x1 = x_ref[...] + g1_ref[...] * y
    x1_ref[...] = x1
    h2 = _rms_mod(x1, gain_ref[...], sc_ref[...], sh_ref[...])
    h2_ref[...] = h2

    logits = _nt_dot(wr_ref[...], h2.astype(BF16))
    scores = jax.nn.sigmoid(logits)
    sel = scores + br_ref[...]
    sel_r = [sel[k:k + 1, :] for k in range(N_EXPERTS)]
    sc_r = [scores[k:k + 1, :] for k in range(N_EXPERTS)]
    grp = []
    for g in range(N_GROUPS):
        a = sel_r[GROUP * g:GROUP * (g + 1)]
        pair = [a[p] + a[q] for p in range(GROUP) for q in range(p + 1, GROUP)]
        grp.append(functools.reduce(jnp.maximum, pair))
    gi, _ = _first_max(grp)
    cand = [_pick(gi, [sel_r[GROUP * g + j] for g in range(N_GROUPS)]) for j in range(GROUP)]
    cand_s = [_pick(gi, [sc_r[GROUP * g + j] for g in range(N_GROUPS)]) for j in range(GROUP)]
    i0, _ = _first_max(cand)
    i1, _ = _first_max([jnp.where(i0 == j, -jnp.inf, cand[j]) for j in range(GROUP)])
    s0, s1 = _pick(i0, cand_s), _pick(i1, cand_s)
    e0, e1 = GROUP * gi + i0, GROUP * gi + i1
    den = s0 + s1

    @pl.when(i == 0)
    def _():
        carry_ref[...] = jnp.zeros((N_EXPERTS, HEAD), F32)

    ek = lax.broadcasted_iota(jnp.int32, (N_EXPERTS, TM), 0)
    oh0 = (ek == e0).astype(F32)
    oh1 = (ek == e1).astype(F32)
    both = oh0 + oh1
    r = lax.broadcasted_iota(jnp.int32, (TM, TM), 0)
    c = lax.broadcasted_iota(jnp.int32, (TM, TM), 1)
    before = (r < c).astype(BF16)
    run = jnp.dot(both.astype(BF16), before, preferred_element_type=F32) + carry_ref[:, 0:1]
    rank0 = jnp.sum(run * oh0, axis=0, keepdims=True)
    rank1 = jnp.sum(run * oh1, axis=0, keepdims=True)
    total = carry_ref[...] + jnp.sum(both, axis=-1, keepdims=True)
    carry_ref[...] = total
    cnt_ref[...] = total

    info_ref[0:1, :] = e0.astype(F32)
    info_ref[1:2, :] = e1.astype(F32)
    info_ref[2:3, :] = s0 / den
    info_ref[3:4, :] = s1 / den
    info_ref[4:5, :] = rank0
    info_ref[5:6, :] = rank1
    info_ref[6:8, :] = jnp.zeros((2, TM), F32)


def _out_proj(yab, yc, x, w_out, g1, gain, sc, sh, wr_t, br):
    row = lambda i: (0, 0)
    return pl.pallas_call(
        _out_kernel,
        out_shape=(
            jax.ShapeDtypeStruct((SEQ, D), F32),
            jax.ShapeDtypeStruct((SEQ, D), F32),
            jax.ShapeDtypeStruct((8, SEQ), F32),
            jax.ShapeDtypeStruct((N_EXPERTS, HEAD), F32),
        ),
        grid=(SEQ // TM,),
        in_specs=[
            pl.BlockSpec((TM, A_WIDTH + B_WIDTH), lambda i: (i, 0)),
            pl.BlockSpec((TM, C_WIDTH), lambda i: (i, 0)),
            pl.BlockSpec((TM, D), lambda i: (i, 0)),
            pl.BlockSpec((D, D), row),
            pl.BlockSpec((1, D), row), pl.BlockSpec((1, D), row),
            pl.BlockSpec((1, D), row), pl.BlockSpec((1, D), row),
            pl.BlockSpec((N_EXPERTS, D), row),
            pl.BlockSpec((N_EXPERTS, 1), row),
        ],
        out_specs=(
            pl.BlockSpec((TM, D), lambda i: (i, 0)),
            pl.BlockSpec((TM, D), lambda i: (i, 0)),
            pl.BlockSpec((8, TM), lambda i: (0, i)),
            pl.BlockSpec((N_EXPERTS, HEAD), row),
        ),
        scratch_shapes=[pltpu.VMEM((N_EXPERTS, HEAD), F32)],
        compiler_params=pltpu.CompilerParams(
            dimension_semantics=("arbitrary",), vmem_limit_bytes=VMEM_LIMIT),
        name="out_proj",
    )(yab, yc, x, w_out, g1, gain, sc, sh, wr_t, br)


def _moe_kernel(dest_ref, bexp_ref, nvalid_ref, nused_ref, h_hbm, wg_ref, wu_ref, wd_ref, y_hbm,
                code_ref, xs_ref, ys_ref, gsem, ssem):
    b = pl.program_id(0)
    n_used = nused_ref[0]
    slot = b % 2

    def gather_copy(blk, r, s):
        tok = code_ref[blk * MOE_BLK + r] & (SEQ - 1)
        return pltpu.make_async_copy(h_hbm.at[pl.ds(tok, 1), :], xs_ref.at[s, pl.ds(r, 1), :],
                                     gsem.at[s])

    def scatter_copy(blk, r, s):
        dst = code_ref[blk * MOE_BLK + r]
        return pltpu.make_async_copy(ys_ref.at[s, pl.ds(r, 1), :], y_hbm.at[pl.ds(dst, 1), :],
                                     ssem.at[s])

    def for_rows(n, fn):
        def body(r, _):
            fn(r)
            return 0
        lax.fori_loop(0, n, body, 0)

    @pl.when(b == 0)
    def _():
        def fill(r, _):
            code_ref[r] = 0
            return 0
        lax.fori_loop(0, MOE_ROWS, fill, 0)

        def place(a, _):
            code_ref[dest_ref[a]] = a
            return 0
        lax.fori_loop(0, N_ASSIGN, place, 0)
        for_rows(MOE_BLK, lambda r: gather_copy(0, r, 0).start())

    @pl.when(b < n_used)
    def _():
        for_rows(MOE_BLK, lambda r: gather_copy(b, r, slot).wait())

        @pl.when(b + 1 < n_used)
        def _():
            for_rows(MOE_BLK, lambda r: gather_copy(b + 1, r, 1 - slot).start())

        x = xs_ref[slot].astype(BF16)
        gate = jnp.dot(x, wg_ref[...], preferred_element_type=F32)
        up = jnp.dot(x, wu_ref[...], preferred_element_type=F32)
        hid = (jax.nn.silu(gate) * up).astype(BF16)
        y = jnp.dot(hid, wd_ref[...], preferred_element_type=F32)

        @pl.when(b >= 2)
        def _():
            for_rows(nvalid_ref[b - 2], lambda r: scatter_copy(b - 2, r, slot).wait())

        ys_ref[slot] = y
        for_rows(nvalid_ref[b], lambda r: scatter_copy(b, r, slot).start())

        @pl.when(b == n_used - 1)
        def _():
            for_rows(nvalid_ref[b], lambda r: scatter_copy(b, r, slot).wait())

            @pl.when(b >= 1)
            def _():
                for_rows(nvalid_ref[b - 1], lambda r: scatter_copy(b - 1, r, 1 - slot).wait())


def _moe(dest, bexp, nvalid, nused, h2, wg, wu, wd):
    def wmap(b, dest_ref, bexp_ref, nvalid_ref, nused_ref):
        return (bexp_ref[jnp.minimum(b, nused_ref[0] - 1)], 0, 0)

    return pl.pallas_call(
        _moe_kernel,
        out_shape=jax.ShapeDtypeStruct((N_ASSIGN, D), F32),
        grid_spec=pltpu.PrefetchScalarGridSpec(
            num_scalar_prefetch=4,
            grid=(MOE_NB,),
            in_specs=[
                pl.BlockSpec(memory_space=pl.ANY),
                pl.BlockSpec((None, D, D_EXPERT), wmap),
                pl.BlockSpec((None, D, D_EXPERT), wmap),
                pl.BlockSpec((None, D_EXPERT, D), wmap),
            ],
            out_specs=pl.BlockSpec(memory_space=pl.ANY),
            scratch_shapes=[
                pltpu.SMEM((MOE_ROWS,), jnp.int32),
                pltpu.VMEM((2, MOE_BLK, D), F32),
                pltpu.VMEM((2, MOE_BLK, D), F32),
                pltpu.SemaphoreType.DMA((2,)),
                pltpu.SemaphoreType.DMA((2,)),
            ],
        ),
        compiler_params=pltpu.CompilerParams(
            dimension_semantics=("arbitrary",), vmem_limit_bytes=VMEM_LIMIT),
        name="moe",
    )(dest, bexp, nvalid, nused, h2, wg, wu, wd)


def _combine_kernel(x1_ref, y0_ref, y1_ref, gw_ref, g2_ref, o_ref):
    gw = gw_ref[...]
    moe = gw[:, 0:1] * y0_ref[...] + gw[:, 1:2] * y1_ref[...]
    o_ref[...] = x1_ref[...] + g2_ref[...] * moe


def _combine(x1, y, gw, g2):
    nt = SEQ // TM
    return pl.pallas_call(
        _combine_kernel,
        out_shape=jax.ShapeDtypeStruct((SEQ, D), F32),
        grid=(nt,),
        in_specs=[
            pl.BlockSpec((TM, D), lambda i: (i, 0)),
            pl.BlockSpec((TM, D), lambda i: (i, 0)),
            pl.BlockSpec((TM, D), lambda i: (i + nt, 0)),
            pl.BlockSpec((TM, 2), lambda i: (i, 0)),
            pl.BlockSpec((1, D), lambda i: (0, 0)),
        ],
        out_specs=pl.BlockSpec((TM, D), lambda i: (i, 0)),
        compiler_params=pltpu.CompilerParams(
            dimension_semantics=("arbitrary",), vmem_limit_bytes=VMEM_LIMIT),
        name="combine",
    )(x1, y, y, gw, g2)


def _dispatch_plan(info, counts):
    cnt = counts[:, 0].astype(jnp.int32)
    padded = (cnt + MOE_BLK - 1) // MOE_BLK * MOE_BLK
    ends = jnp.cumsum(padded)
    pad_start = ends - padded
    e = info[0:2].astype(jnp.int32)
    rank = info[4:6].astype(jnp.int32)
    onehot = e[:, :, None] == jnp.arange(N_EXPERTS, dtype=jnp.int32)
    dest = jnp.sum(jnp.where(onehot, pad_start, 0), axis=-1) + rank
    blk_start = jnp.arange(MOE_NB, dtype=jnp.int32) * MOE_BLK
    bexp = jnp.sum(blk_start[:, None] >= ends[None, :], axis=-1)
    bexp = jnp.minimum(bexp, N_EXPERTS - 1).astype(jnp.int32)
    in_blk = jnp.arange(N_EXPERTS, dtype=jnp.int32)[None, :] == bexp[:, None]
    left = jnp.sum(jnp.where(in_blk, cnt + pad_start, 0), axis=-1) - blk_start
    nvalid = jnp.clip(left, 0, MOE_BLK).astype(jnp.int32)
    nused = (ends[-1:] // MOE_BLK).astype(jnp.int32)
    return dest.reshape(N_ASSIGN), bexp, nvalid, nused


def kernel(x, c, w_ada, b_ada, g_mix, g_ffn, w_in, a_ws, a_bs, a_vg, b_w, b_scale,
           c_qg, c_kg, c_bf, w_out, w_router, b_router, e_gate, e_up, e_down):
    xs = x.reshape(SEQ, D)
    mod = _ada(c, w_ada, b_ada)
    wr_t = w_router.T.astype(BF16)
    br = b_router.reshape(N_EXPERTS, 1).astype(F32)
    n_uvp = 2 * A_WIDTH + B_WIDTH
    for l in range(DEPTH):
        sh1, sc1, g1, sh2, sc2, g2 = [mod[l, :, j * D:(j + 1) * D] for j in range(6)]
        gain1 = g_mix[l].reshape(1, D)
        w_uvp = w_in[l, :, :n_uvp].astype(BF16)
        w_qkv = w_in[l, :, n_uvp:n_uvp + 3 * C_WIDTH].astype(BF16)
        wf_t = w_in[l, :, n_uvp + 3 * C_WIDTH:].T.astype(BF16)
        bsb = jnp.repeat(a_bs[l].T, HEAD, axis=1)
        yab = _mix_ab(xs, gain1, sc1, sh1, w_uvp, a_ws[l], bsb, a_vg[l].reshape(1, A_WIDTH),
                      b_w[l], b_scale[l].reshape(1, B_WIDTH))
        q, k, v, ft = _mix_c(xs, gain1, sc1, sh1, w_qkv, wf_t, c_qg[l].reshape(1, HEAD),
                             c_kg[l].reshape(1, HEAD), c_bf[l].reshape(C_HEADS, 1))
        yc = _attn(q, k, v, ft)
        x1, h2, info, counts = _out_proj(yab, yc, xs, w_out[l].astype(BF16), g1,
                                         g_ffn[l].reshape(1, D), sc2, sh2, wr_t, br)
        dest, bexp, nvalid, nused = _dispatch_plan(info, counts)
        y = _moe(dest, bexp, nvalid, nused, h2, e_gate[l].astype(BF16), e_up[l].astype(BF16),
                 e_down[l].astype(BF16))
        xs = _combine(x1, y, info[2:4].T, g2)
    return xs.reshape(1, SEQ, D)
```

```python
import functools

import jax
import jax.numpy as jnp
from jax import lax
from jax.experimental import pallas as pl
from jax.experimental.pallas import tpu as pltpu

F32 = jnp.float32
BF16 = jnp.bfloat16

D = 2048
SEQ = 8192
DEPTH = 2
CHUNK = 64
A_WIDTH = 512
A_HEADS = 4
A_BLOCK = 128
B_WIDTH = 512
POOL_WINDOWS = (2, 4, 8, 16)
POOL_HALO = 16
C_WIDTH = 1024
C_HEADS = 8
HEAD = 128
N_EXPERTS = 16
N_GROUPS = 4
GROUP = 4
D_EXPERT = 1024
EPS = 1e-6

TM = 256
TQ = 512
TK = 512
MOE_BLK = 256
MOE_CHUNKS = 4
N_ASSIGN = 2 * SEQ
MOE_ROWS = N_ASSIGN + N_EXPERTS * MOE_BLK
MOE_NB = MOE_ROWS // MOE_BLK

VMEM_LIMIT = 56 * 1024 * 1024


def _nt_dot(a, b):
    return lax.dot_general(a, b, (((1,), (1,)), ((), ())), preferred_element_type=F32)


def _rms_mod(x, gain, scale, shift):
    ms = jnp.mean(x * x, axis=-1, keepdims=True)
    return (x * lax.rsqrt(ms + EPS) * gain) * (1.0 + scale) + shift


def _ada_kernel(c_ref, w_ref, b_ref, o_ref):
    ca = jax.nn.silu(c_ref[...])
    r = jnp.dot(ca, w_ref[...], preferred_element_type=F32, precision=lax.Precision.HIGHEST)
    o_ref[...] = r[0:1, :] + b_ref[...]


def _ada(c, w_ada, b_ada):
    tn = 1024
    c8 = jnp.broadcast_to(c, (8, D))
    return pl.pallas_call(
        _ada_kernel,
        out_shape=jax.ShapeDtypeStruct((DEPTH, 1, 6 * D), F32),
        grid=(DEPTH, 6 * D // tn),
        in_specs=[
            pl.BlockSpec((8, D), lambda l, j: (0, 0)),
            pl.BlockSpec((None, D, tn), lambda l, j: (l, 0, j)),
            pl.BlockSpec((None, 1, tn), lambda l, j: (l, 0, j)),
        ],
        out_specs=pl.BlockSpec((None, 1, tn), lambda l, j: (l, 0, j)),
        compiler_params=pltpu.CompilerParams(
            dimension_semantics=("arbitrary", "arbitrary"), vmem_limit_bytes=VMEM_LIMIT),
        name="ada",
    )(c8, w_ada, b_ada.reshape(DEPTH, 1, 6 * D))


def _mix_ab_kernel(x_ref, gain_ref, sc_ref, sh_ref, w_ref, ws_ref, bsb_ref, vg_ref, bw_ref,
                   bscale_ref, y_ref, pext_ref):
    i = pl.program_id(0)
    h = _rms_mod(x_ref[...], gain_ref[...], sc_ref[...], sh_ref[...])
    z = jnp.dot(h.astype(BF16), w_ref[...], preferred_element_type=F32)
    u = jax.nn.gelu(z[:, :A_WIDTH])
    v = jax.nn.gelu(z[:, A_WIDTH:2 * A_WIDTH])
    p = z[:, 2 * A_WIDTH:]

    cid_t = lax.broadcasted_iota(jnp.int32, (A_BLOCK, A_BLOCK), 0) // CHUNK
    cid_s = lax.broadcasted_iota(jnp.int32, (A_BLOCK, A_BLOCK), 1) // CHUNK
    mask = cid_s <= cid_t
    for hh in range(A_HEADS):
        cs = slice(hh * HEAD, (hh + 1) * HEAD)
        vh = v[:, cs]
        vn = vh * lax.rsqrt(jnp.mean(vh * vh, axis=-1, keepdims=True) + EPS) * vg_ref[:, cs]
        vnb = vn.astype(BF16)
        wm = jnp.where(mask, ws_ref[hh], 0.0).astype(BF16)
        for n in range(TM // A_BLOCK):
            rs = slice(n * A_BLOCK, (n + 1) * A_BLOCK)
            sp = jnp.dot(wm, vnb[rs], preferred_element_type=F32) + bsb_ref[:, cs]
            y_ref[rs, cs] = (u[rs, cs] * sp).astype(BF16)

    @pl.when(i == 0)
    def _():
        pext_ref[0:POOL_HALO, :] = jnp.zeros((POOL_HALO, B_WIDTH), F32)

    pext_ref[POOL_HALO:POOL_HALO + TM, :] = p
    t1 = i * TM + lax.broadcasted_iota(jnp.int32, (TM, 1), 0) + 1
    for g, w in enumerate(POOL_WINDOWS):
        cs = slice(g * HEAD, (g + 1) * HEAD)
        acc = p[:, cs]
        for j in range(1, w):
            acc = acc + pext_ref[POOL_HALO - j:POOL_HALO - j + TM, cs]
        cnt = jnp.minimum(t1, w).astype(F32)
        d = acc / cnt - p[:, cs]
        yb = jnp.dot(d.astype(BF16), bw_ref[g].astype(BF16), preferred_element_type=F32)
        y_ref[:, A_WIDTH + g * HEAD:A_WIDTH + (g + 1) * HEAD] = (yb * bscale_ref[:, cs]).astype(BF16)
    pext_ref[0:POOL_HALO, :] = pext_ref[TM:TM + POOL_HALO, :]


def _mix_ab(x, gain, sc, sh, w_uvp, a_ws, bsb, vg, b_w, bscale):
    row = lambda i: (0, 0)
    return pl.pallas_call(
        _mix_ab_kernel,
        out_shape=jax.ShapeDtypeStruct((SEQ, A_WIDTH + B_WIDTH), BF16),
        grid=(SEQ // TM,),
        in_specs=[
            pl.BlockSpec((TM, D), lambda i: (i, 0)),
            pl.BlockSpec((1, D), row), pl.BlockSpec((1, D), row), pl.BlockSpec((1, D), row),
            pl.BlockSpec((D, 2 * A_WIDTH + B_WIDTH), row),
            pl.BlockSpec((A_HEADS, A_BLOCK, A_BLOCK), lambda i: (0, 0, 0)),
            pl.BlockSpec((A_BLOCK, A_WIDTH), row),
            pl.BlockSpec((1, A_WIDTH), row),
            pl.BlockSpec((len(POOL_WINDOWS), HEAD, HEAD), lambda i: (0, 0, 0)),
            pl.BlockSpec((1, B_WIDTH), row),
        ],
        out_specs=pl.BlockSpec((TM, A_WIDTH + B_WIDTH), lambda i: (i, 0)),
        scratch_shapes=[pltpu.VMEM((TM + POOL_HALO, B_WIDTH), F32)],
        compiler_params=pltpu.CompilerParams(
            dimension_semantics=("arbitrary",), vmem_limit_bytes=VMEM_LIMIT),
        name="mix_ab",
    )(x, gain, sc, sh, w_uvp, a_ws, bsb, vg, b_w, bscale)


def _mix_c_kernel(x_ref, gain_ref, sc_ref, sh_ref, w_ref, wf_ref, qg_ref, kg_ref, bf_ref,
                  q_ref, k_ref, v_ref, ft_ref, carry_ref):
    i = pl.program_id(0)
    h = _rms_mod(x_ref[...], gain_ref[...], sc_ref[...], sh_ref[...])
    hb = h.astype(BF16)
    z = jnp.dot(hb, w_ref[...], preferred_element_type=F32)
    qscale = HEAD ** -0.5
    for hh in range(C_HEADS):
        cs = slice(hh * HEAD, (hh + 1) * HEAD)
        qh = z[:, cs]
        qn = qh * lax.rsqrt(jnp.mean(qh * qh, axis=-1, keepdims=True) + EPS) * qg_ref[...]
        q_ref[:, cs] = (qn * qscale).astype(BF16)
        kh = z[:, C_WIDTH + hh * HEAD:C_WIDTH + (hh + 1) * HEAD]
        kn = kh * lax.rsqrt(jnp.mean(kh * kh, axis=-1, keepdims=True) + EPS) * kg_ref[...]
        k_ref[:, cs] = kn.astype(BF16)
    v_ref[...] = z[:, 2 * C_WIDTH:].astype(BF16)

    @pl.when(i == 0)
    def _():
        carry_ref[...] = jnp.zeros((C_HEADS, HEAD), F32)

    fz = _nt_dot(wf_ref[...], hb)
    logf = jax.nn.log_sigmoid(fz + bf_ref[...])
    r = lax.broadcasted_iota(jnp.int32, (TM, TM), 0)
    c = lax.broadcasted_iota(jnp.int32, (TM, TM), 1)
    upper = (r <= c).astype(F32)
    f_cum = jnp.dot(logf, upper, preferred_element_type=F32,
                    precision=lax.Precision.HIGHEST) + carry_ref[:, 0:1]
    ft_ref[...] = f_cum
    carry_ref[...] = jnp.broadcast_to(f_cum[:, TM - 1:TM], (C_HEADS, HEAD))


def _mix_c(x, gain, sc, sh, w_qkv, wf_t, qg, kg, bf):
    row = lambda i: (0, 0)
    tile = pl.BlockSpec((TM, C_WIDTH), lambda i: (i, 0))
    return pl.pallas_call(
        _mix_c_kernel,
        out_shape=(
            jax.ShapeDtypeStruct((SEQ, C_WIDTH), BF16),
            jax.ShapeDtypeStruct((SEQ, C_WIDTH), BF16),
            jax.ShapeDtypeStruct((SEQ, C_WIDTH), BF16),
            jax.ShapeDtypeStruct((C_HEADS, SEQ), F32),
        ),
        grid=(SEQ // TM,),
        in_specs=[
            pl.BlockSpec((TM, D), lambda i: (i, 0)),
            pl.BlockSpec((1, D), row), pl.BlockSpec((1, D), row), pl.BlockSpec((1, D), row),
            pl.BlockSpec((D, 3 * C_WIDTH), row),
            pl.BlockSpec((C_HEADS, D), row),
            pl.BlockSpec((1, HEAD), row), pl.BlockSpec((1, HEAD), row),
            pl.BlockSpec((C_HEADS, 1), row),
        ],
        out_specs=(tile, tile, tile, pl.BlockSpec((C_HEADS, TM), lambda i: (0, i))),
        scratch_shapes=[pltpu.VMEM((C_HEADS, HEAD), F32)],
        compiler_params=pltpu.CompilerParams(
            dimension_semantics=("arbitrary",), vmem_limit_bytes=VMEM_LIMIT),
        name="mix_c",
    )(x, gain, sc, sh, w_qkv, wf_t, qg, kg, bf)


def _attn_kernel(q_ref, k_ref, v_ref, f_ref, o_ref):
    i = pl.program_id(1)
    q = q_ref[...]
    f0 = jnp.max(f_ref[:, pl.ds(pl.multiple_of(i * TQ, TQ), TQ)], axis=-1, keepdims=True)

    def scores(j):
        off = pl.multiple_of(j * TK, TK)
        kj = k_ref[pl.ds(off, TK), :]
        fk = f_ref[:, pl.ds(off, TK)] - f0
        return _nt_dot(q, kj) - fk, off

    def update(s, off, m, l, acc):
        m_new = jnp.maximum(m, jnp.max(s, axis=-1, keepdims=True))
        alpha = jnp.exp(m - m_new)
        p = jnp.exp(s - m_new)
        l_new = alpha * l + jnp.sum(p, axis=-1, keepdims=True)
        pv = jnp.dot(p.astype(BF16), v_ref[pl.ds(off, TK), :], preferred_element_type=F32)
        return m_new, l_new, alpha * acc + pv

    def body(j, carry):
        s, off = scores(j)
        return update(s, off, *carry)

    init = (jnp.full((TQ, 1), -jnp.inf, F32), jnp.zeros((TQ, 1), F32), jnp.zeros((TQ, HEAD), F32))
    m, l, acc = lax.fori_loop(0, i, body, init)
    s, off = scores(i)
    qpos = lax.broadcasted_iota(jnp.int32, (TQ, TK), 0)
    kpos = lax.broadcasted_iota(jnp.int32, (TQ, TK), 1)
    s = jnp.where(kpos <= qpos, s, -jnp.inf)
    m, l, acc = update(s, off, m, l, acc)
    o_ref[...] = (acc / l).astype(BF16)


def _attn(q, k, v, ft):
    return pl.pallas_call(
        _attn_kernel,
        out_shape=jax.ShapeDtypeStruct((SEQ, C_WIDTH), BF16),
        grid=(C_HEADS, SEQ // TQ),
        in_specs=[
            pl.BlockSpec((TQ, HEAD), lambda h, i: (i, h)),
            pl.BlockSpec((SEQ, HEAD), lambda h, i: (0, h)),
            pl.BlockSpec((SEQ, HEAD), lambda h, i: (0, h)),
            pl.BlockSpec((None, 1, SEQ), lambda h, i: (h, 0, 0)),
        ],
        out_specs=pl.BlockSpec((TQ, HEAD), lambda h, i: (i, h)),
        compiler_params=pltpu.CompilerParams(
            dimension_semantics=("arbitrary", "arbitrary"), vmem_limit_bytes=VMEM_LIMIT),
        name="attn",
    )(q, k, v, ft.reshape(C_HEADS, 1, SEQ))


def _first_max(vals):
    best, idx = vals[0], jnp.zeros(vals[0].shape, jnp.int32)
    for j in range(1, len(vals)):
        better = vals[j] > best
        idx = jnp.where(better, j, idx)
        best = jnp.where(better, vals[j], best)
    return idx, best


def _pick(idx, vals):
    out = vals[-1]
    for j in range(len(vals) - 2, -1, -1):
        out = jnp.where(idx == j, vals[j], out)
    return out


def _out_kernel(yab_ref, yc_ref, x_ref, w_ref, g1_ref, gain_ref, sc_ref, sh_ref, wr_ref, br_ref,
                x1_ref, h2_ref, info_ref, cnt_ref, carry_ref):
    i = pl.program_id(0)
    y = jnp.dot(yab_ref[...], w_ref[0:A_WIDTH + B_WIDTH, :], preferred_element_type=F32)
    y = y + jnp.dot(yc_ref[...], w_ref[A_WIDTH + B_WIDTH:, :], preferred_element_type=F32)
    x1 = x_ref[...] + g1_ref[...] * y
    x1_ref[...] = x1
    h2 = _rms_mod(x1, gain_ref[...], sc_ref[...], sh_ref[...])
    h2_ref[...] = h2

    logits = _nt_dot(wr_ref[...], h2.astype(BF16))
    scores = jax.nn.sigmoid(logits)
    sel = scores + br_ref[...]
    sel_r = [sel[k:k + 1, :] for k in range(N_EXPERTS)]
    sc_r = [scores[k:k + 1, :] for k in range(N_EXPERTS)]
    grp = []
    for g in range(N_GROUPS):
        a = sel_r[GROUP * g:GROUP * (g + 1)]
        pair = [a[p] + a[q] for p in range(GROUP) for q in range(p + 1, GROUP)]
        grp.append(functools.reduce(jnp.maximum, pair))
    gi, _ = _first_max(grp)
    cand = [_pick(gi, [sel_r[GROUP * g + j] for g in range(N_GROUPS)]) for j in range(GROUP)]
    cand_s = [_pick(gi, [sc_r[GROUP * g + j] for g in range(N_GROUPS)]) for j in range(GROUP)]
    i0, _ = _first_max(cand)
    i1, _ = _first_max([jnp.where(i0 == j, -jnp.inf, cand[j]) for j in range(GROUP)])
    s0, s1 = _pick(i0, cand_s), _pick(i1, cand_s)
    e0, e1 = GROUP * gi + i0, GROUP * gi + i1
    den = s0 + s1

    @pl.when(i == 0)
    def _():
        carry_ref[...] = jnp.zeros((N_EXPERTS, HEAD), F32)

    ek = lax.broadcasted_iota(jnp.int32, (N_EXPERTS, TM), 0)
    oh0 = (ek == e0).astype(F32)
    oh1 = (ek == e1).astype(F32)
    both = oh0 + oh1
    r = lax.broadcasted_iota(jnp.int32, (TM, TM), 0)
    c = lax.broadcasted_iota(jnp.int32, (TM, TM), 1)
    before = (r < c).astype(BF16)
    run = jnp.dot(both.astype(BF16), before, preferred_element_type=F32) + carry_ref[:, 0:1]
    rank0 = jnp.sum(run * oh0, axis=0, keepdims=True)
    rank1 = jnp.sum(run * oh1, axis=0, keepdims=True)
    total = carry_ref[...] + jnp.sum(both, axis=-1, keepdims=True)
    carry_ref[...] = total
    cnt_ref[...] = total

    info_ref[0:1, :] = e0.astype(F32)
    info_ref[1:2, :] = e1.astype(F32)
    info_ref[2:3, :] = s0 / den
    info_ref[3:4, :] = s1 / den
    info_ref[4:5, :] = rank0
    info_ref[5:6, :] = rank1
    info_ref[6:8, :] = jnp.zeros((2, TM), F32)


def _out_proj(yab, yc, x, w_out, g1, gain, sc, sh, wr_t, br):
    row = lambda i: (0, 0)
    return pl.pallas_call(
        _out_kernel,
        out_shape=(
            jax.ShapeDtypeStruct((SEQ, D), F32),
            jax.ShapeDtypeStruct((SEQ, D), F32),
            jax.ShapeDtypeStruct((8, SEQ), F32),
            jax.ShapeDtypeStruct((N_EXPERTS, HEAD), F32),
        ),
        grid=(SEQ // TM,),
        in_specs=[
            pl.BlockSpec((TM, A_WIDTH + B_WIDTH), lambda i: (i, 0)),
            pl.BlockSpec((TM, C_WIDTH), lambda i: (i, 0)),
            pl.BlockSpec((TM, D), lambda i: (i, 0)),
            pl.BlockSpec((D, D), row),
            pl.BlockSpec((1, D), row), pl.BlockSpec((1, D), row),
            pl.BlockSpec((1, D), row), pl.BlockSpec((1, D), row),
            pl.BlockSpec((N_EXPERTS, D), row),
            pl.BlockSpec((N_EXPERTS, 1), row),
        ],
        out_specs=(
            pl.BlockSpec((TM, D), lambda i: (i, 0)),
            pl.BlockSpec((TM, D), lambda i: (i, 0)),
            pl.BlockSpec((8, TM), lambda i: (0, i)),
            pl.BlockSpec((N_EXPERTS, HEAD), row),
        ),
        scratch_shapes=[pltpu.VMEM((N_EXPERTS, HEAD), F32)],
        compiler_params=pltpu.CompilerParams(
            dimension_semantics=("arbitrary",), vmem_limit_bytes=VMEM_LIMIT),
        name="out_proj",
    )(yab, yc, x, w_out, g1, gain, sc, sh, wr_t, br)


def _moe_kernel(dest_ref, bexp_ref, nvalid_ref, nused_ref, h_hbm, wg_ref, wu_ref, wd_ref, y_hbm,
                code_ref, xs_ref, ys_ref, xb_ref, hid_ref, gsem, ssem):
    b = pl.program_id(0)
    n_used = nused_ref[0]
    slot = b % 2

    def gather_copy(blk, r, s):
        tok = code_ref[blk * MOE_BLK + r] & (SEQ - 1)
        return pltpu.make_async_copy(h_hbm.at[pl.ds(tok, 1), :], xs_ref.at[s, pl.ds(r, 1), :],
                                     gsem.at[s])

    def scatter_copy(blk, r, s):
        dst = code_ref[blk * MOE_BLK + r]
        return pltpu.make_async_copy(ys_ref.at[s, pl.ds(r, 1), :], y_hbm.at[pl.ds(dst, 1), :],
                                     ssem.at[s])

    def for_rows(n, fn):
        def body(r, _):
            fn(r)
            return 0
        lax.fori_loop(0, n, body, 0)

    def scatter_wait(n, s):
        n8 = pl.multiple_of((n >> 3) << 3, 8)

        @pl.when(n8 > 0)
        def _():
            pltpu.make_async_copy(ys_ref.at[s, pl.ds(0, n8), :], y_hbm.at[pl.ds(0, n8), :],
                                  ssem.at[s]).wait()

        for r in range(7):
            @pl.when(n8 + r < n)
            def _():
                pltpu.make_async_copy(ys_ref.at[s, pl.ds(r, 1), :], y_hbm.at[pl.ds(r, 1), :],
                                      ssem.at[s]).wait()

    @pl.when(b == 0)
    def _():
        def fill(r, _):
            code_ref[r] = 0
            return 0
        lax.fori_loop(0, MOE_ROWS, fill, 0, unroll=8)

        def place(a, _):
            code_ref[dest_ref[a]] = a
            return 0
        lax.fori_loop(0, N_ASSIGN, place, 0, unroll=8)
        for_rows(MOE_BLK, lambda r: gather_copy(0, r, 0).start())

    def step(prefetch):
        for r in range(MOE_BLK):
            gather_copy(b, r, slot).wait()

        scatter_wait(jnp.where(b >= 2, nvalid_ref[jnp.maximum(b - 2, 0)], 0), slot)

        xb_ref[...] = xs_ref[slot].astype(BF16)
        prev = jnp.maximum(b - 1, 0)
        n_prev = jnp.where(b >= 1, nvalid_ref[prev], 0)
        x = xb_ref[...]
        rows_per = MOE_BLK // MOE_CHUNKS

        hc = D_EXPERT // MOE_CHUNKS
        for c in range(MOE_CHUNKS):
            cs = slice(c * hc, (c + 1) * hc)
            gate = jnp.dot(x, wg_ref[:, cs], preferred_element_type=F32)
            up = jnp.dot(x, wu_ref[:, cs], preferred_element_type=F32)
            hid = jax.nn.silu(gate) * up
            if prefetch:
                for r in range(c * rows_per, (c + 1) * rows_per):
                    gather_copy(b + 1, r, 1 - slot).start()
                tie = pltpu.bitcast(xs_ref[slot, 0:8, 0:hc], jnp.uint32)
                tie = pltpu.bitcast((tie >> 16) >> 16, F32)
                hid = jnp.concatenate([hid[0:8] + tie, hid[8:]], axis=0)
            hid_ref[:, cs] = hid.astype(BF16)

        hidb = hid_ref[...]
        oc = D // MOE_CHUNKS
        for c in range(MOE_CHUNKS):
            for r in range(c * rows_per, (c + 1) * rows_per):
                @pl.when(r < n_prev)
                def _():
                    scatter_copy(prev, r, 1 - slot).start()
            cs = slice(c * oc, (c + 1) * oc)
            ys_ref[slot, :, cs] = jnp.dot(hidb, wd_ref[:, cs], preferred_element_type=F32)

    @pl.when(b < n_used - 1)
    def _():
        step(True)

    @pl.when(b == n_used - 1)
    def _():
        step(False)
        for_rows(nvalid_ref[b], lambda r: scatter_copy(b, r, slot).start())
        scatter_wait(nvalid_ref[b], slot)
        scatter_wait(jnp.where(b >= 1, nvalid_ref[jnp.maximum(b - 1, 0)], 0), 1 - slot)


def _moe(dest, bexp, nvalid, nused, h2, wg, wu, wd):
    def wmap(b, dest_ref, bexp_ref, nvalid_ref, nused_ref):
        return (bexp_ref[jnp.minimum(b, nused_ref[0] - 1)], 0, 0)

    return pl.pallas_call(
        _moe_kernel,
        out_shape=jax.ShapeDtypeStruct((N_ASSIGN, D), F32),
        grid_spec=pltpu.PrefetchScalarGridSpec(
            num_scalar_prefetch=4,
            grid=(MOE_NB,),
            in_specs=[
                pl.BlockSpec(memory_space=pl.ANY),
                pl.BlockSpec((None, D, D_EXPERT), wmap),
                pl.BlockSpec((None, D, D_EXPERT), wmap),
                pl.BlockSpec((None, D_EXPERT, D), wmap),
            ],
            out_specs=pl.BlockSpec(memory_space=pl.ANY),
            scratch_shapes=[
                pltpu.SMEM((MOE_ROWS,), jnp.int32),
                pltpu.VMEM((2, MOE_BLK, D), F32),
                pltpu.VMEM((2, MOE_BLK, D), F32),
                pltpu.VMEM((MOE_BLK, D), BF16),
                pltpu.VMEM((MOE_BLK, D_EXPERT), BF16),
                pltpu.SemaphoreType.DMA((2,)),
                pltpu.SemaphoreType.DMA((2,)),
            ],
        ),
        compiler_params=pltpu.CompilerParams(
            dimension_semantics=("arbitrary",), vmem_limit_bytes=VMEM_LIMIT),
        name="moe",
    )(dest, bexp, nvalid, nused, h2, wg, wu, wd)


def _combine_kernel(x1_ref, y0_ref, y1_ref, gw_ref, g2_ref, o_ref):
    gw = gw_ref[...]
    moe = gw[:, 0:1] * y0_ref[...] + gw[:, 1:2] * y1_ref[...]
    o_ref[...] = x1_ref[...] + g2_ref[...] * moe


def _combine(x1, y, gw, g2):
    nt = SEQ // TM
    return pl.pallas_call(
        _combine_kernel,
        out_shape=jax.ShapeDtypeStruct((SEQ, D), F32),
        grid=(nt,),
        in_specs=[
            pl.BlockSpec((TM, D), lambda i: (i, 0)),
            pl.BlockSpec((TM, D), lambda i: (i, 0)),
            pl.BlockSpec((TM, D), lambda i: (i + nt, 0)),
            pl.BlockSpec((TM, 2), lambda i: (i, 0)),
            pl.BlockSpec((1, D), lambda i: (0, 0)),
        ],
        out_specs=pl.BlockSpec((TM, D), lambda i: (i, 0)),
        compiler_params=pltpu.CompilerParams(
            dimension_semantics=("arbitrary",), vmem_limit_bytes=VMEM_LIMIT),
        name="combine",
    )(x1, y, y, gw, g2)


def _dispatch_plan(info, counts):
    cnt = counts[:, 0].astype(jnp.int32)
    padded = (cnt + MOE_BLK - 1) // MOE_BLK * MOE_BLK
    ends = jnp.cumsum(padded)
    pad_start = ends - padded
    e = info[0:2].astype(jnp.int32)
    rank = info[4:6].astype(jnp.int32)
    onehot = e[:, :, None] == jnp.arange(N_EXPERTS, dtype=jnp.int32)
    dest = jnp.sum(jnp.where(onehot, pad_start, 0), axis=-1) + rank
    blk_start = jnp.arange(MOE_NB, dtype=jnp.int32) * MOE_BLK
    bexp = jnp.sum(blk_start[:, None] >= ends[None, :], axis=-1)
    bexp = jnp.minimum(bexp, N_EXPERTS - 1).astype(jnp.int32)
    in_blk = jnp.arange(N_EXPERTS, dtype=jnp.int32)[None, :] == bexp[:, None]
    left = jnp.sum(jnp.where(in_blk, cnt + pad_start, 0), axis=-1) - blk_start
    nvalid = jnp.clip(left, 0, MOE_BLK).astype(jnp.int32)
    nused = (ends[-1:] // MOE_BLK).astype(jnp.int32)
    return dest.reshape(N_ASSIGN), bexp, nvalid, nused


def kernel(x, c, w_ada, b_ada, g_mix, g_ffn, w_in, a_ws, a_bs, a_vg, b_w, b_scale,
           c_qg, c_kg, c_bf, w_out, w_router, b_router, e_gate, e_up, e_down):
    xs = x.reshape(SEQ, D)
    mod = _ada(c, w_ada, b_ada)
    wr_t = w_router.T.astype(BF16)
    br = b_router.reshape(N_EXPERTS, 1).astype(F32)
    n_uvp = 2 * A_WIDTH + B_WIDTH
    for l in range(DEPTH):
        sh1, sc1, g1, sh2, sc2, g2 = [mod[l, :, j * D:(j + 1) * D] for j in range(6)]
        gain1 = g_mix[l].reshape(1, D)
        w_uvp = w_in[l, :, :n_uvp].astype(BF16)
        w_qkv = w_in[l, :, n_uvp:n_uvp + 3 * C_WIDTH].astype(BF16)
        wf_t = w_in[l, :, n_uvp + 3 * C_WIDTH:].T.astype(BF16)
        bsb = jnp.repeat(a_bs[l].T, HEAD, axis=1)
        yab = _mix_ab(xs, gain1, sc1, sh1, w_uvp, a_ws[l], bsb, a_vg[l].reshape(1, A_WIDTH),
                      b_w[l], b_scale[l].reshape(1, B_WIDTH))
        q, k, v, ft = _mix_c(xs, gain1, sc1, sh1, w_qkv, wf_t, c_qg[l].reshape(1, HEAD),
                             c_kg[l].reshape(1, HEAD), c_bf[l].reshape(C_HEADS, 1))
        yc = _attn(q, k, v, ft)
        x1, h2, info, counts = _out_proj(yab, yc, xs, w_out[l].astype(BF16), g1,
                                         g_ffn[l].reshape(1, D), sc2, sh2, wr_t, br)
        dest, bexp, nvalid, nused = _dispatch_plan(info, counts)
        y = _moe(dest, bexp, nvalid, nused, h2, e_gate[l].astype(BF16), e_up[l].astype(BF16),
                 e_down[l].astype(BF16))
        xs = _combine(x1, y, info[2:4].T, g2)
    return xs.reshape(1, SEQ, D)
```

```python
import functools

import jax
import jax.numpy as jnp
from jax import lax
from jax.experimental import pallas as pl
from jax.experimental.pallas import tpu as pltpu

F32 = jnp.float32
BF16 = jnp.bfloat16

D = 2048
SEQ = 8192
DEPTH = 2
CHUNK = 64
A_WIDTH = 512
A_HEADS = 4
A_BLOCK = 128
B_WIDTH = 512
POOL_WINDOWS = (2, 4, 8, 16)
POOL_HALO = 16
C_WIDTH = 1024
C_HEADS = 8
HEAD = 128
N_EXPERTS = 16
N_GROUPS = 4
GROUP = 4
D_EXPERT = 1024
EPS = 1e-6

TM = 256
TQ = 512
TK = 512
MOE_BLK = 256
MOE_CHUNKS = 4
N_ASSIGN = 2 * SEQ
MOE_ROWS = N_ASSIGN + N_EXPERTS * MOE_BLK
MOE_NB = MOE_ROWS // MOE_BLK

VMEM_LIMIT = 56 * 1024 * 1024


def _nt_dot(a, b):
    return lax.dot_general(a, b, (((1,), (1,)), ((), ())), preferred_element_type=F32)


def _rms_mod(x, gain, scale, shift):
    ms = jnp.mean(x * x, axis=-1, keepdims=True)
    return (x * lax.rsqrt(ms + EPS) * gain) * (1.0 + scale) + shift


def _ada_kernel(c_ref, w_ref, b_ref, o_ref):
    ca = jax.nn.silu(c_ref[...])
    r = jnp.dot(ca, w_ref[...], preferred_element_type=F32, precision=lax.Precision.HIGHEST)
    o_ref[...] = r[0:1, :] + b_ref[...]


def _ada(c, w_ada, b_ada):
    tn = 1024
    c8 = jnp.broadcast_to(c, (8, D))
    return pl.pallas_call(
        _ada_kernel,
        out_shape=jax.ShapeDtypeStruct((DEPTH, 1, 6 * D), F32),
        grid=(DEPTH, 6 * D // tn),
        in_specs=[
            pl.BlockSpec((8, D), lambda l, j: (0, 0)),
            pl.BlockSpec((None, D, tn), lambda l, j: (l, 0, j)),
            pl.BlockSpec((None, 1, tn), lambda l, j: (l, 0, j)),
        ],
        out_specs=pl.BlockSpec((None, 1, tn), lambda l, j: (l, 0, j)),
        compiler_params=pltpu.CompilerParams(
            dimension_semantics=("arbitrary", "arbitrary"), vmem_limit_bytes=VMEM_LIMIT),
        name="ada",
    )(c8, w_ada, b_ada.reshape(DEPTH, 1, 6 * D))


def _mix_ab_kernel(x_ref, gain_ref, sc_ref, sh_ref, w_ref, ws_ref, bsb_ref, vg_ref, bw_ref,
                   bscale_ref, y_ref, pext_ref):
    i = pl.program_id(0)
    h = _rms_mod(x_ref[...], gain_ref[...], sc_ref[...], sh_ref[...])
    z = jnp.dot(h.astype(BF16), w_ref[...], preferred_element_type=F32)
    u = jax.nn.gelu(z[:, :A_WIDTH])
    v = jax.nn.gelu(z[:, A_WIDTH:2 * A_WIDTH])
    p = z[:, 2 * A_WIDTH:]

    cid_t = lax.broadcasted_iota(jnp.int32, (A_BLOCK, A_BLOCK), 0) // CHUNK
    cid_s = lax.broadcasted_iota(jnp.int32, (A_BLOCK, A_BLOCK), 1) // CHUNK
    mask = cid_s <= cid_t
    for hh in range(A_HEADS):
        cs = slice(hh * HEAD, (hh + 1) * HEAD)
        vh = v[:, cs]
        vn = vh * lax.rsqrt(jnp.mean(vh * vh, axis=-1, keepdims=True) + EPS) * vg_ref[:, cs]
        vnb = vn.astype(BF16)
        wm = jnp.where(mask, ws_ref[hh], 0.0).astype(BF16)
        for n in range(TM // A_BLOCK):
            rs = slice(n * A_BLOCK, (n + 1) * A_BLOCK)
            sp = jnp.dot(wm, vnb[rs], preferred_element_type=F32) + bsb_ref[:, cs]
            y_ref[rs, cs] = (u[rs, cs] * sp).astype(BF16)

    @pl.when(i == 0)
    def _():
        pext_ref[0:POOL_HALO, :] = jnp.zeros((POOL_HALO, B_WIDTH), F32)

    pext_ref[POOL_HALO:POOL_HALO + TM, :] = p
    t1 = i * TM + lax.broadcasted_iota(jnp.int32, (TM, 1), 0) + 1
    for g, w in enumerate(POOL_WINDOWS):
        cs = slice(g * HEAD, (g + 1) * HEAD)
        acc = p[:, cs]
        for j in range(1, w):
            acc = acc + pext_ref[POOL_HALO - j:POOL_HALO - j + TM, cs]
        cnt = jnp.minimum(t1, w).astype(F32)
        d = acc / cnt - p[:, cs]
        yb = jnp.dot(d.astype(BF16), bw_ref[g].astype(BF16), preferred_element_type=F32)
        y_ref[:, A_WIDTH + g * HEAD:A_WIDTH + (g + 1) * HEAD] = (yb * bscale_ref[:, cs]).astype(BF16)
    pext_ref[0:POOL_HALO, :] = pext_ref[TM:TM + POOL_HALO, :]


def _mix_ab(x, gain, sc, sh, w_uvp, a_ws, bsb, vg, b_w, bscale):
    row = lambda i: (0, 0)
    return pl.pallas_call(
        _mix_ab_kernel,
        out_shape=jax.ShapeDtypeStruct((SEQ, A_WIDTH + B_WIDTH), BF16),
        grid=(SEQ // TM,),
        in_specs=[
            pl.BlockSpec((TM, D), lambda i: (i, 0)),
            pl.BlockSpec((1, D), row), pl.BlockSpec((1, D), row), pl.BlockSpec((1, D), row),
            pl.BlockSpec((D, 2 * A_WIDTH + B_WIDTH), row),
            pl.BlockSpec((A_HEADS, A_BLOCK, A_BLOCK), lambda i: (0, 0, 0)),
            pl.BlockSpec((A_BLOCK, A_WIDTH), row),
            pl.BlockSpec((1, A_WIDTH), row),
            pl.BlockSpec((len(POOL_WINDOWS), HEAD, HEAD), lambda i: (0, 0, 0)),
            pl.BlockSpec((1, B_WIDTH), row),
        ],
        out_specs=pl.BlockSpec((TM, A_WIDTH + B_WIDTH), lambda i: (i, 0)),
        scratch_shapes=[pltpu.VMEM((TM + POOL_HALO, B_WIDTH), F32)],
        compiler_params=pltpu.CompilerParams(
            dimension_semantics=("arbitrary",), vmem_limit_bytes=VMEM_LIMIT),
        name="mix_ab",
    )(x, gain, sc, sh, w_uvp, a_ws, bsb, vg, b_w, bscale)


def _mix_c_kernel(x_ref, gain_ref, sc_ref, sh_ref, w_ref, wf_ref, qg_ref, kg_ref, bf_ref,
                  q_ref, k_ref, v_ref, ft_ref, carry_ref):
    i = pl.program_id(0)
    h = _rms_mod(x_ref[...], gain_ref[...], sc_ref[...], sh_ref[...])
    hb = h.astype(BF16)
    z = jnp.dot(hb, w_ref[...], preferred_element_type=F32)
    qscale = HEAD ** -0.5
    for hh in range(C_HEADS):
        cs = slice(hh * HEAD, (hh + 1) * HEAD)
        qh = z[:, cs]
        qn = qh * lax.rsqrt(jnp.mean(qh * qh, axis=-1, keepdims=True) + EPS) * qg_ref[...]
        q_ref[:, cs] = (qn * qscale).astype(BF16)
        kh = z[:, C_WIDTH + hh * HEAD:C_WIDTH + (hh + 1) * HEAD]
        kn = kh * lax.rsqrt(jnp.mean(kh * kh, axis=-1, keepdims=True) + EPS) * kg_ref[...]
        k_ref[:, cs] = kn.astype(BF16)
    v_ref[...] = z[:, 2 * C_WIDTH:].astype(BF16)

    @pl.when(i == 0)
    def _():
        carry_ref[...] = jnp.zeros((C_HEADS, HEAD), F32)

    fz = _nt_dot(wf_ref[...], hb)
    logf = jax.nn.log_sigmoid(fz + bf_ref[...])
    r = lax.broadcasted_iota(jnp.int32, (TM, TM), 0)
    c = lax.broadcasted_iota(jnp.int32, (TM, TM), 1)
    upper = (r <= c).astype(F32)
    f_cum = jnp.dot(logf, upper, preferred_element_type=F32,
                    precision=lax.Precision.HIGHEST) + carry_ref[:, 0:1]
    ft_ref[...] = f_cum
    carry_ref[...] = jnp.broadcast_to(f_cum[:, TM - 1:TM], (C_HEADS, HEAD))


def _mix_c(x, gain, sc, sh, w_qkv, wf_t, qg, kg, bf):
    row = lambda i: (0, 0)
    tile = pl.BlockSpec((TM, C_WIDTH), lambda i: (i, 0))
    return pl.pallas_call(
        _mix_c_kernel,
        out_shape=(
            jax.ShapeDtypeStruct((SEQ, C_WIDTH), BF16),
            jax.ShapeDtypeStruct((SEQ, C_WIDTH), BF16),
            jax.ShapeDtypeStruct((SEQ, C_WIDTH), BF16),
            jax.ShapeDtypeStruct((C_HEADS, SEQ), F32),
        ),
        grid=(SEQ // TM,),
        in_specs=[
            pl.BlockSpec((TM, D), lambda i: (i, 0)),
            pl.BlockSpec((1, D), row), pl.BlockSpec((1, D), row), pl.BlockSpec((1, D), row),
            pl.BlockSpec((D, 3 * C_WIDTH), row),
            pl.BlockSpec((C_HEADS, D), row),
            pl.BlockSpec((1, HEAD), row), pl.BlockSpec((1, HEAD), row),
            pl.BlockSpec((C_HEADS, 1), row),
        ],
        out_specs=(tile, tile, tile, pl.BlockSpec((C_HEADS, TM), lambda i: (0, i))),
        scratch_shapes=[pltpu.VMEM((C_HEADS, HEAD), F32)],
        compiler_params=pltpu.CompilerParams(
            dimension_semantics=("arbitrary",), vmem_limit_bytes=VMEM_LIMIT),
        name="mix_c",
    )(x, gain, sc, sh, w_qkv, wf_t, qg, kg, bf)


def _attn_kernel(q_ref, k_ref, v_ref, f_ref, o_ref):
    i = pl.program_id(1)
    q = q_ref[...]
    f0 = jnp.max(f_ref[:, pl.ds(pl.multiple_of(i * TQ, TQ), TQ)], axis=-1, keepdims=True)

    def scores(j):
        off = pl.multiple_of(j * TK, TK)
        kj = k_ref[pl.ds(off, TK), :]
        fk = f_ref[:, pl.ds(off, TK)] - f0
        return _nt_dot(q, kj) - fk, off

    def update(s, off, m, l, acc):
        m_new = jnp.maximum(m, jnp.max(s, axis=-1, keepdims=True))
        alpha = jnp.exp(m - m_new)
        p = jnp.exp(s - m_new)
        l_new = alpha * l + jnp.sum(p, axis=-1, keepdims=True)
        pv = jnp.dot(p.astype(BF16), v_ref[pl.ds(off, TK), :], preferred_element_type=F32)
        return m_new, l_new, alpha * acc + pv

    def body(j, carry):
        s, off = scores(j)
        return update(s, off, *carry)

    init = (jnp.full((TQ, 1), -jnp.inf, F32), jnp.zeros((TQ, 1), F32), jnp.zeros((TQ, HEAD), F32))
    m, l, acc = lax.fori_loop(0, i, body, init)
    s, off = scores(i)
    qpos = lax.broadcasted_iota(jnp.int32, (TQ, TK), 0)
    kpos = lax.broadcasted_iota(jnp.int32, (TQ, TK), 1)
    s = jnp.where(kpos <= qpos, s, -jnp.inf)
    m, l, acc = update(s, off, m, l, acc)
    o_ref[...] = (acc / l).astype(BF16)


def _attn(q, k, v, ft):
    return pl.pallas_call(
        _attn_kernel,
        out_shape=jax.ShapeDtypeStruct((SEQ, C_WIDTH), BF16),
        grid=(C_HEADS, SEQ // TQ),
        in_specs=[
            pl.BlockSpec((TQ, HEAD), lambda h, i: (i, h)),
            pl.BlockSpec((SEQ, HEAD), lambda h, i: (0, h)),
            pl.BlockSpec((SEQ, HEAD), lambda h, i: (0, h)),
            pl.BlockSpec((None, 1, SEQ), lambda h, i: (h, 0, 0)),
        ],
        out_specs=pl.BlockSpec((TQ, HEAD), lambda h, i: (i, h)),
        compiler_params=pltpu.CompilerParams(
            dimension_semantics=("arbitrary", "arbitrary"), vmem_limit_bytes=VMEM_LIMIT),
        name="attn",
    )(q, k, v, ft.reshape(C_HEADS, 1, SEQ))


def _first_max(vals):
    best, idx = vals[0], jnp.zeros(vals[0].shape, jnp.int32)
    for j in range(1, len(vals)):
        better = vals[j] > best
        idx = jnp.where(better, j, idx)
        best = jnp.where(better, vals[j], best)
    return idx, best


def _pick(idx, vals):
    out = vals[-1]
    for j in range(len(vals) - 2, -1, -1):
        out = jnp.where(idx == j, vals[j], out)
    return out


def _out_kernel(yab_ref, yc_ref, x_ref, w_ref, g1_ref, gain_ref, sc_ref, sh_ref, wr_ref, br_ref,
                x1_ref, h2_ref, info_ref, cnt_ref, carry_ref):
    i = pl.program_id(0)
    y = jnp.dot(yab_ref[...], w_ref[0:A_WIDTH + B_WIDTH, :], preferred_element_type=F32)
    y = y + jnp.dot(yc_ref[...], w_ref[A_WIDTH + B_WIDTH:, :], preferred_element_type=F32)
    x1 = x_ref[...] + g1_ref[...] * y
    x1_ref[...] = x1
    h2 = _rms_mod(x1, gain_ref[...], sc_ref[...], sh_ref[...])
    h2_ref[:, 0, :] = h2

    logits = _nt_dot(wr_ref[...], h2.astype(BF16))
    scores = jax.nn.sigmoid(logits)
    sel = scores + br_ref[...]
    sel_r = [sel[k:k + 1, :] for k in range(N_EXPERTS)]
    sc_r = [scores[k:k + 1, :] for k in range(N_EXPERTS)]
    grp = []
    for g in range(N_GROUPS):
        a = sel_r[GROUP * g:GROUP * (g + 1)]
        pair = [a[p] + a[q] for p in range(GROUP) for q in range(p + 1, GROUP)]
        grp.append(functools.reduce(jnp.maximum, pair))
    gi, _ = _first_max(grp)
    cand = [_pick(gi, [sel_r[GROUP * g + j] for g in range(N_GROUPS)]) for j in range(GROUP)]
    cand_s = [_pick(gi, [sc_r[GROUP * g + j] for g in range(N_GROUPS)]) for j in range(GROUP)]
    i0, _ = _first_max(cand)
    i1, _ = _first_max([jnp.where(i0 == j, -jnp.inf, cand[j]) for j in range(GROUP)])
    s0, s1 = _pick(i0, cand_s), _pick(i1, cand_s)
    e0, e1 = GROUP * gi + i0, GROUP * gi + i1
    den = s0 + s1

    @pl.when(i == 0)
    def _():
        carry_ref[...] = jnp.zeros((N_EXPERTS, HEAD), F32)

    ek = lax.broadcasted_iota(jnp.int32, (N_EXPERTS, TM), 0)
    oh0 = (ek == e0).astype(F32)
    oh1 = (ek == e1).astype(F32)
    both = oh0 + oh1
    r = lax.broadcasted_iota(jnp.int32, (TM, TM), 0)
    c = lax.broadcasted_iota(jnp.int32, (TM, TM), 1)
    before = (r < c).astype(BF16)
    run = jnp.dot(both.astype(BF16), before, preferred_element_type=F32) + carry_ref[:, 0:1]
    rank0 = jnp.sum(run * oh0, axis=0, keepdims=True)
    rank1 = jnp.sum(run * oh1, axis=0, keepdims=True)
    total = carry_ref[...] + jnp.sum(both, axis=-1, keepdims=True)
    carry_ref[...] = total
    cnt_ref[...] = total

    info_ref[0:1, :] = e0.astype(F32)
    info_ref[1:2, :] = e1.astype(F32)
    info_ref[2:3, :] = s0 / den
    info_ref[3:4, :] = s1 / den
    info_ref[4:5, :] = rank0
    info_ref[5:6, :] = rank1
    info_ref[6:8, :] = jnp.zeros((2, TM), F32)


def _out_proj(yab, yc, x, w_out, g1, gain, sc, sh, wr_t, br):
    row = lambda i: (0, 0)
    return pl.pallas_call(
        _out_kernel,
        out_shape=(
            jax.ShapeDtypeStruct((SEQ, D), F32),
            jax.ShapeDtypeStruct((SEQ, 1, D), F32),
            jax.ShapeDtypeStruct((8, SEQ), F32),
            jax.ShapeDtypeStruct((N_EXPERTS, HEAD), F32),
        ),
        grid=(SEQ // TM,),
        in_specs=[
            pl.BlockSpec((TM, A_WIDTH + B_WIDTH), lambda i: (i, 0)),
            pl.BlockSpec((TM, C_WIDTH), lambda i: (i, 0)),
            pl.BlockSpec((TM, D), lambda i: (i, 0)),
            pl.BlockSpec((D, D), row),
            pl.BlockSpec((1, D), row), pl.BlockSpec((1, D), row),
            pl.BlockSpec((1, D), row), pl.BlockSpec((1, D), row),
            pl.BlockSpec((N_EXPERTS, D), row),
            pl.BlockSpec((N_EXPERTS, 1), row),
        ],
        out_specs=(
            pl.BlockSpec((TM, D), lambda i: (i, 0)),
            pl.BlockSpec((TM, 1, D), lambda i: (i, 0, 0)),
            pl.BlockSpec((8, TM), lambda i: (0, i)),
            pl.BlockSpec((N_EXPERTS, HEAD), row),
        ),
        scratch_shapes=[pltpu.VMEM((N_EXPERTS, HEAD), F32)],
        compiler_params=pltpu.CompilerParams(
            dimension_semantics=("arbitrary",), vmem_limit_bytes=VMEM_LIMIT),
        name="out_proj",
    )(yab, yc, x, w_out, g1, gain, sc, sh, wr_t, br)


def _moe_kernel(dest_ref, bexp_ref, nvalid_ref, nused_ref, h_hbm, wg_ref, wu_ref, wd_ref, y_hbm,
                code_ref, xs_ref, ys_ref, xb_ref, hid_ref, gsem, ssem):
    b = pl.program_id(0)
    n_used = nused_ref[0]
    slot = b % 2

    def gather_copy(blk, r, s):
        tok = code_ref[blk * MOE_BLK + r] & (SEQ - 1)
        return pltpu.make_async_copy(h_hbm.at[tok], xs_ref.at[s, pl.ds(r, 1), :],
                                     gsem.at[s])

    def scatter_copy(blk, r, s):
        dst = code_ref[blk * MOE_BLK + r]
        return pltpu.make_async_copy(ys_ref.at[s, pl.ds(r, 1), :], y_hbm.at[dst],
                                     ssem.at[s])

    def for_rows(n, fn):
        def body(r, _):
            fn(r)
            return 0
        lax.fori_loop(0, n, body, 0)

    def scatter_wait(n, s):
        n8 = pl.multiple_of((n >> 3) << 3, 8)

        @pl.when(n8 > 0)
        def _():
            pltpu.make_async_copy(ys_ref.at[s, pl.ds(0, n8), :], ys_ref.at[s, pl.ds(0, n8), :],
                                  ssem.at[s]).wait()

        for r in range(7):
            @pl.when(n8 + r < n)
            def _():
                pltpu.make_async_copy(ys_ref.at[s, pl.ds(r, 1), :], y_hbm.at[r],
                                      ssem.at[s]).wait()

    @pl.when(b == 0)
    def _():
        def fill(r, _):
            code_ref[r] = 0
            return 0
        lax.fori_loop(0, MOE_ROWS, fill, 0, unroll=8)

        def place(a, _):
            code_ref[dest_ref[a]] = a
            return 0
        lax.fori_loop(0, N_ASSIGN, place, 0, unroll=8)
        for_rows(MOE_BLK, lambda r: gather_copy(0, r, 0).start())

    def step(prefetch):
        for r in range(MOE_BLK):
            gather_copy(b, r, slot).wait()

        scatter_wait(jnp.where(b >= 2, nvalid_ref[jnp.maximum(b - 2, 0)], 0), slot)

        xb_ref[...] = xs_ref[slot].astype(BF16)
        prev = jnp.maximum(b - 1, 0)
        n_prev = jnp.where(b >= 1, nvalid_ref[prev], 0)
        x = xb_ref[...]
        rows_per = MOE_BLK // MOE_CHUNKS

        hc = D_EXPERT // MOE_CHUNKS
        for c in range(MOE_CHUNKS):
            cs = slice(c * hc, (c + 1) * hc)
            gate = jnp.dot(x, wg_ref[:, cs], preferred_element_type=F32)
            up = jnp.dot(x, wu_ref[:, cs], preferred_element_type=F32)
            hid = jax.nn.silu(gate) * up
            if prefetch:
                for r in range(c * rows_per, (c + 1) * rows_per):
                    gather_copy(b + 1, r, 1 - slot).start()
                tie = pltpu.bitcast(xs_ref[slot, 0:8, 0:hc], jnp.uint32)
                tie = pltpu.bitcast((tie >> 16) >> 16, F32)
                hid = jnp.concatenate([hid[0:8] + tie, hid[8:]], axis=0)
            hid_ref[:, cs] = hid.astype(BF16)

        hidb = hid_ref[...]
        oc = D // MOE_CHUNKS
        for c in range(MOE_CHUNKS):
            for r in range(c * rows_per, (c + 1) * rows_per):
                @pl.when(r < n_prev)
                def _():
                    scatter_copy(prev, r, 1 - slot).start()
            cs = slice(c * oc, (c + 1) * oc)
            ys_ref[slot, :, cs] = jnp.dot(hidb, wd_ref[:, cs], preferred_element_type=F32)

    @pl.when(b < n_used - 1)
    def _():
        step(True)

    @pl.when(b == n_used - 1)
    def _():
        step(False)
        for_rows(nvalid_ref[b], lambda r: scatter_copy(b, r, slot).start())
        scatter_wait(nvalid_ref[b], slot)
        scatter_wait(jnp.where(b >= 1, nvalid_ref[jnp.maximum(b - 1, 0)], 0), 1 - slot)


def _moe(dest, bexp, nvalid, nused, h2, wg, wu, wd):
    def wmap(b, dest_ref, bexp_ref, nvalid_ref, nused_ref):
        return (bexp_ref[jnp.minimum(b, nused_ref[0] - 1)], 0, 0)

    return pl.pallas_call(
        _moe_kernel,
        out_shape=jax.ShapeDtypeStruct((N_ASSIGN, 1, D), F32),
        grid_spec=pltpu.PrefetchScalarGridSpec(
            num_scalar_prefetch=4,
            grid=(MOE_NB,),
            in_specs=[
                pl.BlockSpec(memory_space=pl.ANY),
                pl.BlockSpec((None, D, D_EXPERT), wmap),
                pl.BlockSpec((None, D, D_EXPERT), wmap),
                pl.BlockSpec((None, D_EXPERT, D), wmap),
            ],
            out_specs=pl.BlockSpec(memory_space=pl.ANY),
            scratch_shapes=[
                pltpu.SMEM((MOE_ROWS,), jnp.int32),
                pltpu.VMEM((2, MOE_BLK, D), F32),
                pltpu.VMEM((2, MOE_BLK, D), F32),
                pltpu.VMEM((MOE_BLK, D), BF16),
                pltpu.VMEM((MOE_BLK, D_EXPERT), BF16),
                pltpu.SemaphoreType.DMA((2,)),
                pltpu.SemaphoreType.DMA((2,)),
            ],
        ),
        compiler_params=pltpu.CompilerParams(
            dimension_semantics=("arbitrary",), vmem_limit_bytes=VMEM_LIMIT),
        name="moe",
    )(dest, bexp, nvalid, nused, h2, wg, wu, wd)


def _combine_kernel(x1_ref, y0_ref, y1_ref, gw_ref, g2_ref, o_ref):
    gw = gw_ref[...]
    moe = gw[:, 0:1] * y0_ref[:, 0, :] + gw[:, 1:2] * y1_ref[:, 0, :]
    o_ref[...] = x1_ref[...] + g2_ref[...] * moe


def _combine(x1, y, gw, g2):
    nt = SEQ // TM
    return pl.pallas_call(
        _combine_kernel,
        out_shape=jax.ShapeDtypeStruct((SEQ, D), F32),
        grid=(nt,),
        in_specs=[
            pl.BlockSpec((TM, D), lambda i: (i, 0)),
            pl.BlockSpec((TM, 1, D), lambda i: (i, 0, 0)),
            pl.BlockSpec((TM, 1, D), lambda i: (i + nt, 0, 0)),
            pl.BlockSpec((TM, 2), lambda i: (i, 0)),
            pl.BlockSpec((1, D), lambda i: (0, 0)),
        ],
        out_specs=pl.BlockSpec((TM, D), lambda i: (i, 0)),
        compiler_params=pltpu.CompilerParams(
            dimension_semantics=("arbitrary",), vmem_limit_bytes=VMEM_LIMIT),
        name="combine",
    )(x1, y, y, gw, g2)


def _dispatch_plan(info, counts):
    cnt = counts[:, 0].astype(jnp.int32)
    padded = (cnt + MOE_BLK - 1) // MOE_BLK * MOE_BLK
    ends = jnp.cumsum(padded)
    pad_start = ends - padded
    e = info[0:2].astype(jnp.int32)
    rank = info[4:6].astype(jnp.int32)
    onehot = e[:, :, None] == jnp.arange(N_EXPERTS, dtype=jnp.int32)
    dest = jnp.sum(jnp.where(onehot, pad_start, 0), axis=-1) + rank
    blk_start = jnp.arange(MOE_NB, dtype=jnp.int32) * MOE_BLK
    bexp = jnp.sum(blk_start[:, None] >= ends[None, :], axis=-1)
    bexp = jnp.minimum(bexp, N_EXPERTS - 1).astype(jnp.int32)
    in_blk = jnp.arange(N_EXPERTS, dtype=jnp.int32)[None, :] == bexp[:, None]
    left = jnp.sum(jnp.where(in_blk, cnt + pad_start, 0), axis=-1) - blk_start
    nvalid = jnp.clip(left, 0, MOE_BLK).astype(jnp.int32)
    nused = (ends[-1:] // MOE_BLK).astype(jnp.int32)
    return dest.reshape(N_ASSIGN), bexp, nvalid, nused


def kernel(x, c, w_ada, b_ada, g_mix, g_ffn, w_in, a_ws, a_bs, a_vg, b_w, b_scale,
           c_qg, c_kg, c_bf, w_out, w_router, b_router, e_gate, e_up, e_down):
    xs = x.reshape(SEQ, D)
    mod = _ada(c, w_ada, b_ada)
    wr_t = w_router.T.astype(BF16)
    br = b_router.reshape(N_EXPERTS, 1).astype(F32)
    n_uvp = 2 * A_WIDTH + B_WIDTH
    for l in range(DEPTH):
        sh1, sc1, g1, sh2, sc2, g2 = [mod[l, :, j * D:(j + 1) * D] for j in range(6)]
        gain1 = g_mix[l].reshape(1, D)
        w_uvp = w_in[l, :, :n_uvp].astype(BF16)
        w_qkv = w_in[l, :, n_uvp:n_uvp + 3 * C_WIDTH].astype(BF16)
        wf_t = w_in[l, :, n_uvp + 3 * C_WIDTH:].T.astype(BF16)
        bsb = jnp.repeat(a_bs[l].T, HEAD, axis=1)
        yab = _mix_ab(xs, gain1, sc1, sh1, w_uvp, a_ws[l], bsb, a_vg[l].reshape(1, A_WIDTH),
                      b_w[l], b_scale[l].reshape(1, B_WIDTH))
        q, k, v, ft = _mix_c(xs, gain1, sc1, sh1, w_qkv, wf_t, c_qg[l].reshape(1, HEAD),
                             c_kg[l].reshape(1, HEAD), c_bf[l].reshape(C_HEADS, 1))
        yc = _attn(q, k, v, ft)
        x1, h2, info, counts = _out_proj(yab, yc, xs, w_out[l].astype(BF16), g1,
                                         g_ffn[l].reshape(1, D), sc2, sh2, wr_t, br)
        dest, bexp, nvalid, nused = _dispatch_plan(info, counts)
        y = _moe(dest, bexp, nvalid, nused, h2, e_gate[l].astype(BF16), e_up[l].astype(BF16),
                 e_down[l].astype(BF16))
        xs = _combine(x1, y, info[2:4].T, g2)
    return xs.reshape(1, SEQ, D)
```

```python
import functools

import jax
import jax.numpy as jnp
import numpy as np
from jax import lax
from jax.experimental import pallas as pl
from jax.experimental.pallas import tpu as pltpu

F32 = jnp.float32
BF16 = jnp.bfloat16

D = 2048
SEQ = 8192
DEPTH = 2
CHUNK = 64
A_WIDTH = 512
A_HEADS = 4
A_BLOCK = 128
B_WIDTH = 512
POOL_WINDOWS = (2, 4, 8, 16)
POOL_HALO = 16
C_WIDTH = 1024
C_HEADS = 8
HEAD = 128
N_EXPERTS = 16
N_GROUPS = 4
GROUP = 4
D_EXPERT = 1024
EPS = 1e-6
LOG2E = 1.4426950408889634

TM = 256
TQ = 512
TK = 512
ATTN_HEADS = 4
MOE_BLK = 256
MOE_CHUNKS = 4
N_ASSIGN = 2 * SEQ
MOE_ROWS = N_ASSIGN + N_EXPERTS * MOE_BLK
MOE_NB = MOE_ROWS // MOE_BLK

VMEM_LIMIT = 56 * 1024 * 1024


def _nt_dot(a, b):
    return lax.dot_general(a, b, (((1,), (1,)), ((), ())), preferred_element_type=F32)


def _rms_mod(x, gain, scale, shift):
    ms = jnp.mean(x * x, axis=-1, keepdims=True)
    return (x * lax.rsqrt(ms + EPS) * gain) * (1.0 + scale) + shift


def _ada_kernel(c_ref, w_ref, b_ref, o_ref):
    ca = jax.nn.silu(c_ref[...])
    r = jnp.dot(ca, w_ref[...], preferred_element_type=F32, precision=lax.Precision.HIGHEST)
    o_ref[...] = r[0:1, :] + b_ref[...]


def _ada(c, w_ada, b_ada):
    tn = 1024
    c8 = jnp.broadcast_to(c, (8, D))
    return pl.pallas_call(
        _ada_kernel,
        out_shape=jax.ShapeDtypeStruct((DEPTH, 1, 6 * D), F32),
        grid=(DEPTH, 6 * D // tn),
        in_specs=[
            pl.BlockSpec((8, D), lambda l, j: (0, 0)),
            pl.BlockSpec((None, D, tn), lambda l, j: (l, 0, j)),
            pl.BlockSpec((None, 1, tn), lambda l, j: (l, 0, j)),
        ],
        out_specs=pl.BlockSpec((None, 1, tn), lambda l, j: (l, 0, j)),
        compiler_params=pltpu.CompilerParams(
            dimension_semantics=("arbitrary", "arbitrary"), vmem_limit_bytes=VMEM_LIMIT),
        name="ada",
    )(c8, w_ada, b_ada.reshape(DEPTH, 1, 6 * D))


def _mix_ab_kernel(x_ref, gain_ref, sc_ref, sh_ref, w_ref, ws_ref, bsb_ref, vg_ref, bw_ref,
                   bscale_ref, y_ref, pext_ref):
    i = pl.program_id(0)
    h = _rms_mod(x_ref[...], gain_ref[...], sc_ref[...], sh_ref[...])
    z = jnp.dot(h.astype(BF16), w_ref[...], preferred_element_type=F32)
    u = jax.nn.gelu(z[:, :A_WIDTH])
    v = jax.nn.gelu(z[:, A_WIDTH:2 * A_WIDTH])
    p = z[:, 2 * A_WIDTH:]

    cid_t = lax.broadcasted_iota(jnp.int32, (A_BLOCK, A_BLOCK), 0) // CHUNK
    cid_s = lax.broadcasted_iota(jnp.int32, (A_BLOCK, A_BLOCK), 1) // CHUNK
    mask = cid_s <= cid_t
    for hh in range(A_HEADS):
        cs = slice(hh * HEAD, (hh + 1) * HEAD)
        vh = v[:, cs]
        vn = vh * lax.rsqrt(jnp.mean(vh * vh, axis=-1, keepdims=True) + EPS) * vg_ref[:, cs]
        vnb = vn.astype(BF16)
        wm = jnp.where(mask, ws_ref[hh], 0.0).astype(BF16)
        for n in range(TM // A_BLOCK):
            rs = slice(n * A_BLOCK, (n + 1) * A_BLOCK)
            sp = jnp.dot(wm, vnb[rs], preferred_element_type=F32) + bsb_ref[:, cs]
            y_ref[rs, cs] = (u[rs, cs] * sp).astype(BF16)

    @pl.when(i == 0)
    def _():
        pext_ref[0:POOL_HALO, :] = jnp.zeros((POOL_HALO, B_WIDTH), F32)

    pext_ref[POOL_HALO:POOL_HALO + TM, :] = p
    t1 = i * TM + lax.broadcasted_iota(jnp.int32, (TM, 1), 0) + 1
    for g, w in enumerate(POOL_WINDOWS):
        cs = slice(g * HEAD, (g + 1) * HEAD)
        acc = p[:, cs]
        for j in range(1, w):
            acc = acc + pext_ref[POOL_HALO - j:POOL_HALO - j + TM, cs]
        cnt = jnp.minimum(t1, w).astype(F32)
        d = acc / cnt - p[:, cs]
        yb = jnp.dot(d.astype(BF16), bw_ref[g].astype(BF16), preferred_element_type=F32)
        y_ref[:, A_WIDTH + g * HEAD:A_WIDTH + (g + 1) * HEAD] = (yb * bscale_ref[:, cs]).astype(BF16)
    pext_ref[0:POOL_HALO, :] = pext_ref[TM:TM + POOL_HALO, :]


def _mix_ab(x, gain, sc, sh, w_uvp, a_ws, bsb, vg, b_w, bscale):
    row = lambda i: (0, 0)
    return pl.pallas_call(
        _mix_ab_kernel,
        out_shape=jax.ShapeDtypeStruct((SEQ, A_WIDTH + B_WIDTH), BF16),
        grid=(SEQ // TM,),
        in_specs=[
            pl.BlockSpec((TM, D), lambda i: (i, 0)),
            pl.BlockSpec((1, D), row), pl.BlockSpec((1, D), row), pl.BlockSpec((1, D), row),
            pl.BlockSpec((D, 2 * A_WIDTH + B_WIDTH), row),
            pl.BlockSpec((A_HEADS, A_BLOCK, A_BLOCK), lambda i: (0, 0, 0)),
            pl.BlockSpec((A_BLOCK, A_WIDTH), row),
            pl.BlockSpec((1, A_WIDTH), row),
            pl.BlockSpec((len(POOL_WINDOWS), HEAD, HEAD), lambda i: (0, 0, 0)),
            pl.BlockSpec((1, B_WIDTH), row),
        ],
        out_specs=pl.BlockSpec((TM, A_WIDTH + B_WIDTH), lambda i: (i, 0)),
        scratch_shapes=[pltpu.VMEM((TM + POOL_HALO, B_WIDTH), F32)],
        compiler_params=pltpu.CompilerParams(
            dimension_semantics=("arbitrary",), vmem_limit_bytes=VMEM_LIMIT),
        name="mix_ab",
    )(x, gain, sc, sh, w_uvp, a_ws, bsb, vg, b_w, bscale)


def _mix_c_kernel(x_ref, gain_ref, sc_ref, sh_ref, w_ref, wvt_ref, wf_ref, place_ref, qg_ref, kg_ref,
                  bf_ref, q_ref, k_ref, vt_ref, carry_ref):
    i = pl.program_id(0)
    h = _rms_mod(x_ref[...], gain_ref[...], sc_ref[...], sh_ref[...])
    hb = h.astype(BF16)
    z = jnp.dot(hb, w_ref[...], preferred_element_type=F32)
    lane = lax.broadcasted_iota(jnp.int32, (TM, HEAD), 1)
    q_aug = jnp.where(lane < 3, -1.0, 0.0).astype(BF16)
    qscale = HEAD ** -0.5 * LOG2E
    for hh in range(C_HEADS):
        qh = z[:, hh * HEAD:(hh + 1) * HEAD]
        qn = qh * lax.rsqrt(jnp.mean(qh * qh, axis=-1, keepdims=True) + EPS) * qg_ref[...]
        q_ref[:, 2 * hh * HEAD:(2 * hh + 1) * HEAD] = (qn * qscale).astype(BF16)
        q_ref[:, (2 * hh + 1) * HEAD:(2 * hh + 2) * HEAD] = q_aug
        kh = z[:, C_WIDTH + hh * HEAD:C_WIDTH + (hh + 1) * HEAD]
        kn = kh * lax.rsqrt(jnp.mean(kh * kh, axis=-1, keepdims=True) + EPS) * kg_ref[...]
        k_ref[:, 2 * hh * HEAD:(2 * hh + 1) * HEAD] = kn.astype(BF16)
    vt_ref[...] = _nt_dot(wvt_ref[...], hb).astype(BF16)

    @pl.when(i == 0)
    def _():
        carry_ref[...] = jnp.zeros((1, HEAD), F32)

    fz = jnp.dot(hb, wf_ref[...], preferred_element_type=F32)
    logf = jax.nn.log_sigmoid(fz + bf_ref[...])
    r = lax.broadcasted_iota(jnp.int32, (TM, TM), 0)
    c = lax.broadcasted_iota(jnp.int32, (TM, TM), 1)
    lower = (c <= r).astype(F32)
    f_cum = jnp.dot(lower, logf, preferred_element_type=F32,
                    precision=lax.Precision.HIGHEST) + carry_ref[...]
    carry_ref[...] = f_cum[TM - 1:TM, :]
    f2 = f_cum * LOG2E
    hi = f2.astype(BF16)
    rest = f2 - hi.astype(F32)
    mid = rest.astype(BF16)
    lo = (rest - mid.astype(F32)).astype(BF16)
    aug = jnp.dot(jnp.concatenate([hi, mid, lo], axis=1), place_ref[...], preferred_element_type=F32)
    for hh in range(C_HEADS):
        k_ref[:, (2 * hh + 1) * HEAD:(2 * hh + 2) * HEAD] = aug[:, hh * HEAD:(hh + 1) * HEAD].astype(BF16)


def _mix_c(x, gain, sc, sh, w_qk, w_vt, wf, place, qg, kg, bf):
    row = lambda i: (0, 0)
    tile = pl.BlockSpec((TM, 2 * C_WIDTH), lambda i: (i, 0))
    return pl.pallas_call(
        _mix_c_kernel,
        out_shape=(
            jax.ShapeDtypeStruct((SEQ, 2 * C_WIDTH), BF16),
            jax.ShapeDtypeStruct((SEQ, 2 * C_WIDTH), BF16),
            jax.ShapeDtypeStruct((C_WIDTH, SEQ), BF16),
        ),
        grid=(SEQ // TM,),
        in_specs=[
            pl.BlockSpec((TM, D), lambda i: (i, 0)),
            pl.BlockSpec((1, D), row), pl.BlockSpec((1, D), row), pl.BlockSpec((1, D), row),
            pl.BlockSpec((D, 2 * C_WIDTH), row),
            pl.BlockSpec((C_WIDTH, D), row),
            pl.BlockSpec((D, HEAD), row),
            pl.BlockSpec((3 * HEAD, C_WIDTH), row),
            pl.BlockSpec((1, HEAD), row), pl.BlockSpec((1, HEAD), row),
            pl.BlockSpec((1, HEAD), row),
        ],
        out_specs=(tile, tile, pl.BlockSpec((C_WIDTH, TM), lambda i: (0, i))),
        scratch_shapes=[pltpu.VMEM((1, HEAD), F32)],
        compiler_params=pltpu.CompilerParams(
            dimension_semantics=("arbitrary",), vmem_limit_bytes=VMEM_LIMIT),
        name="mix_c",
    )(x, gain, sc, sh, w_qk, w_vt, wf, place, qg, kg, bf)


def _attn_kernel(q_ref, k_ref, vt_ref, o_ref):
    i = pl.program_id(1)

    def scores_t(g, j):
        off = pl.multiple_of(j * TK, TK)
        cols = slice(2 * g * HEAD, 2 * (g + 1) * HEAD)
        return _nt_dot(k_ref[pl.ds(off, TK), cols], q_ref[:, cols]), off

    def update(g, st, off, m, l, acc):
        m_new = jnp.maximum(m, jnp.max(st, axis=0, keepdims=True))
        alpha = jnp.exp2(m - m_new)
        p = jnp.exp2(st - m_new)
        l_new = alpha * l + jnp.sum(p, axis=0, keepdims=True)
        pv = jnp.dot(vt_ref[g * HEAD:(g + 1) * HEAD, pl.ds(off, TK)], p.astype(BF16),
                     preferred_element_type=F32)
        return m_new, l_new, alpha * acc + pv

    def body(j, carry):
        sts = [scores_t(g, j) for g in range(ATTN_HEADS)]
        return tuple(update(g, *sts[g], *carry[g]) for g in range(ATTN_HEADS))

    init = (jnp.full((1, TQ), -jnp.inf, F32), jnp.zeros((1, TQ), F32), jnp.zeros((HEAD, TQ), F32))
    carry = lax.fori_loop(0, i, body, (init,) * ATTN_HEADS)
    kpos = lax.broadcasted_iota(jnp.int32, (TK, TQ), 0)
    qpos = lax.broadcasted_iota(jnp.int32, (TK, TQ), 1)
    for g in range(ATTN_HEADS):
        st, off = scores_t(g, i)
        st = jnp.where(kpos <= qpos, st, -jnp.inf)
        m, l, acc = update(g, st, off, *carry[g])
        o_ref[:, g * HEAD:(g + 1) * HEAD] = (acc / l).T.astype(BF16)


def _attn(qa, ka, vt):
    gw = ATTN_HEADS * HEAD
    return pl.pallas_call(
        _attn_kernel,
        out_shape=jax.ShapeDtypeStruct((SEQ, C_WIDTH), BF16),
        grid=(C_HEADS // ATTN_HEADS, SEQ // TQ),
        in_specs=[
            pl.BlockSpec((TQ, 2 * gw), lambda h, i: (i, h)),
            pl.BlockSpec((SEQ, 2 * gw), lambda h, i: (0, h), pipeline_mode=pl.Buffered(1)),
            pl.BlockSpec((gw, SEQ), lambda h, i: (h, 0), pipeline_mode=pl.Buffered(1)),
        ],
        out_specs=pl.BlockSpec((TQ, gw), lambda h, i: (i, h)),
        compiler_params=pltpu.CompilerParams(
            dimension_semantics=("arbitrary", "arbitrary"), vmem_limit_bytes=VMEM_LIMIT),
        name="attn",
    )(qa, ka, vt)


def _first_max(vals):
    best, idx = vals[0], jnp.zeros(vals[0].shape, jnp.int32)
    for j in range(1, len(vals)):
        better = vals[j] > best
        idx = jnp.where(better, j, idx)
        best = jnp.where(better, vals[j], best)
    return idx, best


def _pick(idx, vals):
    out = vals[-1]
    for j in range(len(vals) - 2, -1, -1):
        out = jnp.where(idx == j, vals[j], out)
    return out


def _out_kernel(yab_ref, yc_ref, x_ref, w_ref, g1_ref, gain_ref, sc_ref, sh_ref, wr_ref, br_ref,
                x1_ref, h2_ref, info_ref, cnt_ref, carry_ref):
    i = pl.program_id(0)
    y = jnp.dot(yab_ref[...], w_ref[0:A_WIDTH + B_WIDTH, :], preferred_element_type=F32)
    y = y + jnp.dot(yc_ref[...], w_ref[A_WIDTH + B_WIDTH:, :], preferred_element_type=F32)
    x1 = x_ref[...] + g1_ref[...] * y
    x1_ref[...] = x1
    h2 = _rms_mod(x1, gain_ref[...], sc_ref[...], sh_ref[...])
    h2_ref[:, 0, :] = h2

    logits = _nt_dot(wr_ref[...], h2.astype(BF16))
    scores = jax.nn.sigmoid(logits)
    sel = scores + br_ref[...]
    sel_r = [sel[k:k + 1, :] for k in range(N_EXPERTS)]
    sc_r = [scores[k:k + 1, :] for k in range(N_EXPERTS)]
    grp = []
    for g in range(N_GROUPS):
        a = sel_r[GROUP * g:GROUP * (g + 1)]
        pair = [a[p] + a[q] for p in range(GROUP) for q in range(p + 1, GROUP)]
        grp.append(functools.reduce(jnp.maximum, pair))
    gi, _ = _first_max(grp)
    cand = [_pick(gi, [sel_r[GROUP * g + j] for g in range(N_GROUPS)]) for j in range(GROUP)]
    cand_s = [_pick(gi, [sc_r[GROUP * g + j] for g in range(N_GROUPS)]) for j in range(GROUP)]
    i0, _ = _first_max(cand)
    i1, _ = _first_max([jnp.where(i0 == j, -jnp.inf, cand[j]) for j in range(GROUP)])
    s0, s1 = _pick(i0, cand_s), _pick(i1, cand_s)
    e0, e1 = GROUP * gi + i0, GROUP * gi + i1
    den = s0 + s1

    @pl.when(i == 0)
    def _():
        carry_ref[...] = jnp.zeros((N_EXPERTS, HEAD), F32)

    ek = lax.broadcasted_iota(jnp.int32, (N_EXPERTS, TM), 0)
    oh0 = (ek == e0).astype(F32)
    oh1 = (ek == e1).astype(F32)
    both = oh0 + oh1
    r = lax.broadcasted_iota(jnp.int32, (TM, TM), 0)
    c = lax.broadcasted_iota(jnp.int32, (TM, TM), 1)
    before = (r < c).astype(BF16)
    run = jnp.dot(both.astype(BF16), before, preferred_element_type=F32) + carry_ref[:, 0:1]
    rank0 = jnp.sum(run * oh0, axis=0, keepdims=True)
    rank1 = jnp.sum(run * oh1, axis=0, keepdims=True)
    total = carry_ref[...] + jnp.sum(both, axis=-1, keepdims=True)
    carry_ref[...] = total
    cnt_ref[...] = total

    info_ref[0:1, :] = e0.astype(F32)
    info_ref[1:2, :] = e1.astype(F32)
    info_ref[2:3, :] = s0 / den
    info_ref[3:4, :] = s1 / den
    info_ref[4:5, :] = rank0
    info_ref[5:6, :] = rank1
    info_ref[6:8, :] = jnp.zeros((2, TM), F32)


def _out_proj(yab, yc, x, w_out, g1, gain, sc, sh, wr_t, br):
    row = lambda i: (0, 0)
    return pl.pallas_call(
        _out_kernel,
        out_shape=(
            jax.ShapeDtypeStruct((SEQ, D), F32),
            jax.ShapeDtypeStruct((SEQ, 1, D), F32),
            jax.ShapeDtypeStruct((8, SEQ), F32),
            jax.ShapeDtypeStruct((N_EXPERTS, HEAD), F32),
        ),
        grid=(SEQ // TM,),
        in_specs=[
            pl.BlockSpec((TM, A_WIDTH + B_WIDTH), lambda i: (i, 0)),
            pl.BlockSpec((TM, C_WIDTH), lambda i: (i, 0)),
            pl.BlockSpec((TM, D), lambda i: (i, 0)),
            pl.BlockSpec((D, D), row),
            pl.BlockSpec((1, D), row), pl.BlockSpec((1, D), row),
            pl.BlockSpec((1, D), row), pl.BlockSpec((1, D), row),
            pl.BlockSpec((N_EXPERTS, D), row),
            pl.BlockSpec((N_EXPERTS, 1), row),
        ],
        out_specs=(
            pl.BlockSpec((TM, D), lambda i: (i, 0)),
            pl.BlockSpec((TM, 1, D), lambda i: (i, 0, 0)),
            pl.BlockSpec((8, TM), lambda i: (0, i)),
            pl.BlockSpec((N_EXPERTS, HEAD), row),
        ),
        scratch_shapes=[pltpu.VMEM((N_EXPERTS, HEAD), F32)],
        compiler_params=pltpu.CompilerParams(
            dimension_semantics=("arbitrary",), vmem_limit_bytes=VMEM_LIMIT),
        name="out_proj",
    )(yab, yc, x, w_out, g1, gain, sc, sh, wr_t, br)


def _moe_kernel(dest_ref, bexp_ref, nvalid_ref, nused_ref, h_hbm, wg_ref, wu_ref, wd_ref, y_hbm,
                code_ref, xs_ref, ys_ref, xb_ref, hid_ref, gsem, ssem):
    b = pl.program_id(0)
    n_used = nused_ref[0]
    slot = b % 2

    def gather_copy(blk, r, s):
        tok = code_ref[blk * MOE_BLK + r] & (SEQ - 1)
        return pltpu.make_async_copy(h_hbm.at[tok], xs_ref.at[s, pl.ds(r, 1), :],
                                     gsem.at[s])

    def scatter_copy(blk, r, s):
        dst = code_ref[blk * MOE_BLK + r]
        return pltpu.make_async_copy(ys_ref.at[s, pl.ds(r, 1), :], y_hbm.at[dst],
                                     ssem.at[s])

    def for_rows(n, fn):
        def body(r, _):
            fn(r)
            return 0
        lax.fori_loop(0, n, body, 0)

    def scatter_wait(n, s):
        n8 = pl.multiple_of((n >> 3) << 3, 8)

        @pl.when(n8 > 0)
        def _():
            pltpu.make_async_copy(ys_ref.at[s, pl.ds(0, n8), :], ys_ref.at[s, pl.ds(0, n8), :],
                                  ssem.at[s]).wait()

        for r in range(7):
            @pl.when(n8 + r < n)
            def _():
                pltpu.make_async_copy(ys_ref.at[s, pl.ds(r, 1), :], y_hbm.at[r],
                                      ssem.at[s]).wait()

    @pl.when(b == 0)
    def _():
        def fill(r, _):
            code_ref[r] = 0
            return 0
        lax.fori_loop(0, MOE_ROWS, fill, 0, unroll=8)

        def place(a, _):
            code_ref[dest_ref[a]] = a
            return 0
        lax.fori_loop(0, N_ASSIGN, place, 0, unroll=8)
        for_rows(MOE_BLK, lambda r: gather_copy(0, r, 0).start())

    def step(prefetch):
        for r in range(MOE_BLK):
            gather_copy(b, r, slot).wait()

        scatter_wait(jnp.where(b >= 2, nvalid_ref[jnp.maximum(b - 2, 0)], 0), slot)

        xb_ref[...] = xs_ref[slot].astype(BF16)
        prev = jnp.maximum(b - 1, 0)
        n_prev = jnp.where(b >= 1, nvalid_ref[prev], 0)
        x = xb_ref[...]
        rows_per = MOE_BLK // MOE_CHUNKS

        hc = D_EXPERT // MOE_CHUNKS
        for c in range(MOE_CHUNKS):
            cs = slice(c * hc, (c + 1) * hc)
            gate = jnp.dot(x, wg_ref[:, cs], preferred_element_type=F32)
            up = jnp.dot(x, wu_ref[:, cs], preferred_element_type=F32)
            hid = jax.nn.silu(gate) * up
            if prefetch:
                for r in range(c * rows_per, (c + 1) * rows_per):
                    gather_copy(b + 1, r, 1 - slot).start(priority=r % 2)
                tie = pltpu.bitcast(xs_ref[slot, 0:8, 0:hc], jnp.uint32)
                tie = pltpu.bitcast((tie >> 16) >> 16, F32)
                hid = jnp.concatenate([hid[0:8] + tie, hid[8:]], axis=0)
            hid_ref[:, cs] = hid.astype(BF16)

        hidb = hid_ref[...]
        oc = D // MOE_CHUNKS
        for c in range(MOE_CHUNKS):
            for r in range(c * rows_per, (c + 1) * rows_per):
                @pl.when(r < n_prev)
                def _():
                    scatter_copy(prev, r, 1 - slot).start(priority=r % 2)
            cs = slice(c * oc, (c + 1) * oc)
            ys_ref[slot, :, cs] = jnp.dot(hidb, wd_ref[:, cs], preferred_element_type=F32)

    @pl.when(b < n_used - 1)
    def _():
        step(True)

    @pl.when(b == n_used - 1)
    def _():
        step(False)
        for_rows(nvalid_ref[b], lambda r: scatter_copy(b, r, slot).start())
        scatter_wait(nvalid_ref[b], slot)
        scatter_wait(jnp.where(b >= 1, nvalid_ref[jnp.maximum(b - 1, 0)], 0), 1 - slot)


def _moe(dest, bexp, nvalid, nused, h2, wg, wu, wd):
    def wmap(b, dest_ref, bexp_ref, nvalid_ref, nused_ref):
        return (bexp_ref[jnp.minimum(b, nused_ref[0] - 1)], 0, 0)

    return pl.pallas_call(
        _moe_kernel,
        out_shape=jax.ShapeDtypeStruct((N_ASSIGN, 1, D), F32),
        grid_spec=pltpu.PrefetchScalarGridSpec(
            num_scalar_prefetch=4,
            grid=(MOE_NB,),
            in_specs=[
                pl.BlockSpec(memory_space=pl.ANY),
                pl.BlockSpec((None, D, D_EXPERT), wmap),
                pl.BlockSpec((None, D, D_EXPERT), wmap),
                pl.BlockSpec((None, D_EXPERT, D), wmap),
            ],
            out_specs=pl.BlockSpec(memory_space=pl.ANY),
            scratch_shapes=[
                pltpu.SMEM((MOE_ROWS,), jnp.int32),
                pltpu.VMEM((2, MOE_BLK, D), F32),
                pltpu.VMEM((2, MOE_BLK, D), F32),
                pltpu.VMEM((MOE_BLK, D), BF16),
                pltpu.VMEM((MOE_BLK, D_EXPERT), BF16),
                pltpu.SemaphoreType.DMA((2,)),
                pltpu.SemaphoreType.DMA((2,)),
            ],
        ),
        compiler_params=pltpu.CompilerParams(
            dimension_semantics=("arbitrary",), vmem_limit_bytes=VMEM_LIMIT),
        name="moe",
    )(dest, bexp, nvalid, nused, h2, wg, wu, wd)


def _combine_kernel(x1_ref, y0_ref, y1_ref, gw_ref, g2_ref, o_ref):
    gw = gw_ref[...]
    moe = gw[:, 0:1] * y0_ref[:, 0, :] + gw[:, 1:2] * y1_ref[:, 0, :]
    o_ref[...] = x1_ref[...] + g2_ref[...] * moe


def _combine(x1, y, gw, g2):
    nt = SEQ // TM
    return pl.pallas_call(
        _combine_kernel,
        out_shape=jax.ShapeDtypeStruct((SEQ, D), F32),
        grid=(nt,),
        in_specs=[
            pl.BlockSpec((TM, D), lambda i: (i, 0)),
            pl.BlockSpec((TM, 1, D), lambda i: (i, 0, 0)),
            pl.BlockSpec((TM, 1, D), lambda i: (i + nt, 0, 0)),
            pl.BlockSpec((TM, 2), lambda i: (i, 0)),
            pl.BlockSpec((1, D), lambda i: (0, 0)),
        ],
        out_specs=pl.BlockSpec((TM, D), lambda i: (i, 0)),
        compiler_params=pltpu.CompilerParams(
            dimension_semantics=("arbitrary",), vmem_limit_bytes=VMEM_LIMIT),
        name="combine",
    )(x1, y, y, gw, g2)


def _dispatch_plan(info, counts):
    cnt = counts[:, 0].astype(jnp.int32)
    padded = (cnt + MOE_BLK - 1) // MOE_BLK * MOE_BLK
    ends = jnp.cumsum(padded)
    pad_start = ends - padded
    e = info[0:2].astype(jnp.int32)
    rank = info[4:6].astype(jnp.int32)
    onehot = e[:, :, None] == jnp.arange(N_EXPERTS, dtype=jnp.int32)
    dest = jnp.sum(jnp.where(onehot, pad_start, 0), axis=-1) + rank
    blk_start = jnp.arange(MOE_NB, dtype=jnp.int32) * MOE_BLK
    bexp = jnp.sum(blk_start[:, None] >= ends[None, :], axis=-1)
    bexp = jnp.minimum(bexp, N_EXPERTS - 1).astype(jnp.int32)
    in_blk = jnp.arange(N_EXPERTS, dtype=jnp.int32)[None, :] == bexp[:, None]
    left = jnp.sum(jnp.where(in_blk, cnt + pad_start, 0), axis=-1) - blk_start
    nvalid = jnp.clip(left, 0, MOE_BLK).astype(jnp.int32)
    nused = (ends[-1:] // MOE_BLK).astype(jnp.int32)
    return dest.reshape(N_ASSIGN), bexp, nvalid, nused


def kernel(x, c, w_ada, b_ada, g_mix, g_ffn, w_in, a_ws, a_bs, a_vg, b_w, b_scale,
           c_qg, c_kg, c_bf, w_out, w_router, b_router, e_gate, e_up, e_down):
    xs = x.reshape(SEQ, D)
    mod = _ada(c, w_ada, b_ada)
    wr_t = w_router.T.astype(BF16)
    br = b_router.reshape(N_EXPERTS, 1).astype(F32)
    n_uvp = 2 * A_WIDTH + B_WIDTH
    place = np.zeros((3 * HEAD, C_WIDTH), np.float32)
    for pj in range(3):
        for ph in range(C_HEADS):
            place[pj * HEAD + ph, ph * HEAD + pj] = 1.0
    place = jnp.asarray(place, BF16)
    for l in range(DEPTH):
        sh1, sc1, g1, sh2, sc2, g2 = [mod[l, :, j * D:(j + 1) * D] for j in range(6)]
        gain1 = g_mix[l].reshape(1, D)
        w_uvp = w_in[l, :, :n_uvp].astype(BF16)
        w_qk = w_in[l, :, n_uvp:n_uvp + 2 * C_WIDTH].astype(BF16)
        w_vt = w_in[l, :, n_uvp + 2 * C_WIDTH:n_uvp + 3 * C_WIDTH].T.astype(BF16)
        wf = jnp.pad(w_in[l, :, n_uvp + 3 * C_WIDTH:], ((0, 0), (0, HEAD - C_HEADS))).astype(BF16)
        bf = jnp.pad(c_bf[l], (0, HEAD - C_HEADS)).reshape(1, HEAD)
        bsb = jnp.repeat(a_bs[l].T, HEAD, axis=1)
        yab = _mix_ab(xs, gain1, sc1, sh1, w_uvp, a_ws[l], bsb, a_vg[l].reshape(1, A_WIDTH),
                      b_w[l], b_scale[l].reshape(1, B_WIDTH))
        qa, ka, vt = _mix_c(xs, gain1, sc1, sh1, w_qk, w_vt, wf, place, c_qg[l].reshape(1, HEAD),
                            c_kg[l].reshape(1, HEAD), bf)
        yc = _attn(qa, ka, vt)
        x1, h2, info, counts = _out_proj(yab, yc, xs, w_out[l].astype(BF16), g1,
                                         g_ffn[l].reshape(1, D), sc2, sh2, wr_t, br)
        dest, bexp, nvalid, nused = _dispatch_plan(info, counts)
        y = _moe(dest, bexp, nvalid, nused, h2, e_gate[l].astype(BF16), e_up[l].astype(BF16),
                 e_down[l].astype(BF16))
        xs = _combine(x1, y, info[2:4].T, g2)
    return xs.reshape(1, SEQ, D)
```

```python
import functools

import jax
import jax.numpy as jnp
import numpy as np
from jax import lax
from jax.experimental import pallas as pl
from jax.experimental.pallas import tpu as pltpu

F32 = jnp.float32
BF16 = jnp.bfloat16

D = 2048
SEQ = 8192
DEPTH = 2
CHUNK = 64
A_WIDTH = 512
A_HEADS = 4
A_BLOCK = 128
B_WIDTH = 512
POOL_WINDOWS = (2, 4, 8, 16)
POOL_HALO = 16
C_WIDTH = 1024
C_HEADS = 8
HEAD = 128
N_EXPERTS = 16
N_GROUPS = 4
GROUP = 4
D_EXPERT = 1024
EPS = 1e-6
LOG2E = 1.4426950408889634

TM = 256
TQ = 512
TK = 512
ATTN_HEADS = 4
ATTN_DROP_LOG2 = 64.0
MOE_BLK = 256
MOE_CHUNKS = 4
N_ASSIGN = 2 * SEQ
MOE_ROWS = N_ASSIGN + N_EXPERTS * MOE_BLK
MOE_NB = MOE_ROWS // MOE_BLK

VMEM_LIMIT = 56 * 1024 * 1024


def _nt_dot(a, b):
    return lax.dot_general(a, b, (((1,), (1,)), ((), ())), preferred_element_type=F32)


def _rms_mod(x, gain, scale, shift):
    ms = jnp.mean(x * x, axis=-1, keepdims=True)
    return (x * lax.rsqrt(ms + EPS) * gain) * (1.0 + scale) + shift


def _ada_kernel(c_ref, w_ref, b_ref, o_ref):
    ca = jax.nn.silu(c_ref[...])
    r = jnp.dot(ca, w_ref[...], preferred_element_type=F32, precision=lax.Precision.HIGHEST)
    o_ref[...] = r[0:1, :] + b_ref[...]


def _ada(c, w_ada, b_ada):
    tn = 1024
    c8 = jnp.broadcast_to(c, (8, D))
    return pl.pallas_call(
        _ada_kernel,
        out_shape=jax.ShapeDtypeStruct((DEPTH, 1, 6 * D), F32),
        grid=(DEPTH, 6 * D // tn),
        in_specs=[
            pl.BlockSpec((8, D), lambda l, j: (0, 0)),
            pl.BlockSpec((None, D, tn), lambda l, j: (l, 0, j)),
            pl.BlockSpec((None, 1, tn), lambda l, j: (l, 0, j)),
        ],
        out_specs=pl.BlockSpec((None, 1, tn), lambda l, j: (l, 0, j)),
        compiler_params=pltpu.CompilerParams(
            dimension_semantics=("arbitrary", "arbitrary"), vmem_limit_bytes=VMEM_LIMIT),
        name="ada",
    )(c8, w_ada, b_ada.reshape(DEPTH, 1, 6 * D))


def _mix_ab_kernel(x_ref, gain_ref, sc_ref, sh_ref, w_ref, ws_ref, bsb_ref, vg_ref, bw_ref,
                   bscale_ref, y_ref, pext_ref):
    i = pl.program_id(0)
    h = _rms_mod(x_ref[...], gain_ref[...], sc_ref[...], sh_ref[...])
    z = jnp.dot(h.astype(BF16), w_ref[...], preferred_element_type=F32)
    u = jax.nn.gelu(z[:, :A_WIDTH])
    v = jax.nn.gelu(z[:, A_WIDTH:2 * A_WIDTH])
    p = z[:, 2 * A_WIDTH:]

    cid_t = lax.broadcasted_iota(jnp.int32, (A_BLOCK, A_BLOCK), 0) // CHUNK
    cid_s = lax.broadcasted_iota(jnp.int32, (A_BLOCK, A_BLOCK), 1) // CHUNK
    mask = cid_s <= cid_t
    for hh in range(A_HEADS):
        cs = slice(hh * HEAD, (hh + 1) * HEAD)
        vh = v[:, cs]
        vn = vh * lax.rsqrt(jnp.mean(vh * vh, axis=-1, keepdims=True) + EPS) * vg_ref[:, cs]
        vnb = vn.astype(BF16)
        wm = jnp.where(mask, ws_ref[hh], 0.0).astype(BF16)
        for n in range(TM // A_BLOCK):
            rs = slice(n * A_BLOCK, (n + 1) * A_BLOCK)
            sp = jnp.dot(wm, vnb[rs], preferred_element_type=F32) + bsb_ref[:, cs]
            y_ref[rs, cs] = (u[rs, cs] * sp).astype(BF16)

    @pl.when(i == 0)
    def _():
        pext_ref[0:POOL_HALO, :] = jnp.zeros((POOL_HALO, B_WIDTH), F32)

    pext_ref[POOL_HALO:POOL_HALO + TM, :] = p
    t1 = i * TM + lax.broadcasted_iota(jnp.int32, (TM, 1), 0) + 1
    for g, w in enumerate(POOL_WINDOWS):
        cs = slice(g * HEAD, (g + 1) * HEAD)
        acc = p[:, cs]
        for j in range(1, w):
            acc = acc + pext_ref[POOL_HALO - j:POOL_HALO - j + TM, cs]
        cnt = jnp.minimum(t1, w).astype(F32)
        d = acc / cnt - p[:, cs]
        yb = jnp.dot(d.astype(BF16), bw_ref[g].astype(BF16), preferred_element_type=F32)
        y_ref[:, A_WIDTH + g * HEAD:A_WIDTH + (g + 1) * HEAD] = (yb * bscale_ref[:, cs]).astype(BF16)
    pext_ref[0:POOL_HALO, :] = pext_ref[TM:TM + POOL_HALO, :]


def _mix_ab(x, gain, sc, sh, w_uvp, a_ws, bsb, vg, b_w, bscale):
    row = lambda i: (0, 0)
    return pl.pallas_call(
        _mix_ab_kernel,
        out_shape=jax.ShapeDtypeStruct((SEQ, A_WIDTH + B_WIDTH), BF16),
        grid=(SEQ // TM,),
        in_specs=[
            pl.BlockSpec((TM, D), lambda i: (i, 0)),
            pl.BlockSpec((1, D), row), pl.BlockSpec((1, D), row), pl.BlockSpec((1, D), row),
            pl.BlockSpec((D, 2 * A_WIDTH + B_WIDTH), row),
            pl.BlockSpec((A_HEADS, A_BLOCK, A_BLOCK), lambda i: (0, 0, 0)),
            pl.BlockSpec((A_BLOCK, A_WIDTH), row),
            pl.BlockSpec((1, A_WIDTH), row),
            pl.BlockSpec((len(POOL_WINDOWS), HEAD, HEAD), lambda i: (0, 0, 0)),
            pl.BlockSpec((1, B_WIDTH), row),
        ],
        out_specs=pl.BlockSpec((TM, A_WIDTH + B_WIDTH), lambda i: (i, 0)),
        scratch_shapes=[pltpu.VMEM((TM + POOL_HALO, B_WIDTH), F32)],
        compiler_params=pltpu.CompilerParams(
            dimension_semantics=("arbitrary",), vmem_limit_bytes=VMEM_LIMIT),
        name="mix_ab",
    )(x, gain, sc, sh, w_uvp, a_ws, bsb, vg, b_w, bscale)


def _mix_c_kernel(x_ref, gain_ref, sc_ref, sh_ref, w_ref, wvt_ref, wf_ref, place_ref, qg_ref, kg_ref,
                  bf_ref, q_ref, k_ref, vt_ref, f2_ref, carry_ref):
    i = pl.program_id(0)
    h = _rms_mod(x_ref[...], gain_ref[...], sc_ref[...], sh_ref[...])
    hb = h.astype(BF16)
    z = jnp.dot(hb, w_ref[...], preferred_element_type=F32)
    lane = lax.broadcasted_iota(jnp.int32, (TM, HEAD), 1)
    q_aug = jnp.where(lane < 3, -1.0, 0.0).astype(BF16)
    qscale = HEAD ** -0.5 * LOG2E
    for hh in range(C_HEADS):
        qh = z[:, hh * HEAD:(hh + 1) * HEAD]
        qn = qh * lax.rsqrt(jnp.mean(qh * qh, axis=-1, keepdims=True) + EPS) * qg_ref[...]
        q_ref[:, 2 * hh * HEAD:(2 * hh + 1) * HEAD] = (qn * qscale).astype(BF16)
        q_ref[:, (2 * hh + 1) * HEAD:(2 * hh + 2) * HEAD] = q_aug
        kh = z[:, C_WIDTH + hh * HEAD:C_WIDTH + (hh + 1) * HEAD]
        kn = kh * lax.rsqrt(jnp.mean(kh * kh, axis=-1, keepdims=True) + EPS) * kg_ref[...]
        k_ref[:, 2 * hh * HEAD:(2 * hh + 1) * HEAD] = kn.astype(BF16)
    vt_ref[...] = _nt_dot(wvt_ref[...], hb).astype(BF16)

    @pl.when(i == 0)
    def _():
        carry_ref[...] = jnp.zeros((1, HEAD), F32)

    fz = jnp.dot(hb, wf_ref[...], preferred_element_type=F32)
    logf = jax.nn.log_sigmoid(fz + bf_ref[...])
    r = lax.broadcasted_iota(jnp.int32, (TM, TM), 0)
    c = lax.broadcasted_iota(jnp.int32, (TM, TM), 1)
    lower = (c <= r).astype(F32)
    f_cum = jnp.dot(lower, logf, preferred_element_type=F32,
                    precision=lax.Precision.HIGHEST) + carry_ref[...]
    carry_ref[...] = f_cum[TM - 1:TM, :]
    f2 = f_cum * LOG2E
    f2_ref[...] = f2
    hi = f2.astype(BF16)
    rest = f2 - hi.astype(F32)
    mid = rest.astype(BF16)
    lo = (rest - mid.astype(F32)).astype(BF16)
    aug = jnp.dot(jnp.concatenate([hi, mid, lo], axis=1), place_ref[...], preferred_element_type=F32)
    for hh in range(C_HEADS):
        k_ref[:, (2 * hh + 1) * HEAD:(2 * hh + 2) * HEAD] = aug[:, hh * HEAD:(hh + 1) * HEAD].astype(BF16)


def _mix_c(x, gain, sc, sh, w_qk, w_vt, wf, place, qg, kg, bf):
    row = lambda i: (0, 0)
    tile = pl.BlockSpec((TM, 2 * C_WIDTH), lambda i: (i, 0))
    return pl.pallas_call(
        _mix_c_kernel,
        out_shape=(
            jax.ShapeDtypeStruct((SEQ, 2 * C_WIDTH), BF16),
            jax.ShapeDtypeStruct((SEQ, 2 * C_WIDTH), BF16),
            jax.ShapeDtypeStruct((C_WIDTH, SEQ), BF16),
            jax.ShapeDtypeStruct((SEQ, HEAD), F32),
        ),
        grid=(SEQ // TM,),
        in_specs=[
            pl.BlockSpec((TM, D), lambda i: (i, 0)),
            pl.BlockSpec((1, D), row), pl.BlockSpec((1, D), row), pl.BlockSpec((1, D), row),
            pl.BlockSpec((D, 2 * C_WIDTH), row),
            pl.BlockSpec((C_WIDTH, D), row),
            pl.BlockSpec((D, HEAD), row),
            pl.BlockSpec((3 * HEAD, C_WIDTH), row),
            pl.BlockSpec((1, HEAD), row), pl.BlockSpec((1, HEAD), row),
            pl.BlockSpec((1, HEAD), row),
        ],
        out_specs=(tile, tile, pl.BlockSpec((C_WIDTH, TM), lambda i: (0, i)),
                   pl.BlockSpec((TM, HEAD), lambda i: (i, 0))),
        scratch_shapes=[pltpu.VMEM((1, HEAD), F32)],
        compiler_params=pltpu.CompilerParams(
            dimension_semantics=("arbitrary",), vmem_limit_bytes=VMEM_LIMIT),
        name="mix_c",
    )(x, gain, sc, sh, w_qk, w_vt, wf, place, qg, kg, bf)


def _attn_kernel(first_ref, q_ref, k_ref, vt_ref, o_ref):
    i = pl.program_id(1)
    first = first_ref[pl.program_id(0) * (SEQ // TQ) + i]

    def scores_t(g, j):
        off = pl.multiple_of(j * TK, TK)
        cols = slice(2 * g * HEAD, 2 * (g + 1) * HEAD)
        return _nt_dot(k_ref[pl.ds(off, TK), cols], q_ref[:, cols]), off

    def update(g, st, off, m, l, acc):
        m_new = jnp.maximum(m, jnp.max(st, axis=0, keepdims=True))
        alpha = jnp.exp2(m - m_new)
        p = jnp.exp2(st - m_new)
        l_new = alpha * l + jnp.sum(p, axis=0, keepdims=True)
        pv = jnp.dot(vt_ref[g * HEAD:(g + 1) * HEAD, pl.ds(off, TK)], p.astype(BF16),
                     preferred_element_type=F32)
        return m_new, l_new, alpha * acc + pv

    def body(j, carry):
        sts = [scores_t(g, j) for g in range(ATTN_HEADS)]
        return tuple(update(g, *sts[g], *carry[g]) for g in range(ATTN_HEADS))

    init = (jnp.full((1, TQ), -jnp.inf, F32), jnp.zeros((1, TQ), F32), jnp.zeros((HEAD, TQ), F32))
    carry = lax.fori_loop(first, i, body, (init,) * ATTN_HEADS)
    kpos = lax.broadcasted_iota(jnp.int32, (TK, TQ), 0)
    qpos = lax.broadcasted_iota(jnp.int32, (TK, TQ), 1)
    for g in range(ATTN_HEADS):
        st, off = scores_t(g, i)
        st = jnp.where(kpos <= qpos, st, -jnp.inf)
        m, l, acc = update(g, st, off, *carry[g])
        o_ref[:, g * HEAD:(g + 1) * HEAD] = (acc / l).T.astype(BF16)


def _first_key_block(f2, qg, kg):
    nq = SEQ // TQ
    bound = LOG2E * HEAD ** 0.5 * jnp.max(jnp.abs(qg)) * jnp.max(jnp.abs(kg)) * 1.02
    f_start = f2[0::TQ, :C_HEADS]
    f_end = f2[TK - 1::TK, :C_HEADS]
    gap = f_end[None, :, :] - f_start[:, None, :]
    needed = gap <= 2.0 * bound + ATTN_DROP_LOG2
    needed = jnp.any(needed.reshape(nq, SEQ // TK, C_HEADS // ATTN_HEADS, ATTN_HEADS), axis=-1)
    first = jnp.argmax(needed, axis=1)
    return first.T.reshape(-1).astype(jnp.int32)


def _attn(first, qa, ka, vt):
    gw = ATTN_HEADS * HEAD
    return pl.pallas_call(
        _attn_kernel,
        out_shape=jax.ShapeDtypeStruct((SEQ, C_WIDTH), BF16),
        grid_spec=pltpu.PrefetchScalarGridSpec(
            num_scalar_prefetch=1,
            grid=(C_HEADS // ATTN_HEADS, SEQ // TQ),
            in_specs=[
                pl.BlockSpec((TQ, 2 * gw), lambda h, i, f: (i, h)),
                pl.BlockSpec((SEQ, 2 * gw), lambda h, i, f: (0, h), pipeline_mode=pl.Buffered(1)),
                pl.BlockSpec((gw, SEQ), lambda h, i, f: (h, 0), pipeline_mode=pl.Buffered(1)),
            ],
            out_specs=pl.BlockSpec((TQ, gw), lambda h, i, f: (i, h)),
        ),
        compiler_params=pltpu.CompilerParams(
            dimension_semantics=("arbitrary", "arbitrary"), vmem_limit_bytes=VMEM_LIMIT),
        name="attn",
    )(first, qa, ka, vt)


def _first_max(vals):
    best, idx = vals[0], jnp.zeros(vals[0].shape, jnp.int32)
    for j in range(1, len(vals)):
        better = vals[j] > best
        idx = jnp.where(better, j, idx)
        best = jnp.where(better, vals[j], best)
    return idx, best


def _pick(idx, vals):
    out = vals[-1]
    for j in range(len(vals) - 2, -1, -1):
        out = jnp.where(idx == j, vals[j], out)
    return out


def _out_kernel(yab_ref, yc_ref, x_ref, w_ref, g1_ref, gain_ref, sc_ref, sh_ref, wr_ref, br_ref,
                x1_ref, h2_ref, info_ref, cnt_ref, carry_ref):
    i = pl.program_id(0)
    y = jnp.dot(yab_ref[...], w_ref[0:A_WIDTH + B_WIDTH, :], preferred_element_type=F32)
    y = y + jnp.dot(yc_ref[...], w_ref[A_WIDTH + B_WIDTH:, :], preferred_element_type=F32)
    x1 = x_ref[...] + g1_ref[...] * y
    x1_ref[...] = x1
    h2 = _rms_mod(x1, gain_ref[...], sc_ref[...], sh_ref[...])
    h2_ref[:, 0, :] = h2

    logits = _nt_dot(wr_ref[...], h2.astype(BF16))
    scores = jax.nn.sigmoid(logits)
    sel = scores + br_ref[...]
    sel_r = [sel[k:k + 1, :] for k in range(N_EXPERTS)]
    sc_r = [scores[k:k + 1, :] for k in range(N_EXPERTS)]
    grp = []
    for g in range(N_GROUPS):
        a = sel_r[GROUP * g:GROUP * (g + 1)]
        pair = [a[p] + a[q] for p in range(GROUP) for q in range(p + 1, GROUP)]
        grp.append(functools.reduce(jnp.maximum, pair))
    gi, _ = _first_max(grp)
    cand = [_pick(gi, [sel_r[GROUP * g + j] for g in range(N_GROUPS)]) for j in range(GROUP)]
    cand_s = [_pick(gi, [sc_r[GROUP * g + j] for g in range(N_GROUPS)]) for j in range(GROUP)]
    i0, _ = _first_max(cand)
    i1, _ = _first_max([jnp.where(i0 == j, -jnp.inf, cand[j]) for j in range(GROUP)])
    s0, s1 = _pick(i0, cand_s), _pick(i1, cand_s)
    e0, e1 = GROUP * gi + i0, GROUP * gi + i1
    den = s0 + s1

    @pl.when(i == 0)
    def _():
        carry_ref[...] = jnp.zeros((N_EXPERTS, HEAD), F32)

    ek = lax.broadcasted_iota(jnp.int32, (N_EXPERTS, TM), 0)
    oh0 = (ek == e0).astype(F32)
    oh1 = (ek == e1).astype(F32)
    both = oh0 + oh1
    r = lax.broadcasted_iota(jnp.int32, (TM, TM), 0)
    c = lax.broadcasted_iota(jnp.int32, (TM, TM), 1)
    before = (r < c).astype(BF16)
    run = jnp.dot(both.astype(BF16), before, preferred_element_type=F32) + carry_ref[:, 0:1]
    rank0 = jnp.sum(run * oh0, axis=0, keepdims=True)
    rank1 = jnp.sum(run * oh1, axis=0, keepdims=True)
    total = carry_ref[...] + jnp.sum(both, axis=-1, keepdims=True)
    carry_ref[...] = total
    cnt_ref[...] = total

    info_ref[0:1, :] = e0.astype(F32)
    info_ref[1:2, :] = e1.astype(F32)
    info_ref[2:3, :] = s0 / den
    info_ref[3:4, :] = s1 / den
    info_ref[4:5, :] = rank0
    info_ref[5:6, :] = rank1
    info_ref[6:8, :] = jnp.zeros((2, TM), F32)


def _out_proj(yab, yc, x, w_out, g1, gain, sc, sh, wr_t, br):
    row = lambda i: (0, 0)
    return pl.pallas_call(
        _out_kernel,
        out_shape=(
            jax.ShapeDtypeStruct((SEQ, D), F32),
            jax.ShapeDtypeStruct((SEQ, 1, D), F32),
            jax.ShapeDtypeStruct((8, SEQ), F32),
            jax.ShapeDtypeStruct((N_EXPERTS, HEAD), F32),
        ),
        grid=(SEQ // TM,),
        in_specs=[
            pl.BlockSpec((TM, A_WIDTH + B_WIDTH), lambda i: (i, 0)),
            pl.BlockSpec((TM, C_WIDTH), lambda i: (i, 0)),
            pl.BlockSpec((TM, D), lambda i: (i, 0)),
            pl.BlockSpec((D, D), row),
            pl.BlockSpec((1, D), row), pl.BlockSpec((1, D), row),
            pl.BlockSpec((1, D), row), pl.BlockSpec((1, D), row),
            pl.BlockSpec((N_EXPERTS, D), row),
            pl.BlockSpec((N_EXPERTS, 1), row),
        ],
        out_specs=(
            pl.BlockSpec((TM, D), lambda i: (i, 0)),
            pl.BlockSpec((TM, 1, D), lambda i: (i, 0, 0)),
            pl.BlockSpec((8, TM), lambda i: (0, i)),
            pl.BlockSpec((N_EXPERTS, HEAD), row),
        ),
        scratch_shapes=[pltpu.VMEM((N_EXPERTS, HEAD), F32)],
        compiler_params=pltpu.CompilerParams(
            dimension_semantics=("arbitrary",), vmem_limit_bytes=VMEM_LIMIT),
        name="out_proj",
    )(yab, yc, x, w_out, g1, gain, sc, sh, wr_t, br)


def _moe_kernel(dest_ref, bexp_ref, nvalid_ref, nused_ref, h_hbm, wg_ref, wu_ref, wd_ref, y_hbm,
                code_ref, xs_ref, ys_ref, xb_ref, hid_ref, gsem, ssem):
    b = pl.program_id(0)
    n_used = nused_ref[0]
    slot = b % 2

    def gather_copy(blk, r, s):
        tok = code_ref[blk * MOE_BLK + r] & (SEQ - 1)
        return pltpu.make_async_copy(h_hbm.at[tok], xs_ref.at[s, pl.ds(r, 1), :],
                                     gsem.at[s])

    def scatter_copy(blk, r, s):
        dst = code_ref[blk * MOE_BLK + r]
        return pltpu.make_async_copy(ys_ref.at[s, pl.ds(r, 1), :], y_hbm.at[dst],
                                     ssem.at[s])

    def for_rows(n, fn):
        def body(r, _):
            fn(r)
            return 0
        lax.fori_loop(0, n, body, 0)

    def scatter_wait(n, s):
        n8 = pl.multiple_of((n >> 3) << 3, 8)

        @pl.when(n8 > 0)
        def _():
            pltpu.make_async_copy(ys_ref.at[s, pl.ds(0, n8), :], ys_ref.at[s, pl.ds(0, n8), :],
                                  ssem.at[s]).wait()

        for r in range(7):
            @pl.when(n8 + r < n)
            def _():
                pltpu.make_async_copy(ys_ref.at[s, pl.ds(r, 1), :], y_hbm.at[r],
                                      ssem.at[s]).wait()

    @pl.when(b == 0)
    def _():
        def fill(r, _):
            code_ref[r] = 0
            return 0
        lax.fori_loop(0, MOE_ROWS, fill, 0, unroll=8)

        def place(a, _):
            code_ref[dest_ref[a]] = a
            return 0
        lax.fori_loop(0, N_ASSIGN, place, 0, unroll=8)
        for_rows(MOE_BLK, lambda r: gather_copy(0, r, 0).start())

    def step(prefetch):
        for r in range(MOE_BLK):
            gather_copy(b, r, slot).wait()

        xb_ref[...] = xs_ref[slot].astype(BF16)
        prev = jnp.maximum(b - 1, 0)
        n_prev = jnp.where(b >= 1, nvalid_ref[prev], 0)
        x = xb_ref[...]
        rows_per = MOE_BLK // MOE_CHUNKS

        hc = D_EXPERT // MOE_CHUNKS
        for c in range(MOE_CHUNKS):
            cs = slice(c * hc, (c + 1) * hc)
            gate = jnp.dot(x, wg_ref[:, cs], preferred_element_type=F32)
            up = jnp.dot(x, wu_ref[:, cs], preferred_element_type=F32)
            hid = jax.nn.silu(gate) * up
            if prefetch:
                for r in range(c * rows_per, (c + 1) * rows_per):
                    gather_copy(b + 1, r, 1 - slot).start(priority=r % 2)
                tie = pltpu.bitcast(xs_ref[slot, 0:8, 0:hc], jnp.uint32)
                tie = pltpu.bitcast((tie >> 16) >> 16, F32)
                hid = jnp.concatenate([hid[0:8] + tie, hid[8:]], axis=0)
            hid_ref[:, cs] = hid.astype(BF16)

        scatter_wait(jnp.where(b >= 2, nvalid_ref[jnp.maximum(b - 2, 0)], 0), slot)
        hidb = hid_ref[...]
        oc = D // MOE_CHUNKS
        for c in range(MOE_CHUNKS):
            for r in range(c * rows_per, (c + 1) * rows_per):
                @pl.when(r < n_prev)
                def _():
                    scatter_copy(prev, r, 1 - slot).start(priority=r % 2)
            cs = slice(c * oc, (c + 1) * oc)
            ys_ref[slot, :, cs] = jnp.dot(hidb, wd_ref[:, cs], preferred_element_type=F32)

    @pl.when(b < n_used - 1)
    def _():
        step(True)

    @pl.when(b == n_used - 1)
    def _():
        step(False)
        for_rows(nvalid_ref[b], lambda r: scatter_copy(b, r, slot).start())
        scatter_wait(nvalid_ref[b], slot)
        scatter_wait(jnp.where(b >= 1, nvalid_ref[jnp.maximum(b - 1, 0)], 0), 1 - slot)


def _moe(layer, dest, bexp, nvalid, nused, h2, wg, wu, wd):
    def wmap(b, dest_ref, bexp_ref, nvalid_ref, nused_ref):
        return (layer, bexp_ref[jnp.minimum(b, nused_ref[0] - 1)], 0, 0)

    return pl.pallas_call(
        _moe_kernel,
        out_shape=jax.ShapeDtypeStruct((N_ASSIGN, 1, D), F32),
        grid_spec=pltpu.PrefetchScalarGridSpec(
            num_scalar_prefetch=4,
            grid=(MOE_NB,),
            in_specs=[
                pl.BlockSpec(memory_space=pl.ANY),
                pl.BlockSpec((None, None, D, D_EXPERT), wmap),
                pl.BlockSpec((None, None, D, D_EXPERT), wmap),
                pl.BlockSpec((None, None, D_EXPERT, D), wmap),
            ],
            out_specs=pl.BlockSpec(memory_space=pl.ANY),
            scratch_shapes=[
                pltpu.SMEM((MOE_ROWS,), jnp.int32),
                pltpu.VMEM((2, MOE_BLK, D), F32),
                pltpu.VMEM((2, MOE_BLK, D), F32),
                pltpu.VMEM((MOE_BLK, D), BF16),
                pltpu.VMEM((MOE_BLK, D_EXPERT), BF16),
                pltpu.SemaphoreType.DMA((2,)),
                pltpu.SemaphoreType.DMA((2,)),
            ],
        ),
        compiler_params=pltpu.CompilerParams(
            dimension_semantics=("arbitrary",), vmem_limit_bytes=VMEM_LIMIT),
        name="moe",
    )(dest, bexp, nvalid, nused, h2, wg, wu, wd)


def _combine_kernel(x1_ref, y0_ref, y1_ref, gw_ref, g2_ref, o_ref):
    gw = gw_ref[...]
    moe = gw[:, 0:1] * y0_ref[:, 0, :] + gw[:, 1:2] * y1_ref[:, 0, :]
    o_ref[...] = x1_ref[...] + g2_ref[...] * moe


def _combine(x1, y, gw, g2):
    nt = SEQ // TM
    return pl.pallas_call(
        _combine_kernel,
        out_shape=jax.ShapeDtypeStruct((SEQ, D), F32),
        grid=(nt,),
        in_specs=[
            pl.BlockSpec((TM, D), lambda i: (i, 0)),
            pl.BlockSpec((TM, 1, D), lambda i: (i, 0, 0)),
            pl.BlockSpec((TM, 1, D), lambda i: (i + nt, 0, 0)),
            pl.BlockSpec((TM, 2), lambda i: (i, 0)),
            pl.BlockSpec((1, D), lambda i: (0, 0)),
        ],
        out_specs=pl.BlockSpec((TM, D), lambda i: (i, 0)),
        compiler_params=pltpu.CompilerParams(
            dimension_semantics=("arbitrary",), vmem_limit_bytes=VMEM_LIMIT),
        name="combine",
    )(x1, y, y, gw, g2)


def _dispatch_plan(info, counts):
    cnt = counts[:, 0].astype(jnp.int32)
    padded = (cnt + MOE_BLK - 1) // MOE_BLK * MOE_BLK
    ends = jnp.cumsum(padded)
    pad_start = ends - padded
    e = info[0:2].astype(jnp.int32)
    rank = info[4:6].astype(jnp.int32)
    onehot = e[:, :, None] == jnp.arange(N_EXPERTS, dtype=jnp.int32)
    dest = jnp.sum(jnp.where(onehot, pad_start, 0), axis=-1) + rank
    blk_start = jnp.arange(MOE_NB, dtype=jnp.int32) * MOE_BLK
    bexp = jnp.sum(blk_start[:, None] >= ends[None, :], axis=-1)
    bexp = jnp.minimum(bexp, N_EXPERTS - 1).astype(jnp.int32)
    in_blk = jnp.arange(N_EXPERTS, dtype=jnp.int32)[None, :] == bexp[:, None]
    left = jnp.sum(jnp.where(in_blk, cnt + pad_start, 0), axis=-1) - blk_start
    nvalid = jnp.clip(left, 0, MOE_BLK).astype(jnp.int32)
    nused = (ends[-1:] // MOE_BLK).astype(jnp.int32)
    return dest.reshape(N_ASSIGN), bexp, nvalid, nused


def kernel(x, c, w_ada, b_ada, g_mix, g_ffn, w_in, a_ws, a_bs, a_vg, b_w, b_scale,
           c_qg, c_kg, c_bf, w_out, w_router, b_router, e_gate, e_up, e_down):
    xs = x.reshape(SEQ, D)
    mod = _ada(c, w_ada, b_ada)
    wr_t = w_router.T.astype(BF16)
    br = b_router.reshape(N_EXPERTS, 1).astype(F32)
    n_uvp = 2 * A_WIDTH + B_WIDTH
    place = np.zeros((3 * HEAD, C_WIDTH), np.float32)
    for pj in range(3):
        for ph in range(C_HEADS):
            place[pj * HEAD + ph, ph * HEAD + pj] = 1.0
    place = jnp.asarray(place, BF16)
    wg_all, wu_all, wd_all = e_gate.astype(BF16), e_up.astype(BF16), e_down.astype(BF16)
    for l in range(DEPTH):
        sh1, sc1, g1, sh2, sc2, g2 = [mod[l, :, j * D:(j + 1) * D] for j in range(6)]
        gain1 = g_mix[l].reshape(1, D)
        w_uvp = w_in[l, :, :n_uvp].astype(BF16)
        w_qk = w_in[l, :, n_uvp:n_uvp + 2 * C_WIDTH].astype(BF16)
        w_vt = w_in[l, :, n_uvp + 2 * C_WIDTH:n_uvp + 3 * C_WIDTH].T.astype(BF16)
        wf = jnp.pad(w_in[l, :, n_uvp + 3 * C_WIDTH:], ((0, 0), (0, HEAD - C_HEADS))).astype(BF16)
        bf = jnp.pad(c_bf[l], (0, HEAD - C_HEADS)).reshape(1, HEAD)
        bsb = jnp.repeat(a_bs[l].T, HEAD, axis=1)
        yab = _mix_ab(xs, gain1, sc1, sh1, w_uvp, a_ws[l], bsb, a_vg[l].reshape(1, A_WIDTH),
                      b_w[l], b_scale[l].reshape(1, B_WIDTH))
        qa, ka, vt, f2 = _mix_c(xs, gain1, sc1, sh1, w_qk, w_vt, wf, place, c_qg[l].reshape(1, HEAD),
                            c_kg[l].reshape(1, HEAD), bf)
        yc = _attn(_first_key_block(f2, c_qg[l], c_kg[l]), qa, ka, vt)
        x1, h2, info, counts = _out_proj(yab, yc, xs, w_out[l].astype(BF16), g1,
                                         g_ffn[l].reshape(1, D), sc2, sh2, wr_t, br)
        dest, bexp, nvalid, nused = _dispatch_plan(info, counts)
        y = _moe(l, dest, bexp, nvalid, nused, h2, wg_all, wu_all, wd_all)
        xs = _combine(x1, y, info[2:4].T, g2)
    return xs.reshape(1, SEQ, D)
```

```python
import functools

import jax
import jax.numpy as jnp
import numpy as np
from jax import lax
from jax.experimental import pallas as pl
from jax.experimental.pallas import tpu as pltpu

F32 = jnp.float32
BF16 = jnp.bfloat16

D = 2048
SEQ = 8192
DEPTH = 2
CHUNK = 64
A_WIDTH = 512
A_HEADS = 4
A_BLOCK = 128
B_WIDTH = 512
POOL_WINDOWS = (2, 4, 8, 16)
POOL_HALO = 16
C_WIDTH = 1024
C_HEADS = 8
HEAD = 128
N_EXPERTS = 16
N_GROUPS = 4
GROUP = 4
D_EXPERT = 1024
EPS = 1e-6
LOG2E = 1.4426950408889634

TM = 512
TQ = 512
TK = 512
ATTN_HEADS = 4
ATTN_DROP_LOG2 = 64.0
MOE_BLK = 256
MOE_CHUNKS = 4
N_ASSIGN = 2 * SEQ
MOE_ROWS = N_ASSIGN + N_EXPERTS * MOE_BLK
MOE_NB = MOE_ROWS // MOE_BLK

VMEM_LIMIT = 56 * 1024 * 1024


def _nt_dot(a, b):
    return lax.dot_general(a, b, (((1,), (1,)), ((), ())), preferred_element_type=F32)


def _rms_mod(x, gain, scale, shift):
    ms = jnp.mean(x * x, axis=-1, keepdims=True)
    return (x * lax.rsqrt(ms + EPS) * gain) * (1.0 + scale) + shift


def _ada_kernel(c_ref, w_ref, b_ref, o_ref):
    ca = jax.nn.silu(c_ref[...])
    r = jnp.dot(ca, w_ref[...], preferred_element_type=F32, precision=lax.Precision.HIGHEST)
    o_ref[...] = r[0:1, :] + b_ref[...]


def _ada(c, w_ada, b_ada):
    tn = 1024
    c8 = jnp.broadcast_to(c, (8, D))
    return pl.pallas_call(
        _ada_kernel,
        out_shape=jax.ShapeDtypeStruct((DEPTH, 1, 6 * D), F32),
        grid=(DEPTH, 6 * D // tn),
        in_specs=[
            pl.BlockSpec((8, D), lambda l, j: (0, 0)),
            pl.BlockSpec((None, D, tn), lambda l, j: (l, 0, j)),
            pl.BlockSpec((None, 1, tn), lambda l, j: (l, 0, j)),
        ],
        out_specs=pl.BlockSpec((None, 1, tn), lambda l, j: (l, 0, j)),
        compiler_params=pltpu.CompilerParams(
            dimension_semantics=("arbitrary", "arbitrary"), vmem_limit_bytes=VMEM_LIMIT),
        name="ada",
    )(c8, w_ada, b_ada.reshape(DEPTH, 1, 6 * D))


def _mix_ab_kernel(x_ref, gain_ref, sc_ref, sh_ref, w_ref, ws_ref, bsb_ref, vg_ref, bw_ref,
                   bscale_ref, y_ref, pext_ref):
    i = pl.program_id(0)
    h = _rms_mod(x_ref[...], gain_ref[...], sc_ref[...], sh_ref[...])
    z = jnp.dot(h.astype(BF16), w_ref[...], preferred_element_type=F32)
    u = jax.nn.gelu(z[:, :A_WIDTH])
    v = jax.nn.gelu(z[:, A_WIDTH:2 * A_WIDTH])
    p = z[:, 2 * A_WIDTH:]

    cid_t = lax.broadcasted_iota(jnp.int32, (A_BLOCK, A_BLOCK), 0) // CHUNK
    cid_s = lax.broadcasted_iota(jnp.int32, (A_BLOCK, A_BLOCK), 1) // CHUNK
    mask = cid_s <= cid_t
    for hh in range(A_HEADS):
        cs = slice(hh * HEAD, (hh + 1) * HEAD)
        vh = v[:, cs]
        vn = vh * lax.rsqrt(jnp.mean(vh * vh, axis=-1, keepdims=True) + EPS) * vg_ref[:, cs]
        vnb = vn.astype(BF16)
        wm = jnp.where(mask, ws_ref[hh], 0.0).astype(BF16)
        for n in range(TM // A_BLOCK):
            rs = slice(n * A_BLOCK, (n + 1) * A_BLOCK)
            sp = jnp.dot(wm, vnb[rs], preferred_element_type=F32) + bsb_ref[:, cs]
            y_ref[rs, cs] = (u[rs, cs] * sp).astype(BF16)

    @pl.when(i == 0)
    def _():
        pext_ref[0:POOL_HALO, :] = jnp.zeros((POOL_HALO, B_WIDTH), F32)

    pext_ref[POOL_HALO:POOL_HALO + TM, :] = p
    t1 = i * TM + lax.broadcasted_iota(jnp.int32, (TM, 1), 0) + 1
    for g, w in enumerate(POOL_WINDOWS):
        cs = slice(g * HEAD, (g + 1) * HEAD)
        acc = p[:, cs]
        for j in range(1, w):
            acc = acc + pext_ref[POOL_HALO - j:POOL_HALO - j + TM, cs]
        cnt = jnp.minimum(t1, w).astype(F32)
        d = acc / cnt - p[:, cs]
        yb = jnp.dot(d.astype(BF16), bw_ref[g].astype(BF16), preferred_element_type=F32)
        y_ref[:, A_WIDTH + g * HEAD:A_WIDTH + (g + 1) * HEAD] = (yb * bscale_ref[:, cs]).astype(BF16)
    pext_ref[0:POOL_HALO, :] = pext_ref[TM:TM + POOL_HALO, :]


def _mix_ab(x, gain, sc, sh, w_uvp, a_ws, bsb, vg, b_w, bscale):
    row = lambda i: (0, 0)
    return pl.pallas_call(
        _mix_ab_kernel,
        out_shape=jax.ShapeDtypeStruct((SEQ, A_WIDTH + B_WIDTH), BF16),
        grid=(SEQ // TM,),
        in_specs=[
            pl.BlockSpec((TM, D), lambda i: (i, 0)),
            pl.BlockSpec((1, D), row), pl.BlockSpec((1, D), row), pl.BlockSpec((1, D), row),
            pl.BlockSpec((D, 2 * A_WIDTH + B_WIDTH), row, pipeline_mode=pl.Buffered(1)),
            pl.BlockSpec((A_HEADS, A_BLOCK, A_BLOCK), lambda i: (0, 0, 0)),
            pl.BlockSpec((A_BLOCK, A_WIDTH), row),
            pl.BlockSpec((1, A_WIDTH), row),
            pl.BlockSpec((len(POOL_WINDOWS), HEAD, HEAD), lambda i: (0, 0, 0)),
            pl.BlockSpec((1, B_WIDTH), row),
        ],
        out_specs=pl.BlockSpec((TM, A_WIDTH + B_WIDTH), lambda i: (i, 0)),
        scratch_shapes=[pltpu.VMEM((TM + POOL_HALO, B_WIDTH), F32)],
        compiler_params=pltpu.CompilerParams(
            dimension_semantics=("arbitrary",), vmem_limit_bytes=VMEM_LIMIT),
        name="mix_ab",
    )(x, gain, sc, sh, w_uvp, a_ws, bsb, vg, b_w, bscale)


def _split3(x):
    hi = x.astype(BF16)
    rest = x - hi.astype(F32)
    mid = rest.astype(BF16)
    lo = (rest - mid.astype(F32)).astype(BF16)
    return jnp.concatenate([hi, mid, lo], axis=1)


def _mix_c_kernel(x_ref, gain_ref, sc_ref, sh_ref, w_ref, wvt_ref, place_ref, qg_ref, kg_ref,
                  bf_ref, q_ref, k_ref, vt_ref, f2_ref, carry_ref):
    i = pl.program_id(0)
    h = _rms_mod(x_ref[...], gain_ref[...], sc_ref[...], sh_ref[...])
    hb = h.astype(BF16)
    z = jnp.dot(hb, w_ref[...], preferred_element_type=F32)
    lane = lax.broadcasted_iota(jnp.int32, (TM, HEAD), 1)
    q_aug = jnp.where(lane < 3, -1.0, 0.0).astype(BF16)
    qscale = HEAD ** -0.5 * LOG2E
    for hh in range(C_HEADS):
        qh = z[:, hh * HEAD:(hh + 1) * HEAD]
        qn = qh * lax.rsqrt(jnp.mean(qh * qh, axis=-1, keepdims=True) + EPS) * qg_ref[...]
        q_ref[:, 2 * hh * HEAD:(2 * hh + 1) * HEAD] = (qn * qscale).astype(BF16)
        q_ref[:, (2 * hh + 1) * HEAD:(2 * hh + 2) * HEAD] = q_aug
        kh = z[:, C_WIDTH + hh * HEAD:C_WIDTH + (hh + 1) * HEAD]
        kn = kh * lax.rsqrt(jnp.mean(kh * kh, axis=-1, keepdims=True) + EPS) * kg_ref[...]
        k_ref[:, 2 * hh * HEAD:(2 * hh + 1) * HEAD] = kn.astype(BF16)
    vt_ref[...] = _nt_dot(wvt_ref[...], hb).astype(BF16)

    @pl.when(i == 0)
    def _():
        carry_ref[...] = jnp.zeros((1, HEAD), F32)

    logf = jax.nn.log_sigmoid(z[:, 2 * C_WIDTH:] + bf_ref[...])
    r = lax.broadcasted_iota(jnp.int32, (TM, TM), 0)
    c = lax.broadcasted_iota(jnp.int32, (TM, TM), 1)
    lower = jnp.where(c <= r, 1.0, 0.0).astype(BF16)
    cum = jnp.dot(lower, _split3(logf), preferred_element_type=F32)
    f_cum = (cum[:, :HEAD] + cum[:, HEAD:2 * HEAD]) + cum[:, 2 * HEAD:] + carry_ref[...]
    carry_ref[...] = f_cum[TM - 1:TM, :]
    f2 = f_cum * LOG2E
    f2_ref[...] = f2
    aug = jnp.dot(_split3(f2), place_ref[...], preferred_element_type=F32)
    for hh in range(C_HEADS):
        k_ref[:, (2 * hh + 1) * HEAD:(2 * hh + 2) * HEAD] = aug[:, hh * HEAD:(hh + 1) * HEAD].astype(BF16)


def _mix_c(x, gain, sc, sh, w_qkf, w_vt, place, qg, kg, bf):
    row = lambda i: (0, 0)
    tile = pl.BlockSpec((TM, 2 * C_WIDTH), lambda i: (i, 0))
    return pl.pallas_call(
        _mix_c_kernel,
        out_shape=(
            jax.ShapeDtypeStruct((SEQ, 2 * C_WIDTH), BF16),
            jax.ShapeDtypeStruct((SEQ, 2 * C_WIDTH), BF16),
            jax.ShapeDtypeStruct((C_WIDTH, SEQ), BF16),
            jax.ShapeDtypeStruct((SEQ, HEAD), F32),
        ),
        grid=(SEQ // TM,),
        in_specs=[
            pl.BlockSpec((TM, D), lambda i: (i, 0)),
            pl.BlockSpec((1, D), row), pl.BlockSpec((1, D), row), pl.BlockSpec((1, D), row),
            pl.BlockSpec((D, 2 * C_WIDTH + HEAD), row, pipeline_mode=pl.Buffered(1)),
            pl.BlockSpec((C_WIDTH, D), row, pipeline_mode=pl.Buffered(1)),
            pl.BlockSpec((3 * HEAD, C_WIDTH), row),
            pl.BlockSpec((1, HEAD), row), pl.BlockSpec((1, HEAD), row),
            pl.BlockSpec((1, HEAD), row),
        ],
        out_specs=(tile, tile, pl.BlockSpec((C_WIDTH, TM), lambda i: (0, i)),
                   pl.BlockSpec((TM, HEAD), lambda i: (i, 0))),
        scratch_shapes=[pltpu.VMEM((1, HEAD), F32)],
        compiler_params=pltpu.CompilerParams(
            dimension_semantics=("arbitrary",), vmem_limit_bytes=VMEM_LIMIT),
        name="mix_c",
    )(x, gain, sc, sh, w_qkf, w_vt, place, qg, kg, bf)


def _attn_kernel(first_ref, q_ref, k_ref, vt_ref, o_ref, st_ref):
    i = pl.program_id(1)
    first = first_ref[pl.program_id(0) * (SEQ // TQ) + i]

    def scores_t(g, j):
        off = pl.multiple_of(j * TK, TK)
        cols = slice(2 * g * HEAD, 2 * (g + 1) * HEAD)
        return _nt_dot(k_ref[pl.ds(off, TK), cols], q_ref[:, cols])

    def update(g, st, j, m, l, acc):
        m_new = jnp.maximum(m, jnp.max(st, axis=0, keepdims=True))
        alpha = jnp.exp2(m - m_new)
        p = jnp.exp2(st - m_new)
        l_new = alpha * l + jnp.sum(p, axis=0, keepdims=True)
        off = pl.multiple_of(j * TK, TK)
        pv = jnp.dot(vt_ref[g * HEAD:(g + 1) * HEAD, pl.ds(off, TK)], p.astype(BF16),
                     preferred_element_type=F32)
        return m_new, l_new, alpha * acc + pv

    for g in range(ATTN_HEADS):
        st_ref[g] = scores_t(g, first)

    def body(j, carry):
        cur = [st_ref[g] for g in range(ATTN_HEADS)]
        nxt = [scores_t(g, j + 1) for g in range(ATTN_HEADS)]
        out = tuple(update(g, cur[g], j, *carry[g]) for g in range(ATTN_HEADS))
        for g in range(ATTN_HEADS):
            st_ref[g] = nxt[g]
        return out

    init = (jnp.full((1, TQ), -jnp.inf, F32), jnp.zeros((1, TQ), F32), jnp.zeros((HEAD, TQ), F32))
    carry = lax.fori_loop(first, i, body, (init,) * ATTN_HEADS)
    kpos = lax.broadcasted_iota(jnp.int32, (TK, TQ), 0)
    qpos = lax.broadcasted_iota(jnp.int32, (TK, TQ), 1)
    for g in range(ATTN_HEADS):
        st = jnp.where(kpos <= qpos, st_ref[g], -jnp.inf)
        m, l, acc = update(g, st, i, *carry[g])
        o_ref[:, g * HEAD:(g + 1) * HEAD] = (acc / l).T.astype(BF16)


def _first_key_block(f2, qg, kg):
    nq = SEQ // TQ
    bound = LOG2E * HEAD ** 0.5 * jnp.max(jnp.abs(qg)) * jnp.max(jnp.abs(kg)) * 1.02
    f_start = f2[0::TQ, :C_HEADS]
    f_end = f2[TK - 1::TK, :C_HEADS]
    gap = f_end[None, :, :] - f_start[:, None, :]
    needed = gap <= 2.0 * bound + ATTN_DROP_LOG2
    needed = jnp.any(needed.reshape(nq, SEQ // TK, C_HEADS // ATTN_HEADS, ATTN_HEADS), axis=-1)
    first = jnp.argmax(needed, axis=1)
    return first.T.reshape(-1).astype(jnp.int32)


def _attn(first, qa, ka, vt):
    gw = ATTN_HEADS * HEAD
    return pl.pallas_call(
        _attn_kernel,
        out_shape=jax.ShapeDtypeStruct((SEQ, C_WIDTH), BF16),
        grid_spec=pltpu.PrefetchScalarGridSpec(
            num_scalar_prefetch=1,
            grid=(C_HEADS // ATTN_HEADS, SEQ // TQ),
            in_specs=[
                pl.BlockSpec((TQ, 2 * gw), lambda h, i, f: (i, h)),
                pl.BlockSpec((SEQ, 2 * gw), lambda h, i, f: (0, h), pipeline_mode=pl.Buffered(1)),
                pl.BlockSpec((gw, SEQ), lambda h, i, f: (h, 0), pipeline_mode=pl.Buffered(1)),
            ],
            out_specs=pl.BlockSpec((TQ, gw), lambda h, i, f: (i, h)),
            scratch_shapes=[pltpu.VMEM((ATTN_HEADS, TK, TQ), F32)],
        ),
        compiler_params=pltpu.CompilerParams(
            dimension_semantics=("arbitrary", "arbitrary"), vmem_limit_bytes=VMEM_LIMIT),
        name="attn",
    )(first, qa, ka, vt)


def _first_max(vals):
    best, idx = vals[0], jnp.zeros(vals[0].shape, jnp.int32)
    for j in range(1, len(vals)):
        better = vals[j] > best
        idx = jnp.where(better, j, idx)
        best = jnp.where(better, vals[j], best)
    return idx, best


def _pick(idx, vals):
    out = vals[-1]
    for j in range(len(vals) - 2, -1, -1):
        out = jnp.where(idx == j, vals[j], out)
    return out


def _out_kernel(yab_ref, yc_ref, x_ref, w_ref, g1_ref, gain_ref, sc_ref, sh_ref, wr_ref, br_ref,
                x1_ref, h2_ref, info_ref, cnt_ref, carry_ref):
    i = pl.program_id(0)
    y = jnp.dot(yab_ref[...], w_ref[0:A_WIDTH + B_WIDTH, :], preferred_element_type=F32)
    y = y + jnp.dot(yc_ref[...], w_ref[A_WIDTH + B_WIDTH:, :], preferred_element_type=F32)
    x1 = x_ref[...] + g1_ref[...] * y
    x1_ref[...] = x1
    h2 = _rms_mod(x1, gain_ref[...], sc_ref[...], sh_ref[...])
    h2_ref[:, 0, :] = h2

    logits = _nt_dot(wr_ref[...], h2.astype(BF16))
    scores = jax.nn.sigmoid(logits)
    sel = scores + br_ref[...]
    sel_r = [sel[k:k + 1, :] for k in range(N_EXPERTS)]
    sc_r = [scores[k:k + 1, :] for k in range(N_EXPERTS)]
    grp = []
    for g in range(N_GROUPS):
        a = sel_r[GROUP * g:GROUP * (g + 1)]
        pair = [a[p] + a[q] for p in range(GROUP) for q in range(p + 1, GROUP)]
        grp.append(functools.reduce(jnp.maximum, pair))
    gi, _ = _first_max(grp)
    cand = [_pick(gi, [sel_r[GROUP * g + j] for g in range(N_GROUPS)]) for j in range(GROUP)]
    cand_s = [_pick(gi, [sc_r[GROUP * g + j] for g in range(N_GROUPS)]) for j in range(GROUP)]
    i0, _ = _first_max(cand)
    i1, _ = _first_max([jnp.where(i0 == j, -jnp.inf, cand[j]) for j in range(GROUP)])
    s0, s1 = _pick(i0, cand_s), _pick(i1, cand_s)
    e0, e1 = GROUP * gi + i0, GROUP * gi + i1
    den = s0 + s1

    @pl.when(i == 0)
    def _():
        carry_ref[...] = jnp.zeros((N_EXPERTS, HEAD), F32)

    ek = lax.broadcasted_iota(jnp.int32, (N_EXPERTS, TM), 0)
    oh0 = (ek == e0).astype(F32)
    oh1 = (ek == e1).astype(F32)
    both = oh0 + oh1
    r = lax.broadcasted_iota(jnp.int32, (TM, TM), 0)
    c = lax.broadcasted_iota(jnp.int32, (TM, TM), 1)
    before = (r < c).astype(BF16)
    run = jnp.dot(both.astype(BF16), before, preferred_element_type=F32) + carry_ref[:, 0:1]
    rank0 = jnp.sum(run * oh0, axis=0, keepdims=True)
    rank1 = jnp.sum(run * oh1, axis=0, keepdims=True)
    total = carry_ref[...] + jnp.sum(both, axis=-1, keepdims=True)
    carry_ref[...] = total
    cnt_ref[...] = total

    info_ref[0:1, :] = e0.astype(F32)
    info_ref[1:2, :] = e1.astype(F32)
    info_ref[2:3, :] = s0 / den
    info_ref[3:4, :] = s1 / den
    info_ref[4:5, :] = rank0
    info_ref[5:6, :] = rank1
    info_ref[6:8, :] = jnp.zeros((2, TM), F32)


def _out_proj(yab, yc, x, w_out, g1, gain, sc, sh, wr_t, br):
    row = lambda i: (0, 0)
    return pl.pallas_call(
        _out_kernel,
        out_shape=(
            jax.ShapeDtypeStruct((SEQ, D), F32),
            jax.ShapeDtypeStruct((SEQ, 1, D), F32),
            jax.ShapeDtypeStruct((8, SEQ), F32),
            jax.ShapeDtypeStruct((N_EXPERTS, HEAD), F32),
        ),
        grid=(SEQ // TM,),
        in_specs=[
            pl.BlockSpec((TM, A_WIDTH + B_WIDTH), lambda i: (i, 0)),
            pl.BlockSpec((TM, C_WIDTH), lambda i: (i, 0)),
            pl.BlockSpec((TM, D), lambda i: (i, 0)),
            pl.BlockSpec((D, D), row, pipeline_mode=pl.Buffered(1)),
            pl.BlockSpec((1, D), row), pl.BlockSpec((1, D), row),
            pl.BlockSpec((1, D), row), pl.BlockSpec((1, D), row),
            pl.BlockSpec((N_EXPERTS, D), row),
            pl.BlockSpec((N_EXPERTS, 1), row),
        ],
        out_specs=(
            pl.BlockSpec((TM, D), lambda i: (i, 0)),
            pl.BlockSpec((TM, 1, D), lambda i: (i, 0, 0)),
            pl.BlockSpec((8, TM), lambda i: (0, i)),
            pl.BlockSpec((N_EXPERTS, HEAD), row),
        ),
        scratch_shapes=[pltpu.VMEM((N_EXPERTS, HEAD), F32)],
        compiler_params=pltpu.CompilerParams(
            dimension_semantics=("arbitrary",), vmem_limit_bytes=VMEM_LIMIT),
        name="out_proj",
    )(yab, yc, x, w_out, g1, gain, sc, sh, wr_t, br)


def _moe_kernel(dest_ref, bexp_ref, nvalid_ref, nused_ref, h_hbm, wg_ref, wu_ref, wd_ref, y_hbm,
                code_ref, xs_ref, ys_ref, xb_ref, hid_ref, gsem, ssem):
    b = pl.program_id(0)
    n_used = nused_ref[0]
    slot = b % 2

    def gather_copy(blk, r, s):
        tok = code_ref[blk * MOE_BLK + r] & (SEQ - 1)
        return pltpu.make_async_copy(h_hbm.at[tok], xs_ref.at[s, pl.ds(r, 1), :],
                                     gsem.at[s])

    def scatter_copy(blk, r, s):
        dst = code_ref[blk * MOE_BLK + r]
        return pltpu.make_async_copy(ys_ref.at[s, pl.ds(r, 1), :], y_hbm.at[dst],
                                     ssem.at[s])

    def for_rows(n, fn):
        def body(r, _):
            fn(r)
            return 0
        lax.fori_loop(0, n, body, 0)

    def scatter_wait(n, s):
        n8 = pl.multiple_of((n >> 3) << 3, 8)

        @pl.when(n8 > 0)
        def _():
            pltpu.make_async_copy(ys_ref.at[s, pl.ds(0, n8), :], ys_ref.at[s, pl.ds(0, n8), :],
                                  ssem.at[s]).wait()

        for r in range(7):
            @pl.when(n8 + r < n)
            def _():
                pltpu.make_async_copy(ys_ref.at[s, pl.ds(r, 1), :], y_hbm.at[r],
                                      ssem.at[s]).wait()

    @pl.when(b == 0)
    def _():
        def fill(r, _):
            code_ref[r] = 0
            return 0
        lax.fori_loop(0, MOE_ROWS, fill, 0, unroll=8)

        def place(a, _):
            code_ref[dest_ref[a]] = a
            return 0
        lax.fori_loop(0, N_ASSIGN, place, 0, unroll=8)
        for_rows(MOE_BLK, lambda r: gather_copy(0, r, 0).start())

    def step(prefetch):
        for r in range(MOE_BLK):
            gather_copy(b, r, slot).wait()

        xb_ref[...] = xs_ref[slot].astype(BF16)
        prev = jnp.maximum(b - 1, 0)
        n_prev = jnp.where(b >= 1, nvalid_ref[prev], 0)
        x = xb_ref[...]
        rows_per = MOE_BLK // MOE_CHUNKS

        hc = D_EXPERT // MOE_CHUNKS
        for c in range(MOE_CHUNKS):
            cs = slice(c * hc, (c + 1) * hc)
            gate = jnp.dot(x, wg_ref[:, cs], preferred_element_type=F32)
            up = jnp.dot(x, wu_ref[:, cs], preferred_element_type=F32)
            hid = jax.nn.silu(gate) * up
            if prefetch:
                for r in range(c * rows_per, (c + 1) * rows_per):
                    gather_copy(b + 1, r, 1 - slot).start(priority=r % 2)
                tie = pltpu.bitcast(xs_ref[slot, 0:8, 0:hc], jnp.uint32)
                tie = pltpu.bitcast((tie >> 16) >> 16, F32)
                hid = jnp.concatenate([hid[0:8] + tie, hid[8:]], axis=0)
            hid_ref[:, cs] = hid.astype(BF16)

        scatter_wait(jnp.where(b >= 2, nvalid_ref[jnp.maximum(b - 2, 0)], 0), slot)
        hidb = hid_ref[...]
        oc = D // MOE_CHUNKS
        for c in range(MOE_CHUNKS):
            for r in range(c * rows_per, (c + 1) * rows_per):
                @pl.when(r < n_prev)
                def _():
                    scatter_copy(prev, r, 1 - slot).start(priority=r % 2)
            cs = slice(c * oc, (c + 1) * oc)
            ys_ref[slot, :, cs] = jnp.dot(hidb, wd_ref[:, cs], preferred_element_type=F32)

    @pl.when(b < n_used - 1)
    def _():
        step(True)

    @pl.when(b == n_used - 1)
    def _():
        step(False)
        for_rows(nvalid_ref[b], lambda r: scatter_copy(b, r, slot).start())
        scatter_wait(nvalid_ref[b], slot)
        scatter_wait(jnp.where(b >= 1, nvalid_ref[jnp.maximum(b - 1, 0)], 0), 1 - slot)


def _moe(layer, dest, bexp, nvalid, nused, h2, wg, wu, wd):
    def wmap(b, dest_ref, bexp_ref, nvalid_ref, nused_ref):
        return (layer, bexp_ref[jnp.minimum(b, nused_ref[0] - 1)], 0, 0)

    return pl.pallas_call(
        _moe_kernel,
        out_shape=jax.ShapeDtypeStruct((N_ASSIGN, 1, D), F32),
        grid_spec=pltpu.PrefetchScalarGridSpec(
            num_scalar_prefetch=4,
            grid=(MOE_NB,),
            in_specs=[
                pl.BlockSpec(memory_space=pl.ANY),
                pl.BlockSpec((None, None, D, D_EXPERT), wmap),
                pl.BlockSpec((None, None, D, D_EXPERT), wmap),
                pl.BlockSpec((None, None, D_EXPERT, D), wmap),
            ],
            out_specs=pl.BlockSpec(memory_space=pl.ANY),
            scratch_shapes=[
                pltpu.SMEM((MOE_ROWS,), jnp.int32),
                pltpu.VMEM((2, MOE_BLK, D), F32),
                pltpu.VMEM((2, MOE_BLK, D), F32),
                pltpu.VMEM((MOE_BLK, D), BF16),
                pltpu.VMEM((MOE_BLK, D_EXPERT), BF16),
                pltpu.SemaphoreType.DMA((2,)),
                pltpu.SemaphoreType.DMA((2,)),
            ],
        ),
        compiler_params=pltpu.CompilerParams(
            dimension_semantics=("arbitrary",), vmem_limit_bytes=VMEM_LIMIT),
        name="moe",
    )(dest, bexp, nvalid, nused, h2, wg, wu, wd)


def _combine_kernel(x1_ref, y0_ref, y1_ref, gw_ref, g2_ref, o_ref):
    gw = gw_ref[...]
    moe = gw[:, 0:1] * y0_ref[:, 0, :] + gw[:, 1:2] * y1_ref[:, 0, :]
    o_ref[...] = x1_ref[...] + g2_ref[...] * moe


def _combine(x1, y, gw, g2):
    nt = SEQ // TM
    return pl.pallas_call(
        _combine_kernel,
        out_shape=jax.ShapeDtypeStruct((SEQ, D), F32),
        grid=(nt,),
        in_specs=[
            pl.BlockSpec((TM, D), lambda i: (i, 0)),
            pl.BlockSpec((TM, 1, D), lambda i: (i, 0, 0)),
            pl.BlockSpec((TM, 1, D), lambda i: (i + nt, 0, 0)),
            pl.BlockSpec((TM, 2), lambda i: (i, 0)),
            pl.BlockSpec((1, D), lambda i: (0, 0)),
        ],
        out_specs=pl.BlockSpec((TM, D), lambda i: (i, 0)),
        compiler_params=pltpu.CompilerParams(
            dimension_semantics=("arbitrary",), vmem_limit_bytes=VMEM_LIMIT),
        name="combine",
    )(x1, y, y, gw, g2)


def _dispatch_plan(info, counts):
    cnt = counts[:, 0].astype(jnp.int32)
    padded = (cnt + MOE_BLK - 1) // MOE_BLK * MOE_BLK
    ends = jnp.cumsum(padded)
    pad_start = ends - padded
    e = info[0:2].astype(jnp.int32)
    rank = info[4:6].astype(jnp.int32)
    onehot = e[:, :, None] == jnp.arange(N_EXPERTS, dtype=jnp.int32)
    dest = jnp.sum(jnp.where(onehot, pad_start, 0), axis=-1) + rank
    blk_start = jnp.arange(MOE_NB, dtype=jnp.int32) * MOE_BLK
    bexp = jnp.sum(blk_start[:, None] >= ends[None, :], axis=-1)
    bexp = jnp.minimum(bexp, N_EXPERTS - 1).astype(jnp.int32)
    in_blk = jnp.arange(N_EXPERTS, dtype=jnp.int32)[None, :] == bexp[:, None]
    left = jnp.sum(jnp.where(in_blk, cnt + pad_start, 0), axis=-1) - blk_start
    nvalid = jnp.clip(left, 0, MOE_BLK).astype(jnp.int32)
    nused = (ends[-1:] // MOE_BLK).astype(jnp.int32)
    return dest.reshape(N_ASSIGN), bexp, nvalid, nused


def kernel(x, c, w_ada, b_ada, g_mix, g_ffn, w_in, a_ws, a_bs, a_vg, b_w, b_scale,
           c_qg, c_kg, c_bf, w_out, w_router, b_router, e_gate, e_up, e_down):
    xs = x.reshape(SEQ, D)
    mod = _ada(c, w_ada, b_ada)
    wr_t = w_router.T.astype(BF16)
    br = b_router.reshape(N_EXPERTS, 1).astype(F32)
    n_uvp = 2 * A_WIDTH + B_WIDTH
    place = np.zeros((3 * HEAD, C_WIDTH), np.float32)
    for pj in range(3):
        for ph in range(C_HEADS):
            place[pj * HEAD + ph, ph * HEAD + pj] = 1.0
    place = jnp.asarray(place, BF16)
    wg_all, wu_all, wd_all = e_gate.astype(BF16), e_up.astype(BF16), e_down.astype(BF16)
    for l in range(DEPTH):
        sh1, sc1, g1, sh2, sc2, g2 = [mod[l, :, j * D:(j + 1) * D] for j in range(6)]
        gain1 = g_mix[l].reshape(1, D)
        w_uvp = w_in[l, :, :n_uvp].astype(BF16)
        w_qk = w_in[l, :, n_uvp:n_uvp + 2 * C_WIDTH].astype(BF16)
        wf = jnp.pad(w_in[l, :, n_uvp + 3 * C_WIDTH:], ((0, 0), (0, HEAD - C_HEADS))).astype(BF16)
        w_qkf = jnp.concatenate([w_qk, wf], axis=1)
        w_vt = w_in[l, :, n_uvp + 2 * C_WIDTH:n_uvp + 3 * C_WIDTH].T.astype(BF16)
        bf =jnp.pad(c_bf[l], (0, HEAD - C_HEADS)).reshape(1, HEAD)
        bsb = jnp.repeat(a_bs[l].T, HEAD, axis=1)
        yab = _mix_ab(xs, gain1, sc1, sh1, w_uvp, a_ws[l], bsb, a_vg[l].reshape(1, A_WIDTH),
                      b_w[l], b_scale[l].reshape(1, B_WIDTH))
        qa, ka, vt, f2 = _mix_c(xs, gain1, sc1, sh1, w_qkf, w_vt, place, c_qg[l].reshape(1, HEAD),
                            c_kg[l].reshape(1, HEAD), bf)
        yc = _attn(_first_key_block(f2, c_qg[l], c_kg[l]), qa, ka, vt)
        x1, h2, info, counts = _out_proj(yab, yc, xs, w_out[l].astype(BF16), g1,
                                         g_ffn[l].reshape(1, D), sc2, sh2, wr_t, br)
        dest, bexp, nvalid, nused = _dispatch_plan(info, counts)
        y = _moe(l, dest, bexp, nvalid, nused, h2, wg_all, wu_all, wd_all)
        xs = _combine(x1, y, info[2:4].T, g2)
    return xs.reshape(1, SEQ, D)
```

```python
import functools

import jax
import jax.numpy as jnp
import numpy as np
from jax import lax
from jax.experimental import pallas as pl
from jax.experimental.pallas import tpu as pltpu

F32 = jnp.float32
BF16 = jnp.bfloat16

D = 2048
SEQ = 8192
DEPTH = 2
CHUNK = 64
A_WIDTH = 512
A_HEADS = 4
A_BLOCK = 128
B_WIDTH = 512
POOL_WINDOWS = (2, 4, 8, 16)
POOL_HALO = 16
C_WIDTH = 1024
C_HEADS = 8
HEAD = 128
N_EXPERTS = 16
N_GROUPS = 4
GROUP = 4
D_EXPERT = 1024
EPS = 1e-6
LOG2E = 1.4426950408889634

TM = 512
TQ = 512
TK = 512
ATTN_HEADS = 4
ATTN_DROP_LOG2 = 64.0
MOE_BLK = 256
MOE_CHUNKS = 4
N_ASSIGN = 2 * SEQ
MOE_ROWS = N_ASSIGN + N_EXPERTS * MOE_BLK
MOE_NB = MOE_ROWS // MOE_BLK

VMEM_LIMIT = 56 * 1024 * 1024


def _nt_dot(a, b):
    return lax.dot_general(a, b, (((1,), (1,)), ((), ())), preferred_element_type=F32)


def _rms_mod(x, gain, scale, shift):
    ms = jnp.mean(x * x, axis=-1, keepdims=True)
    return (x * lax.rsqrt(ms + EPS) * gain) * (1.0 + scale) + shift


def _ada_kernel(c_ref, w_ref, b_ref, o_ref):
    ca = jax.nn.silu(c_ref[...])
    r = jnp.dot(ca, w_ref[...], preferred_element_type=F32, precision=lax.Precision.HIGHEST)
    o_ref[...] = r[0:1, :] + b_ref[...]


def _ada(c, w_ada, b_ada):
    tn = 1024
    c8 = jnp.broadcast_to(c, (8, D))
    return pl.pallas_call(
        _ada_kernel,
        out_shape=jax.ShapeDtypeStruct((DEPTH, 1, 6 * D), F32),
        grid=(DEPTH, 6 * D // tn),
        in_specs=[
            pl.BlockSpec((8, D), lambda l, j: (0, 0)),
            pl.BlockSpec((None, D, tn), lambda l, j: (l, 0, j)),
            pl.BlockSpec((None, 1, tn), lambda l, j: (l, 0, j)),
        ],
        out_specs=pl.BlockSpec((None, 1, tn), lambda l, j: (l, 0, j)),
        compiler_params=pltpu.CompilerParams(
            dimension_semantics=("arbitrary", "arbitrary"), vmem_limit_bytes=VMEM_LIMIT),
        name="ada",
    )(c8, w_ada, b_ada.reshape(DEPTH, 1, 6 * D))


def _mix_ab_kernel(x_ref, gain_ref, sc_ref, sh_ref, w_ref, ws_ref, bsb_ref, vg_ref, bw_ref,
                   bscale_ref, y_ref, pext_ref):
    i = pl.program_id(0)
    h = _rms_mod(x_ref[...], gain_ref[...], sc_ref[...], sh_ref[...])
    z = jnp.dot(h.astype(BF16), w_ref[...], preferred_element_type=F32)
    u = jax.nn.gelu(z[:, :A_WIDTH])
    v = jax.nn.gelu(z[:, A_WIDTH:2 * A_WIDTH])
    p = z[:, 2 * A_WIDTH:]

    cid_t = lax.broadcasted_iota(jnp.int32, (A_BLOCK, A_BLOCK), 0) // CHUNK
    cid_s = lax.broadcasted_iota(jnp.int32, (A_BLOCK, A_BLOCK), 1) // CHUNK
    mask = cid_s <= cid_t
    for hh in range(A_HEADS):
        cs = slice(hh * HEAD, (hh + 1) * HEAD)
        vh = v[:, cs]
        vn = vh * lax.rsqrt(jnp.mean(vh * vh, axis=-1, keepdims=True) + EPS) * vg_ref[:, cs]
        vnb = vn.astype(BF16)
        wm = jnp.where(mask, ws_ref[hh], 0.0).astype(BF16)
        for n in range(TM // A_BLOCK):
            rs = slice(n * A_BLOCK, (n + 1) * A_BLOCK)
            sp = jnp.dot(wm, vnb[rs], preferred_element_type=F32) + bsb_ref[:, cs]
            y_ref[rs, cs] = (u[rs, cs] * sp).astype(BF16)

    @pl.when(i == 0)
    def _():
        pext_ref[0:POOL_HALO, :] = jnp.zeros((POOL_HALO, B_WIDTH), F32)

    pext_ref[POOL_HALO:POOL_HALO + TM, :] = p
    t1 = i * TM + lax.broadcasted_iota(jnp.int32, (TM, 1), 0) + 1
    for g, w in enumerate(POOL_WINDOWS):
        cs = slice(g * HEAD, (g + 1) * HEAD)
        acc = p[:, cs]
        for j in range(1, w):
            acc = acc + pext_ref[POOL_HALO - j:POOL_HALO - j + TM, cs]
        cnt = jnp.minimum(t1, w).astype(F32)
        d = acc / cnt - p[:, cs]
        yb = jnp.dot(d.astype(BF16), bw_ref[g].astype(BF16), preferred_element_type=F32)
        y_ref[:, A_WIDTH + g * HEAD:A_WIDTH + (g + 1) * HEAD] = (yb * bscale_ref[:, cs]).astype(BF16)
    pext_ref[0:POOL_HALO, :] = pext_ref[TM:TM + POOL_HALO, :]


def _mix_ab(x, gain, sc, sh, w_uvp, a_ws, bsb, vg, b_w, bscale):
    row = lambda i: (0, 0)
    return pl.pallas_call(
        _mix_ab_kernel,
        out_shape=jax.ShapeDtypeStruct((SEQ, A_WIDTH + B_WIDTH), BF16),
        grid=(SEQ // TM,),
        in_specs=[
            pl.BlockSpec((TM, D), lambda i: (i, 0)),
            pl.BlockSpec((1, D), row), pl.BlockSpec((1, D), row), pl.BlockSpec((1, D), row),
            pl.BlockSpec((D, 2 * A_WIDTH + B_WIDTH), row, pipeline_mode=pl.Buffered(1)),
            pl.BlockSpec((A_HEADS, A_BLOCK, A_BLOCK), lambda i: (0, 0, 0)),
            pl.BlockSpec((A_BLOCK, A_WIDTH), row),
            pl.BlockSpec((1, A_WIDTH), row),
            pl.BlockSpec((len(POOL_WINDOWS), HEAD, HEAD), lambda i: (0, 0, 0)),
            pl.BlockSpec((1, B_WIDTH), row),
        ],
        out_specs=pl.BlockSpec((TM, A_WIDTH + B_WIDTH), lambda i: (i, 0)),
        scratch_shapes=[pltpu.VMEM((TM + POOL_HALO, B_WIDTH), F32)],
        compiler_params=pltpu.CompilerParams(
            dimension_semantics=("arbitrary",), vmem_limit_bytes=VMEM_LIMIT),
        name="mix_ab",
    )(x, gain, sc, sh, w_uvp, a_ws, bsb, vg, b_w, bscale)


def _split3(x):
    hi = x.astype(BF16)
    rest = x - hi.astype(F32)
    mid = rest.astype(BF16)
    lo = (rest - mid.astype(F32)).astype(BF16)
    return jnp.concatenate([hi, mid, lo], axis=1)


def _mix_c_kernel(x_ref, gain_ref, sc_ref, sh_ref, w_ref, wvt_ref, place_ref, qg_ref, kg_ref,
                  bf_ref, q_ref, k_ref, vt_ref, f2_ref, carry_ref):
    i = pl.program_id(0)
    h = _rms_mod(x_ref[...], gain_ref[...], sc_ref[...], sh_ref[...])
    hb = h.astype(BF16)
    z = jnp.dot(hb, w_ref[...], preferred_element_type=F32)
    lane = lax.broadcasted_iota(jnp.int32, (TM, HEAD), 1)
    q_aug = jnp.where(lane < 3, -1.0, 0.0).astype(BF16)
    qscale = HEAD ** -0.5 * LOG2E
    for hh in range(C_HEADS):
        qh = z[:, hh * HEAD:(hh + 1) * HEAD]
        qn = qh * lax.rsqrt(jnp.mean(qh * qh, axis=-1, keepdims=True) + EPS) * qg_ref[...]
        q_ref[:, 2 * hh * HEAD:(2 * hh + 1) * HEAD] = (qn * qscale).astype(BF16)
        q_ref[:, (2 * hh + 1) * HEAD:(2 * hh + 2) * HEAD] = q_aug
        kh = z[:, C_WIDTH + hh * HEAD:C_WIDTH + (hh + 1) * HEAD]
        kn = kh * lax.rsqrt(jnp.mean(kh * kh, axis=-1, keepdims=True) + EPS) * kg_ref[...]
        k_ref[:, 2 * hh * HEAD:(2 * hh + 1) * HEAD] = kn.astype(BF16)
    vt_ref[...] = _nt_dot(wvt_ref[...], hb).astype(BF16)

    @pl.when(i == 0)
    def _():
        carry_ref[...] = jnp.zeros((1, HEAD), F32)

    logf = jax.nn.log_sigmoid(z[:, 2 * C_WIDTH:] + bf_ref[...])
    r = lax.broadcasted_iota(jnp.int32, (TM, TM), 0)
    c = lax.broadcasted_iota(jnp.int32, (TM, TM), 1)
    lower = jnp.where(c <= r, 1.0, 0.0).astype(BF16)
    cum = jnp.dot(lower, _split3(logf), preferred_element_type=F32)
    f_cum = (cum[:, :HEAD] + cum[:, HEAD:2 * HEAD]) + cum[:, 2 * HEAD:] + carry_ref[...]
    carry_ref[...] = f_cum[TM - 1:TM, :]
    f2 = f_cum * LOG2E
    f2_ref[...] = f2
    aug = jnp.dot(_split3(f2), place_ref[...], preferred_element_type=F32)
    for hh in range(C_HEADS):
        k_ref[:, (2 * hh + 1) * HEAD:(2 * hh + 2) * HEAD] = aug[:, hh * HEAD:(hh + 1) * HEAD].astype(BF16)


def _mix_c(x, gain, sc, sh, w_qkf, w_vt, place, qg, kg, bf):
    row = lambda i: (0, 0)
    tile = pl.BlockSpec((TM, 2 * C_WIDTH), lambda i: (i, 0))
    return pl.pallas_call(
        _mix_c_kernel,
        out_shape=(
            jax.ShapeDtypeStruct((SEQ, 2 * C_WIDTH), BF16),
            jax.ShapeDtypeStruct((SEQ, 2 * C_WIDTH), BF16),
            jax.ShapeDtypeStruct((C_WIDTH, SEQ), BF16),
            jax.ShapeDtypeStruct((SEQ, HEAD), F32),
        ),
        grid=(SEQ // TM,),
        in_specs=[
            pl.BlockSpec((TM, D), lambda i: (i, 0)),
            pl.BlockSpec((1, D), row), pl.BlockSpec((1, D), row), pl.BlockSpec((1, D), row),
            pl.BlockSpec((D, 2 * C_WIDTH + HEAD), row, pipeline_mode=pl.Buffered(1)),
            pl.BlockSpec((C_WIDTH, D), row, pipeline_mode=pl.Buffered(1)),
            pl.BlockSpec((3 * HEAD, C_WIDTH), row),
            pl.BlockSpec((1, HEAD), row), pl.BlockSpec((1, HEAD), row),
            pl.BlockSpec((1, HEAD), row),
        ],
        out_specs=(tile, tile, pl.BlockSpec((C_WIDTH, TM), lambda i: (0, i)),
                   pl.BlockSpec((TM, HEAD), lambda i: (i, 0))),
        scratch_shapes=[pltpu.VMEM((1, HEAD), F32)],
        compiler_params=pltpu.CompilerParams(
            dimension_semantics=("arbitrary",), vmem_limit_bytes=VMEM_LIMIT),
        name="mix_c",
    )(x, gain, sc, sh, w_qkf, w_vt, place, qg, kg, bf)


def _attn_kernel(first_ref, q_ref, k_ref, vt_ref, o_ref, st_ref):
    i = pl.program_id(1)
    first = first_ref[pl.program_id(0) * (SEQ // TQ) + i]

    def scores_t(g, j):
        off = pl.multiple_of(j * TK, TK)
        cols = slice(2 * g * HEAD, 2 * (g + 1) * HEAD)
        return _nt_dot(k_ref[pl.ds(off, TK), cols], q_ref[:, cols])

    def update(g, st, j, m, l, acc):
        m_new = jnp.maximum(m, jnp.max(st, axis=0, keepdims=True))
        alpha = jnp.exp2(m - m_new)
        p = jnp.exp2(st - m_new)
        l_new = alpha * l + jnp.sum(p, axis=0, keepdims=True)
        off = pl.multiple_of(j * TK, TK)
        pv = jnp.dot(vt_ref[g * HEAD:(g + 1) * HEAD, pl.ds(off, TK)], p.astype(BF16),
                     preferred_element_type=F32)
        return m_new, l_new, alpha * acc + pv

    for g in range(ATTN_HEADS):
        st_ref[g] = scores_t(g, first)

    def body(j, carry):
        cur = [st_ref[g] for g in range(ATTN_HEADS)]
        nxt = [scores_t(g, j + 1) for g in range(ATTN_HEADS)]
        out = tuple(update(g, cur[g], j, *carry[g]) for g in range(ATTN_HEADS))
        for g in range(ATTN_HEADS):
            st_ref[g] = nxt[g]
        return out

    init = (jnp.full((1, TQ), -jnp.inf, F32), jnp.zeros((1, TQ), F32), jnp.zeros((HEAD, TQ), F32))
    carry = lax.fori_loop(first, i, body, (init,) * ATTN_HEADS)
    kpos = lax.broadcasted_iota(jnp.int32, (TK, TQ), 0)
    qpos = lax.broadcasted_iota(jnp.int32, (TK, TQ), 1)
    for g in range(ATTN_HEADS):
        st = jnp.where(kpos <= qpos, st_ref[g], -jnp.inf)
        m, l, acc = update(g, st, i, *carry[g])
        o_ref[:, g * HEAD:(g + 1) * HEAD] = (acc / l).T.astype(BF16)


def _first_key_block(f2, qg, kg):
    nq = SEQ // TQ
    bound = LOG2E * HEAD ** 0.5 * jnp.max(jnp.abs(qg)) * jnp.max(jnp.abs(kg)) * 1.02
    f_start = f2[0::TQ, :C_HEADS]
    f_end = f2[TK - 1::TK, :C_HEADS]
    gap = f_end[None, :, :] - f_start[:, None, :]
    needed = gap <= 2.0 * bound + ATTN_DROP_LOG2
    needed = jnp.any(needed.reshape(nq, SEQ // TK, C_HEADS // ATTN_HEADS, ATTN_HEADS), axis=-1)
    first = jnp.argmax(needed, axis=1)
    return first.T.reshape(-1).astype(jnp.int32)


def _attn(first, qa, ka, vt):
    gw = ATTN_HEADS * HEAD
    return pl.pallas_call(
        _attn_kernel,
        out_shape=jax.ShapeDtypeStruct((SEQ, C_WIDTH), BF16),
        grid_spec=pltpu.PrefetchScalarGridSpec(
            num_scalar_prefetch=1,
            grid=(C_HEADS // ATTN_HEADS, SEQ // TQ),
            in_specs=[
                pl.BlockSpec((TQ, 2 * gw), lambda h, i, f: (i, h)),
                pl.BlockSpec((SEQ, 2 * gw), lambda h, i, f: (0, h), pipeline_mode=pl.Buffered(1)),
                pl.BlockSpec((gw, SEQ), lambda h, i, f: (h, 0), pipeline_mode=pl.Buffered(1)),
            ],
            out_specs=pl.BlockSpec((TQ, gw), lambda h, i, f: (i, h)),
            scratch_shapes=[pltpu.VMEM((ATTN_HEADS, TK, TQ), F32)],
        ),
        compiler_params=pltpu.CompilerParams(
            dimension_semantics=("arbitrary", "arbitrary"), vmem_limit_bytes=VMEM_LIMIT),
        name="attn",
    )(first, qa, ka, vt)


def _first_max(vals):
    best, idx = vals[0], jnp.zeros(vals[0].shape, jnp.int32)
    for j in range(1, len(vals)):
        better = vals[j] > best
        idx = jnp.where(better, j, idx)
        best = jnp.where(better, vals[j], best)
    return idx, best


def _pick(idx, vals):
    out = vals[-1]
    for j in range(len(vals) - 2, -1, -1):
        out = jnp.where(idx == j, vals[j], out)
    return out


def _out_kernel(yab_ref, yc_ref, x_ref, w_ref, g1_ref, gain_ref, sc_ref, sh_ref, wr_ref, br_ref,
                x1_ref, h2_ref, info_ref, cnt_ref, carry_ref):
    i = pl.program_id(0)
    y = jnp.dot(yab_ref[...], w_ref[0:A_WIDTH + B_WIDTH, :], preferred_element_type=F32)
    y = y + jnp.dot(yc_ref[...], w_ref[A_WIDTH + B_WIDTH:, :], preferred_element_type=F32)
    x1 = x_ref[...] + g1_ref[...] * y
    x1_ref[...] = x1
    h2 = _rms_mod(x1, gain_ref[...], sc_ref[...], sh_ref[...])
    h2_ref[:, 0, :] = h2

    logits = _nt_dot(wr_ref[...], h2.astype(BF16))
    scores = jax.nn.sigmoid(logits)
    sel = scores + br_ref[...]
    sel_r = [sel[k:k + 1, :] for k in range(N_EXPERTS)]
    sc_r = [scores[k:k + 1, :] for k in range(N_EXPERTS)]
    grp = []
    for g in range(N_GROUPS):
        a = sel_r[GROUP * g:GROUP * (g + 1)]
        pair = [a[p] + a[q] for p in range(GROUP) for q in range(p + 1, GROUP)]
        grp.append(functools.reduce(jnp.maximum, pair))
    gi, _ = _first_max(grp)
    cand = [_pick(gi, [sel_r[GROUP * g + j] for g in range(N_GROUPS)]) for j in range(GROUP)]
    cand_s = [_pick(gi, [sc_r[GROUP * g + j] for g in range(N_GROUPS)]) for j in range(GROUP)]
    i0, _ = _first_max(cand)
    i1, _ = _first_max([jnp.where(i0 == j, -jnp.inf, cand[j]) for j in range(GROUP)])
    s0, s1 = _pick(i0, cand_s), _pick(i1, cand_s)
    e0, e1 = GROUP * gi + i0, GROUP * gi + i1
    den = s0 + s1

    @pl.when(i == 0)
    def _():
        carry_ref[...] = jnp.zeros((N_EXPERTS, HEAD), F32)

    ek = lax.broadcasted_iota(jnp.int32, (N_EXPERTS, TM), 0)
    oh0 = (ek == e0).astype(F32)
    oh1 = (ek == e1).astype(F32)
    both = oh0 + oh1
    r = lax.broadcasted_iota(jnp.int32, (TM, TM), 0)
    c = lax.broadcasted_iota(jnp.int32, (TM, TM), 1)
    before = (r < c).astype(BF16)
    run = jnp.dot(both.astype(BF16), before, preferred_element_type=F32) + carry_ref[:, 0:1]
    rank0 = jnp.sum(run * oh0, axis=0, keepdims=True)
    rank1 = jnp.sum(run * oh1, axis=0, keepdims=True)
    total = carry_ref[...] + jnp.sum(both, axis=-1, keepdims=True)
    carry_ref[...] = total
    cnt_ref[...] = total

    info_ref[0:1, :] = e0.astype(F32)
    info_ref[1:2, :] = e1.astype(F32)
    info_ref[2:3, :] = s0 / den
    info_ref[3:4, :] = s1 / den
    info_ref[4:5, :] = rank0
    info_ref[5:6, :] = rank1
    info_ref[6:8, :] = jnp.zeros((2, TM), F32)


def _out_proj(yab, yc, x, w_out, g1, gain, sc, sh, wr_t, br):
    row = lambda i: (0, 0)
    return pl.pallas_call(
        _out_kernel,
        out_shape=(
            jax.ShapeDtypeStruct((SEQ, D), F32),
            jax.ShapeDtypeStruct((SEQ, 1, D), F32),
            jax.ShapeDtypeStruct((8, SEQ), F32),
            jax.ShapeDtypeStruct((N_EXPERTS, HEAD), F32),
        ),
        grid=(SEQ // TM,),
        in_specs=[
            pl.BlockSpec((TM, A_WIDTH + B_WIDTH), lambda i: (i, 0)),
            pl.BlockSpec((TM, C_WIDTH), lambda i: (i, 0)),
            pl.BlockSpec((TM, D), lambda i: (i, 0)),
            pl.BlockSpec((D, D), row, pipeline_mode=pl.Buffered(1)),
            pl.BlockSpec((1, D), row), pl.BlockSpec((1, D), row),
            pl.BlockSpec((1, D), row), pl.BlockSpec((1, D), row),
            pl.BlockSpec((N_EXPERTS, D), row),
            pl.BlockSpec((N_EXPERTS, 1), row),
        ],
        out_specs=(
            pl.BlockSpec((TM, D), lambda i: (i, 0)),
            pl.BlockSpec((TM, 1, D), lambda i: (i, 0, 0)),
            pl.BlockSpec((8, TM), lambda i: (0, i)),
            pl.BlockSpec((N_EXPERTS, HEAD), row),
        ),
        scratch_shapes=[pltpu.VMEM((N_EXPERTS, HEAD), F32)],
        compiler_params=pltpu.CompilerParams(
            dimension_semantics=("arbitrary",), vmem_limit_bytes=VMEM_LIMIT),
        name="out_proj",
    )(yab, yc, x, w_out, g1, gain, sc, sh, wr_t, br)


def _moe_kernel(layer, dest_ref, bexp_ref, nvalid_ref, nused_ref, newexp_ref, nextexp_ref,
                h_hbm, wg_hbm, wu_hbm, wd_hbm, y_hbm,
                code_ref, xs_ref, ys_ref, xb_ref, hid_ref, wg32_ref, wu32_ref, wd32_ref,
                wg_ref, wu_ref, wd_ref, gsem, ssem, wsem):
    b = pl.program_id(0)
    n_used = nused_ref[0]
    slot = b % 2

    def weight_copies(e):
        return (pltpu.make_async_copy(wg_hbm.at[layer, e], wg32_ref, wsem.at[0]),
                pltpu.make_async_copy(wu_hbm.at[layer, e], wu32_ref, wsem.at[1]),
                pltpu.make_async_copy(wd_hbm.at[layer, e], wd32_ref, wsem.at[2]))

    def switch_expert():
        for cp in weight_copies(bexp_ref[b]):
            cp.wait()

        def cast(k, _):
            r = pl.multiple_of(k * 256, 256)
            wg_ref[pl.ds(r, 256), :] = wg32_ref[pl.ds(r, 256), :].astype(BF16)
            wu_ref[pl.ds(r, 256), :] = wu32_ref[pl.ds(r, 256), :].astype(BF16)
            r = pl.multiple_of(k * 128, 128)
            wd_ref[pl.ds(r, 128), :] = wd32_ref[pl.ds(r, 128), :].astype(BF16)
            return 0
        lax.fori_loop(0, D // 256, cast, 0)

        @pl.when(nextexp_ref[b] >= 0)
        def _():
            for cp in weight_copies(jnp.maximum(nextexp_ref[b], 0)):
                cp.start()

    def gather_copy(blk, r, s):
        tok = code_ref[blk * MOE_BLK + r] & (SEQ - 1)
        return pltpu.make_async_copy(h_hbm.at[tok], xs_ref.at[s, pl.ds(r, 1), :],
                                     gsem.at[s])

    def scatter_copy(blk, r, s):
        dst = code_ref[blk * MOE_BLK + r]
        return pltpu.make_async_copy(ys_ref.at[s, pl.ds(r, 1), :], y_hbm.at[dst],
                                     ssem.at[s])

    def for_rows(n, fn):
        def body(r, _):
            fn(r)
            return 0
        lax.fori_loop(0, n, body, 0)

    def scatter_wait(n, s):
        n8 = pl.multiple_of((n >> 3) << 3, 8)

        @pl.when(n8 > 0)
        def _():
            pltpu.make_async_copy(ys_ref.at[s, pl.ds(0, n8), :], ys_ref.at[s, pl.ds(0, n8), :],
                                  ssem.at[s]).wait()

        for r in range(7):
            @pl.when(n8 + r < n)
            def _():
                pltpu.make_async_copy(ys_ref.at[s, pl.ds(r, 1), :], y_hbm.at[r],
                                      ssem.at[s]).wait()

    @pl.when(b == 0)
    def _():
        for cp in weight_copies(bexp_ref[0]):
            cp.start()

        def fill(r, _):
            code_ref[r] = 0
            return 0
        lax.fori_loop(0, MOE_ROWS, fill, 0, unroll=8)

        def place(a, _):
            code_ref[dest_ref[a]] = a
            return 0
        lax.fori_loop(0, N_ASSIGN, place, 0, unroll=8)
        for_rows(MOE_BLK, lambda r: gather_copy(0, r, 0).start())

    def step(prefetch):
        @pl.when(newexp_ref[b] == 1)
        def _():
            switch_expert()

        for r in range(MOE_BLK):
            gather_copy(b, r, slot).wait()

        xb_ref[...] = xs_ref[slot].astype(BF16)
        prev = jnp.maximum(b - 1, 0)
        n_prev = jnp.where(b >= 1, nvalid_ref[prev], 0)
        x = xb_ref[...]
        rows_per = MOE_BLK // MOE_CHUNKS

        hc = D_EXPERT // MOE_CHUNKS
        for c in range(MOE_CHUNKS):
            cs = slice(c * hc, (c + 1) * hc)
            gate = jnp.dot(x, wg_ref[:, cs], preferred_element_type=F32)
            up = jnp.dot(x, wu_ref[:, cs], preferred_element_type=F32)
            hid = jax.nn.silu(gate) * up
            if prefetch:
                for r in range(c * rows_per, (c + 1) * rows_per):
                    gather_copy(b + 1, r, 1 - slot).start(priority=r % 2)
                tie = pltpu.bitcast(xs_ref[slot, 0:8, 0:hc], jnp.uint32)
                tie = pltpu.bitcast((tie >> 16) >> 16, F32)
                hid = jnp.concatenate([hid[0:8] + tie, hid[8:]], axis=0)
            hid_ref[:, cs] = hid.astype(BF16)

        scatter_wait(jnp.where(b >= 2, nvalid_ref[jnp.maximum(b - 2, 0)], 0), slot)
        hidb = hid_ref[...]
        oc = D // MOE_CHUNKS
        for c in range(MOE_CHUNKS):
            for r in range(c * rows_per, (c + 1) * rows_per):
                @pl.when(r < n_prev)
                def _():
                    scatter_copy(prev, r, 1 - slot).start(priority=r % 2)
            cs = slice(c * oc, (c + 1) * oc)
            ys_ref[slot, :, cs] = jnp.dot(hidb, wd_ref[:, cs], preferred_element_type=F32)

    @pl.when(b < n_used - 1)
    def _():
        step(True)

    @pl.when(b == n_used - 1)
    def _():
        step(False)
        for_rows(nvalid_ref[b], lambda r: scatter_copy(b, r, slot).start())
        scatter_wait(nvalid_ref[b], slot)
        scatter_wait(jnp.where(b >= 1, nvalid_ref[jnp.maximum(b - 1, 0)], 0), 1 - slot)


def _moe(layer, dest, bexp, nvalid, nused, newexp, nextexp, h2, wg, wu, wd):
    hbm = pl.BlockSpec(memory_space=pl.ANY)
    return pl.pallas_call(
        functools.partial(_moe_kernel, layer),
        out_shape=jax.ShapeDtypeStruct((N_ASSIGN, 1, D), F32),
        grid_spec=pltpu.PrefetchScalarGridSpec(
            num_scalar_prefetch=6,
            grid=(MOE_NB,),
            in_specs=[hbm, hbm, hbm, hbm],
            out_specs=hbm,
            scratch_shapes=[
                pltpu.SMEM((MOE_ROWS,), jnp.int32),
                pltpu.VMEM((2, MOE_BLK, D), F32),
                pltpu.VMEM((2, MOE_BLK, D), F32),
                pltpu.VMEM((MOE_BLK, D), BF16),
                pltpu.VMEM((MOE_BLK, D_EXPERT), BF16),
                pltpu.VMEM((D, D_EXPERT), F32),
                pltpu.VMEM((D, D_EXPERT), F32),
                pltpu.VMEM((D_EXPERT, D), F32),
                pltpu.VMEM((D, D_EXPERT), BF16),
                pltpu.VMEM((D, D_EXPERT), BF16),
                pltpu.VMEM((D_EXPERT, D), BF16),
                pltpu.SemaphoreType.DMA((2,)),
                pltpu.SemaphoreType.DMA((2,)),
                pltpu.SemaphoreType.DMA((3,)),
            ],
        ),
        compiler_params=pltpu.CompilerParams(
            dimension_semantics=("arbitrary",), vmem_limit_bytes=VMEM_LIMIT),
        name="moe",
    )(dest, bexp, nvalid, nused, newexp, nextexp, h2, wg, wu, wd)


def _combine_kernel(x1_ref, y0_ref, y1_ref, gw_ref, g2_ref, o_ref):
    gw = gw_ref[...]
    moe = gw[:, 0:1] * y0_ref[:, 0, :] + gw[:, 1:2] * y1_ref[:, 0, :]
    o_ref[...] = x1_ref[...] + g2_ref[...] * moe


def _combine(x1, y, gw, g2):
    nt = SEQ // TM
    return pl.pallas_call(
        _combine_kernel,
        out_shape=jax.ShapeDtypeStruct((SEQ, D), F32),
        grid=(nt,),
        in_specs=[
            pl.BlockSpec((TM, D), lambda i: (i, 0)),
            pl.BlockSpec((TM, 1, D), lambda i: (i, 0, 0)),
            pl.BlockSpec((TM, 1, D), lambda i: (i + nt, 0, 0)),
            pl.BlockSpec((TM, 2), lambda i: (i, 0)),
            pl.BlockSpec((1, D), lambda i: (0, 0)),
        ],
        out_specs=pl.BlockSpec((TM, D), lambda i: (i, 0)),
        compiler_params=pltpu.CompilerParams(
            dimension_semantics=("arbitrary",), vmem_limit_bytes=VMEM_LIMIT),
        name="combine",
    )(x1, y, y, gw, g2)


def _dispatch_plan(info, counts):
    cnt = counts[:, 0].astype(jnp.int32)
    padded = (cnt + MOE_BLK - 1) // MOE_BLK * MOE_BLK
    ends = jnp.cumsum(padded)
    pad_start = ends - padded
    e = info[0:2].astype(jnp.int32)
    rank = info[4:6].astype(jnp.int32)
    onehot = e[:, :, None] == jnp.arange(N_EXPERTS, dtype=jnp.int32)
    dest = jnp.sum(jnp.where(onehot, pad_start, 0), axis=-1) + rank
    blk_start = jnp.arange(MOE_NB, dtype=jnp.int32) * MOE_BLK
    bexp = jnp.sum(blk_start[:, None] >= ends[None, :], axis=-1)
    bexp = jnp.minimum(bexp, N_EXPERTS - 1).astype(jnp.int32)
    in_blk = jnp.arange(N_EXPERTS, dtype=jnp.int32)[None, :] == bexp[:, None]
    left = jnp.sum(jnp.where(in_blk, cnt + pad_start, 0), axis=-1) - blk_start
    nvalid = jnp.clip(left, 0, MOE_BLK).astype(jnp.int32)
    nused = (ends[-1:] // MOE_BLK).astype(jnp.int32)
    used = jnp.arange(MOE_NB, dtype=jnp.int32) < nused[0]
    newexp = jnp.concatenate([jnp.ones((1,), jnp.int32), (bexp[1:] != bexp[:-1]).astype(jnp.int32)])
    later = (bexp[None, :] > bexp[:, None]) & used[None, :]
    nextexp = jnp.min(jnp.where(later, bexp[None, :], N_EXPERTS), axis=1)
    nextexp = jnp.where(nextexp < N_EXPERTS, nextexp, -1).astype(jnp.int32)
    return dest.reshape(N_ASSIGN), bexp, nvalid, nused, newexp, nextexp


def kernel(x, c, w_ada, b_ada, g_mix, g_ffn, w_in, a_ws, a_bs, a_vg, b_w, b_scale,
           c_qg, c_kg, c_bf, w_out, w_router, b_router, e_gate, e_up, e_down):
    xs = x.reshape(SEQ, D)
    mod = _ada(c, w_ada, b_ada)
    wr_t = w_router.T.astype(BF16)
    br = b_router.reshape(N_EXPERTS, 1).astype(F32)
    n_uvp = 2 * A_WIDTH + B_WIDTH
    place = np.zeros((3 * HEAD, C_WIDTH), np.float32)
    for pj in range(3):
        for ph in range(C_HEADS):
            place[pj * HEAD + ph, ph * HEAD + pj] = 1.0
    place = jnp.asarray(place, BF16)
    for l in range(DEPTH):
        sh1, sc1, g1, sh2, sc2, g2 = [mod[l, :, j * D:(j + 1) * D] for j in range(6)]
        gain1 = g_mix[l].reshape(1, D)
        w_uvp = w_in[l, :, :n_uvp].astype(BF16)
        w_qk = w_in[l, :, n_uvp:n_uvp + 2 * C_WIDTH].astype(BF16)
        wf = jnp.pad(w_in[l, :, n_uvp + 3 * C_WIDTH:], ((0, 0), (0, HEAD - C_HEADS))).astype(BF16)
        w_qkf = jnp.concatenate([w_qk, wf], axis=1)
        w_vt = w_in[l, :, n_uvp + 2 * C_WIDTH:n_uvp + 3 * C_WIDTH].T.astype(BF16)
        bf =jnp.pad(c_bf[l], (0, HEAD - C_HEADS)).reshape(1, HEAD)
        bsb = jnp.repeat(a_bs[l].T, HEAD, axis=1)
        yab = _mix_ab(xs, gain1, sc1, sh1, w_uvp, a_ws[l], bsb, a_vg[l].reshape(1, A_WIDTH),
                      b_w[l], b_scale[l].reshape(1, B_WIDTH))
        qa, ka, vt, f2 = _mix_c(xs, gain1, sc1, sh1, w_qkf, w_vt, place, c_qg[l].reshape(1, HEAD),
                            c_kg[l].reshape(1, HEAD), bf)
        yc = _attn(_first_key_block(f2, c_qg[l], c_kg[l]), qa, ka, vt)
        x1, h2, info, counts = _out_proj(yab, yc, xs, w_out[l].astype(BF16), g1,
                                         g_ffn[l].reshape(1, D), sc2, sh2, wr_t, br)
        plan = _dispatch_plan(info, counts)
        y = _moe(l, *plan, h2, e_gate, e_up, e_down)
        xs = _combine(x1, y, info[2:4].T, g2)
    return xs.reshape(1, SEQ, D)
```

```python
import functools

import jax
import jax.numpy as jnp
import numpy as np
from jax import lax
from jax.experimental import pallas as pl
from jax.experimental.pallas import tpu as pltpu

F32 = jnp.float32
BF16 = jnp.bfloat16

D = 2048
SEQ = 8192
DEPTH = 2
CHUNK = 64
A_WIDTH = 512
A_HEADS = 4
A_BLOCK = 128
B_WIDTH = 512
POOL_WINDOWS = (2, 4, 8, 16)
POOL_HALO = 16
C_WIDTH = 1024
C_HEADS = 8
HEAD = 128
N_EXPERTS = 16
N_GROUPS = 4
GROUP = 4
D_EXPERT = 1024
EPS = 1e-6
LOG2E = 1.4426950408889634

TM = 512
TQ = 512
TK = 512
ATTN_HEADS = 4
ATTN_DROP_LOG2 = 64.0
MOE_BLK = 256
MOE_CHUNKS = 4
N_ASSIGN = 2 * SEQ
MOE_ROWS = N_ASSIGN + N_EXPERTS * MOE_BLK
MOE_NB = MOE_ROWS // MOE_BLK

VMEM_LIMIT = 56 * 1024 * 1024


def _nt_dot(a, b):
    return lax.dot_general(a, b, (((1,), (1,)), ((), ())), preferred_element_type=F32)


def _rms_mod(x, gain, scale, shift):
    ms = jnp.mean(x * x, axis=-1, keepdims=True)
    return (x * lax.rsqrt(ms + EPS) * gain) * (1.0 + scale) + shift


def _ada_kernel(c_ref, w_ref, b_ref, o_ref):
    ca = jax.nn.silu(c_ref[...]).astype(BF16)
    r = jnp.dot(ca, w_ref[...].astype(BF16), preferred_element_type=F32)
    o_ref[...] = r[0:1, :] + b_ref[...]


def _ada(c, w_ada, b_ada):
    tn = 1024
    c8 = jnp.broadcast_to(c, (16, D))
    return pl.pallas_call(
        _ada_kernel,
        out_shape=jax.ShapeDtypeStruct((DEPTH, 1, 6 * D), F32),
        grid=(DEPTH, 6 * D // tn),
        in_specs=[
            pl.BlockSpec((16, D), lambda l, j: (0, 0)),
            pl.BlockSpec((None, D, tn), lambda l, j: (l, 0, j)),
            pl.BlockSpec((None, 1, tn), lambda l, j: (l, 0, j)),
        ],
        out_specs=pl.BlockSpec((None, 1, tn), lambda l, j: (l, 0, j)),
        compiler_params=pltpu.CompilerParams(
            dimension_semantics=("arbitrary", "arbitrary"), vmem_limit_bytes=VMEM_LIMIT),
        name="ada",
    )(c8, w_ada, b_ada.reshape(DEPTH, 1, 6 * D))


def _mix_ab_kernel(x_ref, gain_ref, sc_ref, sh_ref, w_ref, ws_ref, bsb_ref, vg_ref, bw_ref,
                   bscale_ref, y_ref, pext_ref):
    i = pl.program_id(0)
    h = _rms_mod(x_ref[...], gain_ref[...], sc_ref[...], sh_ref[...])
    z = jnp.dot(h.astype(BF16), w_ref[...], preferred_element_type=F32)
    u = jax.nn.gelu(z[:, :A_WIDTH])
    v = jax.nn.gelu(z[:, A_WIDTH:2 * A_WIDTH])
    p = z[:, 2 * A_WIDTH:]

    cid_t = lax.broadcasted_iota(jnp.int32, (A_BLOCK, A_BLOCK), 0) // CHUNK
    cid_s = lax.broadcasted_iota(jnp.int32, (A_BLOCK, A_BLOCK), 1) // CHUNK
    mask = cid_s <= cid_t
    for hh in range(A_HEADS):
        cs = slice(hh * HEAD, (hh + 1) * HEAD)
        vh = v[:, cs]
        vn = vh * lax.rsqrt(jnp.mean(vh * vh, axis=-1, keepdims=True) + EPS) * vg_ref[:, cs]
        vnb = vn.astype(BF16)
        wm = jnp.where(mask, ws_ref[hh], 0.0).astype(BF16)
        for n in range(TM // A_BLOCK):
            rs = slice(n * A_BLOCK, (n + 1) * A_BLOCK)
            sp = jnp.dot(wm, vnb[rs], preferred_element_type=F32) + bsb_ref[:, cs]
            y_ref[rs, cs] = (u[rs, cs] * sp).astype(BF16)

    @pl.when(i == 0)
    def _():
        pext_ref[0:POOL_HALO, :] = jnp.zeros((POOL_HALO, B_WIDTH), F32)

    pext_ref[POOL_HALO:POOL_HALO + TM, :] = p
    t1 = i * TM + lax.broadcasted_iota(jnp.int32, (TM, 1), 0) + 1
    for g, w in enumerate(POOL_WINDOWS):
        cs = slice(g * HEAD, (g + 1) * HEAD)
        acc = p[:, cs]
        for j in range(1, w):
            acc = acc + pext_ref[POOL_HALO - j:POOL_HALO - j + TM, cs]
        cnt = jnp.minimum(t1, w).astype(F32)
        d = acc / cnt - p[:, cs]
        yb = jnp.dot(d.astype(BF16), bw_ref[g].astype(BF16), preferred_element_type=F32)
        y_ref[:, A_WIDTH + g * HEAD:A_WIDTH + (g + 1) * HEAD] = (yb * bscale_ref[:, cs]).astype(BF16)
    pext_ref[0:POOL_HALO, :] = pext_ref[TM:TM + POOL_HALO, :]


def _mix_ab(x, gain, sc, sh, w_uvp, a_ws, bsb, vg, b_w, bscale):
    row = lambda i: (0, 0)
    return pl.pallas_call(
        _mix_ab_kernel,
        out_shape=jax.ShapeDtypeStruct((SEQ, A_WIDTH + B_WIDTH), BF16),
        grid=(SEQ // TM,),
        in_specs=[
            pl.BlockSpec((TM, D), lambda i: (i, 0)),
            pl.BlockSpec((1, D), row), pl.BlockSpec((1, D), row), pl.BlockSpec((1, D), row),
            pl.BlockSpec((D, 2 * A_WIDTH + B_WIDTH), row, pipeline_mode=pl.Buffered(1)),
            pl.BlockSpec((A_HEADS, A_BLOCK, A_BLOCK), lambda i: (0, 0, 0)),
            pl.BlockSpec((A_BLOCK, A_WIDTH), row),
            pl.BlockSpec((1, A_WIDTH), row),
            pl.BlockSpec((len(POOL_WINDOWS), HEAD, HEAD), lambda i: (0, 0, 0)),
            pl.BlockSpec((1, B_WIDTH), row),
        ],
        out_specs=pl.BlockSpec((TM, A_WIDTH + B_WIDTH), lambda i: (i, 0)),
        scratch_shapes=[pltpu.VMEM((TM + POOL_HALO, B_WIDTH), F32)],
        compiler_params=pltpu.CompilerParams(
            dimension_semantics=("arbitrary",), vmem_limit_bytes=VMEM_LIMIT),
        name="mix_ab",
    )(x, gain, sc, sh, w_uvp, a_ws, bsb, vg, b_w, bscale)


def _split3(x):
    hi = x.astype(BF16)
    rest = x - hi.astype(F32)
    mid = rest.astype(BF16)
    lo = (rest - mid.astype(F32)).astype(BF16)
    return jnp.concatenate([hi, mid, lo], axis=1)


def _mix_c_kernel(x_ref, gain_ref, sc_ref, sh_ref, w_ref, wvt_ref, place_ref, qg_ref, kg_ref,
                  bf_ref, q_ref, k_ref, vt_ref, f2_ref, carry_ref):
    i = pl.program_id(0)
    h = _rms_mod(x_ref[...], gain_ref[...], sc_ref[...], sh_ref[...])
    hb = h.astype(BF16)
    z = jnp.dot(hb, w_ref[...], preferred_element_type=F32)
    lane = lax.broadcasted_iota(jnp.int32, (TM, HEAD), 1)
    q_aug = jnp.where(lane < 3, -1.0, 0.0).astype(BF16)
    qscale = HEAD ** -0.5 * LOG2E
    for hh in range(C_HEADS):
        qh = z[:, hh * HEAD:(hh + 1) * HEAD]
        qn = qh * lax.rsqrt(jnp.mean(qh * qh, axis=-1, keepdims=True) + EPS) * qg_ref[...]
        q_ref[:, 2 * hh * HEAD:(2 * hh + 1) * HEAD] = (qn * qscale).astype(BF16)
        q_ref[:, (2 * hh + 1) * HEAD:(2 * hh + 2) * HEAD] = q_aug
        kh = z[:, C_WIDTH + hh * HEAD:C_WIDTH + (hh + 1) * HEAD]
        kn = kh * lax.rsqrt(jnp.mean(kh * kh, axis=-1, keepdims=True) + EPS) * kg_ref[...]
        k_ref[:, 2 * hh * HEAD:(2 * hh + 1) * HEAD] = kn.astype(BF16)
    vt_ref[...] = _nt_dot(wvt_ref[...], hb).astype(BF16)

    @pl.when(i == 0)
    def _():
        carry_ref[...] = jnp.zeros((1, HEAD), F32)

    logf = jax.nn.log_sigmoid(z[:, 2 * C_WIDTH:] + bf_ref[...])
    r = lax.broadcasted_iota(jnp.int32, (TM, TM), 0)
    c = lax.broadcasted_iota(jnp.int32, (TM, TM), 1)
    lower = jnp.where(c <= r, 1.0, 0.0).astype(BF16)
    cum = jnp.dot(lower, _split3(logf), preferred_element_type=F32)
    f_cum = (cum[:, :HEAD] + cum[:, HEAD:2 * HEAD]) + cum[:, 2 * HEAD:] + carry_ref[...]
    carry_ref[...] = f_cum[TM - 1:TM, :]
    f2 = f_cum * LOG2E
    f2_ref[...] = f2
    aug = jnp.dot(_split3(f2), place_ref[...], preferred_element_type=F32)
    for hh in range(C_HEADS):
        k_ref[:, (2 * hh + 1) * HEAD:(2 * hh + 2) * HEAD] = aug[:, hh * HEAD:(hh + 1) * HEAD].astype(BF16)


def _mix_c(x, gain, sc, sh, w_qkf, w_vt, place, qg, kg, bf):
    row = lambda i: (0, 0)
    tile = pl.BlockSpec((TM, 2 * C_WIDTH), lambda i: (i, 0))
    return pl.pallas_call(
        _mix_c_kernel,
        out_shape=(
            jax.ShapeDtypeStruct((SEQ, 2 * C_WIDTH), BF16),
            jax.ShapeDtypeStruct((SEQ, 2 * C_WIDTH), BF16),
            jax.ShapeDtypeStruct((C_WIDTH, SEQ), BF16),
            jax.ShapeDtypeStruct((SEQ, HEAD), F32),
        ),
        grid=(SEQ // TM,),
        in_specs=[
            pl.BlockSpec((TM, D), lambda i: (i, 0)),
            pl.BlockSpec((1, D), row), pl.BlockSpec((1, D), row), pl.BlockSpec((1, D), row),
            pl.BlockSpec((D, 2 * C_WIDTH + HEAD), row, pipeline_mode=pl.Buffered(1)),
            pl.BlockSpec((C_WIDTH, D), row, pipeline_mode=pl.Buffered(1)),
            pl.BlockSpec((3 * HEAD, C_WIDTH), row),
            pl.BlockSpec((1, HEAD), row), pl.BlockSpec((1, HEAD), row),
            pl.BlockSpec((1, HEAD), row),
        ],
        out_specs=(tile, tile, pl.BlockSpec((C_WIDTH, TM), lambda i: (0, i)),
                   pl.BlockSpec((TM, HEAD), lambda i: (i, 0))),
        scratch_shapes=[pltpu.VMEM((1, HEAD), F32)],
        compiler_params=pltpu.CompilerParams(
            dimension_semantics=("arbitrary",), vmem_limit_bytes=VMEM_LIMIT),
        name="mix_c",
    )(x, gain, sc, sh, w_qkf, w_vt, place, qg, kg, bf)


def _attn_kernel(first_ref, q_ref, k_ref, vt_ref, o_ref, st_ref):
    i = pl.program_id(1)
    first = first_ref[pl.program_id(0) * (SEQ // TQ) + i]

    def scores_t(g, j):
        off = pl.multiple_of(j * TK, TK)
        cols = slice(2 * g * HEAD, 2 * (g + 1) * HEAD)
        return _nt_dot(k_ref[pl.ds(off, TK), cols], q_ref[:, cols])

    def update(g, read_st, j, m, l, acc):
        m_new = jnp.maximum(m, jnp.max(read_st(), axis=0, keepdims=True))
        alpha = jnp.exp2(m - m_new)
        p = jnp.exp2(read_st() - m_new)
        l_new = alpha * l + jnp.sum(p, axis=0, keepdims=True)
        off = pl.multiple_of(j * TK, TK)
        pv = jnp.dot(vt_ref[g * HEAD:(g + 1) * HEAD, pl.ds(off, TK)], p.astype(BF16),
                     preferred_element_type=F32)
        return m_new, l_new, alpha * acc + pv

    for g in range(ATTN_HEADS):
        st_ref[g] = scores_t(g, first)

    def body(j, carry):
        out = []
        for g in range(ATTN_HEADS):
            out.append(update(g, lambda g=g: st_ref[g], j, *carry[g]))
            st_ref[g] = scores_t(g, j + 1)
        return tuple(out)

    init = (jnp.full((1, TQ), -jnp.inf, F32), jnp.zeros((1, TQ), F32), jnp.zeros((HEAD, TQ), F32))
    carry = lax.fori_loop(first, i, body, (init,) * ATTN_HEADS)
    kpos = lax.broadcasted_iota(jnp.int32, (TK, TQ), 0)
    qpos = lax.broadcasted_iota(jnp.int32, (TK, TQ), 1)
    for g in range(ATTN_HEADS):
        m, l, acc = update(g, lambda g=g: jnp.where(kpos <= qpos, st_ref[g], -jnp.inf), i, *carry[g])
        o_ref[:, g * HEAD:(g + 1) * HEAD] = (acc / l).T.astype(BF16)


def _first_key_block(f2, qg, kg):
    nq = SEQ // TQ
    bound = LOG2E * HEAD ** 0.5 * jnp.max(jnp.abs(qg)) * jnp.max(jnp.abs(kg)) * 1.02
    f_start = f2[0::TQ, :C_HEADS]
    f_end = f2[TK - 1::TK, :C_HEADS]
    gap = f_end[None, :, :] - f_start[:, None, :]
    needed = gap <= 2.0 * bound + ATTN_DROP_LOG2
    needed = jnp.any(needed.reshape(nq, SEQ // TK, C_HEADS // ATTN_HEADS, ATTN_HEADS), axis=-1)
    first = jnp.argmax(needed, axis=1)
    return first.T.reshape(-1).astype(jnp.int32)


def _attn(first, qa, ka, vt):
    gw = ATTN_HEADS * HEAD
    return pl.pallas_call(
        _attn_kernel,
        out_shape=jax.ShapeDtypeStruct((SEQ, C_WIDTH), BF16),
        grid_spec=pltpu.PrefetchScalarGridSpec(
            num_scalar_prefetch=1,
            grid=(C_HEADS // ATTN_HEADS, SEQ // TQ),
            in_specs=[
                pl.BlockSpec((TQ, 2 * gw), lambda h, i, f: (i, h)),
                pl.BlockSpec((SEQ, 2 * gw), lambda h, i, f: (0, h), pipeline_mode=pl.Buffered(1)),
                pl.BlockSpec((gw, SEQ), lambda h, i, f: (h, 0), pipeline_mode=pl.Buffered(1)),
            ],
            out_specs=pl.BlockSpec((TQ, gw), lambda h, i, f: (i, h)),
            scratch_shapes=[pltpu.VMEM((ATTN_HEADS, TK, TQ), F32)],
        ),
        compiler_params=pltpu.CompilerParams(
            dimension_semantics=("arbitrary", "arbitrary"), vmem_limit_bytes=VMEM_LIMIT),
        name="attn",
    )(first, qa, ka, vt)


def _first_max(vals):
    best, idx = vals[0], jnp.zeros(vals[0].shape, jnp.int32)
    for j in range(1, len(vals)):
        better = vals[j] > best
        idx = jnp.where(better, j, idx)
        best = jnp.where(better, vals[j], best)
    return idx, best


def _pick(idx, vals):
    out = vals[-1]
    for j in range(len(vals) - 2, -1, -1):
        out = jnp.where(idx == j, vals[j], out)
    return out


def _out_kernel(yab_ref, yc_ref, x_ref, w_ref, g1_ref, gain_ref, sc_ref, sh_ref, wr_ref, br_ref,
                x1_ref, h2_ref, info_ref, cnt_ref, carry_ref):
    i = pl.program_id(0)
    y = jnp.dot(yab_ref[...], w_ref[0:A_WIDTH + B_WIDTH, :], preferred_element_type=F32)
    y = y + jnp.dot(yc_ref[...], w_ref[A_WIDTH + B_WIDTH:, :], preferred_element_type=F32)
    x1 = x_ref[...] + g1_ref[...] * y
    x1_ref[...] = x1
    h2 = _rms_mod(x1, gain_ref[...], sc_ref[...], sh_ref[...])
    h2_ref[:, 0, :] = h2

    logits = _nt_dot(wr_ref[...], h2.astype(BF16))
    scores = jax.nn.sigmoid(logits)
    sel = scores + br_ref[...]
    sel_r = [sel[k:k + 1, :] for k in range(N_EXPERTS)]
    sc_r = [scores[k:k + 1, :] for k in range(N_EXPERTS)]
    grp = []
    for g in range(N_GROUPS):
        a = sel_r[GROUP * g:GROUP * (g + 1)]
        pair = [a[p] + a[q] for p in range(GROUP) for q in range(p + 1, GROUP)]
        grp.append(functools.reduce(jnp.maximum, pair))
    gi, _ = _first_max(grp)
    cand = [_pick(gi, [sel_r[GROUP * g + j] for g in range(N_GROUPS)]) for j in range(GROUP)]
    cand_s = [_pick(gi, [sc_r[GROUP * g + j] for g in range(N_GROUPS)]) for j in range(GROUP)]
    i0, _ = _first_max(cand)
    i1, _ = _first_max([jnp.where(i0 == j, -jnp.inf, cand[j]) for j in range(GROUP)])
    s0, s1 = _pick(i0, cand_s), _pick(i1, cand_s)
    e0, e1 = GROUP * gi + i0, GROUP * gi + i1
    den = s0 + s1

    @pl.when(i == 0)
    def _():
        carry_ref[...] = jnp.zeros((N_EXPERTS, HEAD), F32)

    ek = lax.broadcasted_iota(jnp.int32, (N_EXPERTS, TM), 0)
    oh0 = (ek == e0).astype(F32)
    oh1 = (ek == e1).astype(F32)
    both = oh0 + oh1
    r = lax.broadcasted_iota(jnp.int32, (TM, TM), 0)
    c = lax.broadcasted_iota(jnp.int32, (TM, TM), 1)
    before = (r < c).astype(BF16)
    run = jnp.dot(both.astype(BF16), before, preferred_element_type=F32) + carry_ref[:, 0:1]
    rank0 = jnp.sum(run * oh0, axis=0, keepdims=True)
    rank1 = jnp.sum(run * oh1, axis=0, keepdims=True)
    total = carry_ref[...] + jnp.sum(both, axis=-1, keepdims=True)
    carry_ref[...] = total
    cnt_ref[...] = total

    info_ref[0:1, :] = e0.astype(F32)
    info_ref[1:2, :] = e1.astype(F32)
    info_ref[2:3, :] = s0 / den
    info_ref[3:4, :] = s1 / den
    info_ref[4:5, :] = rank0
    info_ref[5:6, :] = rank1
    info_ref[6:8, :] = jnp.zeros((2, TM), F32)


def _out_proj(yab, yc, x, w_out, g1, gain, sc, sh, wr_t, br):
    row = lambda i: (0, 0)
    return pl.pallas_call(
        _out_kernel,
        out_shape=(
            jax.ShapeDtypeStruct((SEQ, D), F32),
            jax.ShapeDtypeStruct((SEQ, 1, D), F32),
            jax.ShapeDtypeStruct((8, SEQ), F32),
            jax.ShapeDtypeStruct((N_EXPERTS, HEAD), F32),
        ),
        grid=(SEQ // TM,),
        in_specs=[
            pl.BlockSpec((TM, A_WIDTH + B_WIDTH), lambda i: (i, 0)),
            pl.BlockSpec((TM, C_WIDTH), lambda i: (i, 0)),
            pl.BlockSpec((TM, D), lambda i: (i, 0)),
            pl.BlockSpec((D, D), row, pipeline_mode=pl.Buffered(1)),
            pl.BlockSpec((1, D), row), pl.BlockSpec((1, D), row),
            pl.BlockSpec((1, D), row), pl.BlockSpec((1, D), row),
            pl.BlockSpec((N_EXPERTS, D), row),
            pl.BlockSpec((N_EXPERTS, 1), row),
        ],
        out_specs=(
            pl.BlockSpec((TM, D), lambda i: (i, 0)),
            pl.BlockSpec((TM, 1, D), lambda i: (i, 0, 0)),
            pl.BlockSpec((8, TM), lambda i: (0, i)),
            pl.BlockSpec((N_EXPERTS, HEAD), row),
        ),
        scratch_shapes=[pltpu.VMEM((N_EXPERTS, HEAD), F32)],
        compiler_params=pltpu.CompilerParams(
            dimension_semantics=("arbitrary",), vmem_limit_bytes=VMEM_LIMIT),
        name="out_proj",
    )(yab, yc, x, w_out, g1, gain, sc, sh, wr_t, br)


def _moe_kernel(layer, dest_ref, bexp_ref, nvalid_ref, nused_ref, newexp_ref, nextexp_ref,
                h_hbm, wg_hbm, wu_hbm, wd_hbm, zeros_hbm, y_hbm,
                code_ref, xs_ref, ys_ref, xb_ref, hid_ref, wg32_ref, wu32_ref, wd32_ref,
                wg_ref, wu_ref, wd_ref, gsem, ssem, wsem):
    b = pl.program_id(0)
    n_used = nused_ref[0]
    slot = b % 2

    def weight_copies(e):
        return (pltpu.make_async_copy(wg_hbm.at[layer, e], wg32_ref, wsem.at[0]),
                pltpu.make_async_copy(wu_hbm.at[layer, e], wu32_ref, wsem.at[1]),
                pltpu.make_async_copy(wd_hbm.at[layer, e], wd32_ref, wsem.at[2]))

    def switch_expert():
        for cp in weight_copies(bexp_ref[b]):
            cp.wait()

        def cast(k, _):
            r = pl.multiple_of(k * 256, 256)
            wg_ref[pl.ds(r, 256), :] = wg32_ref[pl.ds(r, 256), :].astype(BF16)
            wu_ref[pl.ds(r, 256), :] = wu32_ref[pl.ds(r, 256), :].astype(BF16)
            r = pl.multiple_of(k * 128, 128)
            wd_ref[pl.ds(r, 128), :] = wd32_ref[pl.ds(r, 128), :].astype(BF16)
            return 0
        lax.fori_loop(0, D // 256, cast, 0)

        @pl.when(nextexp_ref[b] >= 0)
        def _():
            for cp in weight_copies(jnp.maximum(nextexp_ref[b], 0)):
                cp.start(priority=1)

    def gather_copy(blk, r, s):
        tok = code_ref[blk * MOE_BLK + r] & (SEQ - 1)
        return pltpu.make_async_copy(h_hbm.at[tok], xs_ref.at[s, pl.ds(r, 1), :],
                                     gsem.at[s])

    def scatter_copy(blk, r, s):
        dst = code_ref[blk * MOE_BLK + r]
        return pltpu.make_async_copy(ys_ref.at[s, pl.ds(r, 1), :], y_hbm.at[dst],
                                     ssem.at[s])

    def for_rows(n, fn):
        def body(r, _):
            fn(r)
            return 0
        lax.fori_loop(0, n, body, 0)

    def scatter_wait(n, s):
        n8 = pl.multiple_of((n >> 3) << 3, 8)

        @pl.when(n8 > 0)
        def _():
            pltpu.make_async_copy(ys_ref.at[s, pl.ds(0, n8), :], ys_ref.at[s, pl.ds(0, n8), :],
                                  ssem.at[s]).wait()

        for r in range(7):
            @pl.when(n8 + r < n)
            def _():
                pltpu.make_async_copy(ys_ref.at[s, pl.ds(r, 1), :], y_hbm.at[r],
                                      ssem.at[s]).wait()

    @pl.when(b == 0)
    def _():
        for cp in weight_copies(bexp_ref[0]):
            cp.start()

        fill = pltpu.make_async_copy(zeros_hbm, code_ref, wsem.at[3])
        fill.start()
        fill.wait()

        def place(a, _):
            code_ref[dest_ref[a]] = a
            return 0
        lax.fori_loop(0, N_ASSIGN, place, 0, unroll=8)
        for_rows(MOE_BLK, lambda r: gather_copy(0, r, 0).start())

    def step(prefetch):
        @pl.when(newexp_ref[b] == 1)
        def _():
            switch_expert()

        for r in range(MOE_BLK):
            gather_copy(b, r, slot).wait()

        xb_ref[...] = xs_ref[slot].astype(BF16)
        prev = jnp.maximum(b - 1, 0)
        n_prev = jnp.where(b >= 1, nvalid_ref[prev], 0)
        x = xb_ref[...]
        rows_per = MOE_BLK // MOE_CHUNKS

        hc = D_EXPERT // MOE_CHUNKS
        for c in range(MOE_CHUNKS):
            cs = slice(c * hc, (c + 1) * hc)
            gate = jnp.dot(x, wg_ref[:, cs], preferred_element_type=F32)
            up = jnp.dot(x, wu_ref[:, cs], preferred_element_type=F32)
            hid = jax.nn.silu(gate) * up
            if prefetch:
                for r in range(c * rows_per, (c + 1) * rows_per):
                    gather_copy(b + 1, r, 1 - slot).start()
                tie = pltpu.bitcast(xs_ref[slot, 0:8, 0:hc], jnp.uint32)
                tie = pltpu.bitcast((tie >> 16) >> 16, F32)
                hid = jnp.concatenate([hid[0:8] + tie, hid[8:]], axis=0)
            hid_ref[:, cs] = hid.astype(BF16)

        scatter_wait(jnp.where(b >= 2, nvalid_ref[jnp.maximum(b - 2, 0)], 0), slot)
        hidb = hid_ref[...]
        oc = D // MOE_CHUNKS
        for c in range(MOE_CHUNKS):
            for r in range(c * rows_per, (c + 1) * rows_per):
                @pl.when(r < n_prev)
                def _():
                    scatter_copy(prev, r, 1 - slot).start(priority=1)
            cs = slice(c * oc, (c + 1) * oc)
            ys_ref[slot, :, cs] = jnp.dot(hidb, wd_ref[:, cs], preferred_element_type=F32)

    @pl.when(b < n_used - 1)
    def _():
        step(True)

    @pl.when(b == n_used - 1)
    def _():
        step(False)
        for_rows(nvalid_ref[b], lambda r: scatter_copy(b, r, slot).start())
        scatter_wait(nvalid_ref[b], slot)
        scatter_wait(jnp.where(b >= 1, nvalid_ref[jnp.maximum(b - 1, 0)], 0), 1 - slot)


def _moe(layer, dest, bexp, nvalid, nused, newexp, nextexp, h2, wg, wu, wd):
    hbm = pl.BlockSpec(memory_space=pl.ANY)
    return pl.pallas_call(
        functools.partial(_moe_kernel, layer),
        out_shape=jax.ShapeDtypeStruct((N_ASSIGN, 1, D), F32),
        grid_spec=pltpu.PrefetchScalarGridSpec(
            num_scalar_prefetch=6,
            grid=(MOE_NB,),
            in_specs=[hbm, hbm, hbm, hbm, hbm],
            out_specs=hbm,
            scratch_shapes=[
                pltpu.SMEM((MOE_ROWS,), jnp.int32),
                pltpu.VMEM((2, MOE_BLK, D), F32),
                pltpu.VMEM((2, MOE_BLK, D), F32),
                pltpu.VMEM((MOE_BLK, D), BF16),
                pltpu.VMEM((MOE_BLK, D_EXPERT), BF16),
                pltpu.VMEM((D, D_EXPERT), F32),
                pltpu.VMEM((D, D_EXPERT), F32),
                pltpu.VMEM((D_EXPERT, D), F32),
                pltpu.VMEM((D, D_EXPERT), BF16),
                pltpu.VMEM((D, D_EXPERT), BF16),
                pltpu.VMEM((D_EXPERT, D), BF16),
                pltpu.SemaphoreType.DMA((2,)),
                pltpu.SemaphoreType.DMA((2,)),
                pltpu.SemaphoreType.DMA((4,)),
            ],
        ),
        compiler_params=pltpu.CompilerParams(
            dimension_semantics=("arbitrary",), vmem_limit_bytes=VMEM_LIMIT),
        name="moe",
    )(dest, bexp, nvalid, nused, newexp, nextexp, h2, wg, wu, wd, jnp.zeros((MOE_ROWS,), jnp.int32))


def _combine_kernel(x1_ref, y0_ref, y1_ref, gw_ref, g2_ref, o_ref):
    gw = gw_ref[...]
    moe = gw[:, 0:1] * y0_ref[:, 0, :] + gw[:, 1:2] * y1_ref[:, 0, :]
    o_ref[...] = x1_ref[...] + g2_ref[...] * moe


def _combine(x1, y, gw, g2):
    nt = SEQ // TM
    return pl.pallas_call(
        _combine_kernel,
        out_shape=jax.ShapeDtypeStruct((SEQ, D), F32),
        grid=(nt,),
        in_specs=[
            pl.BlockSpec((TM, D), lambda i: (i, 0)),
            pl.BlockSpec((TM, 1, D), lambda i: (i, 0, 0)),
            pl.BlockSpec((TM, 1, D), lambda i: (i + nt, 0, 0)),
            pl.BlockSpec((TM, 2), lambda i: (i, 0)),
            pl.BlockSpec((1, D), lambda i: (0, 0)),
        ],
        out_specs=pl.BlockSpec((TM, D), lambda i: (i, 0)),
        compiler_params=pltpu.CompilerParams(
            dimension_semantics=("arbitrary",), vmem_limit_bytes=VMEM_LIMIT),
        name="combine",
    )(x1, y, y, gw, g2)


def _dispatch_plan(info, counts):
    cnt = counts[:, 0].astype(jnp.int32)
    padded = (cnt + MOE_BLK - 1) // MOE_BLK * MOE_BLK
    ends = jnp.cumsum(padded)
    pad_start = ends - padded
    e = info[0:2].astype(jnp.int32)
    rank = info[4:6].astype(jnp.int32)
    onehot = e[:, :, None] == jnp.arange(N_EXPERTS, dtype=jnp.int32)
    dest = jnp.sum(jnp.where(onehot, pad_start, 0), axis=-1) + rank
    blk_start = jnp.arange(MOE_NB, dtype=jnp.int32) * MOE_BLK
    bexp = jnp.sum(blk_start[:, None] >= ends[None, :], axis=-1)
    bexp = jnp.minimum(bexp, N_EXPERTS - 1).astype(jnp.int32)
    in_blk = jnp.arange(N_EXPERTS, dtype=jnp.int32)[None, :] == bexp[:, None]
    left = jnp.sum(jnp.where(in_blk, cnt + pad_start, 0), axis=-1) - blk_start
    nvalid = jnp.clip(left, 0, MOE_BLK).astype(jnp.int32)
    nused = (ends[-1:] // MOE_BLK).astype(jnp.int32)
    used = jnp.arange(MOE_NB, dtype=jnp.int32) < nused[0]
    newexp = jnp.concatenate([jnp.ones((1,), jnp.int32), (bexp[1:] != bexp[:-1]).astype(jnp.int32)])
    later = (bexp[None, :] > bexp[:, None]) & used[None, :]
    nextexp = jnp.min(jnp.where(later, bexp[None, :], N_EXPERTS), axis=1)
    nextexp = jnp.where(nextexp < N_EXPERTS, nextexp, -1).astype(jnp.int32)
    return dest.reshape(N_ASSIGN), bexp, nvalid, nused, newexp, nextexp


def kernel(x, c, w_ada, b_ada, g_mix, g_ffn, w_in, a_ws, a_bs, a_vg, b_w, b_scale,
           c_qg, c_kg, c_bf, w_out, w_router, b_router, e_gate, e_up, e_down):
    xs = x.reshape(SEQ, D)
    mod = _ada(c, w_ada, b_ada)
    wr_t = w_router.T.astype(BF16)
    br = b_router.reshape(N_EXPERTS, 1).astype(F32)
    n_uvp = 2 * A_WIDTH + B_WIDTH
    place = np.zeros((3 * HEAD, C_WIDTH), np.float32)
    for pj in range(3):
        for ph in range(C_HEADS):
            place[pj * HEAD + ph, ph * HEAD + pj] = 1.0
    place = jnp.asarray(place, BF16)
    for l in range(DEPTH):
        sh1, sc1, g1, sh2, sc2, g2 = [mod[l, :, j * D:(j + 1) * D] for j in range(6)]
        gain1 = g_mix[l].reshape(1, D)
        w_uvp = w_in[l, :, :n_uvp].astype(BF16)
        w_qk = w_in[l, :, n_uvp:n_uvp + 2 * C_WIDTH].astype(BF16)
        wf = jnp.pad(w_in[l, :, n_uvp + 3 * C_WIDTH:], ((0, 0), (0, HEAD - C_HEADS))).astype(BF16)
        w_qkf = jnp.concatenate([w_qk, wf], axis=1)
        w_vt = w_in[l, :, n_uvp + 2 * C_WIDTH:n_uvp + 3 * C_WIDTH].T.astype(BF16)
        bf =jnp.pad(c_bf[l], (0, HEAD - C_HEADS)).reshape(1, HEAD)
        bsb = jnp.repeat(a_bs[l].T, HEAD, axis=1)
        yab = _mix_ab(xs, gain1, sc1, sh1, w_uvp, a_ws[l], bsb, a_vg[l].reshape(1, A_WIDTH),
                      b_w[l], b_scale[l].reshape(1, B_WIDTH))
        qa, ka, vt, f2 = _mix_c(xs, gain1, sc1, sh1, w_qkf, w_vt, place, c_qg[l].reshape(1, HEAD),
                            c_kg[l].reshape(1, HEAD), bf)
        yc = _attn(_first_key_block(f2, c_qg[l], c_kg[l]), qa, ka, vt)
        x1, h2, info, counts = _out_proj(yab, yc, xs, w_out[l].astype(BF16), g1,
                                         g_ffn[l].reshape(1, D), sc2, sh2, wr_t, br)
        plan = _dispatch_plan(info, counts)
        y = _moe(l, *plan, h2, e_gate, e_up, e_down)
        xs = _combine(x1, y, info[2:4].T, g2)
    return xs.reshape(1, SEQ, D)
```

```python
import functools

import jax
import jax.numpy as jnp
import numpy as np
from jax import lax
from jax.experimental import pallas as pl
from jax.experimental.pallas import tpu as pltpu

F32 = jnp.float32
BF16 = jnp.bfloat16

D = 2048
SEQ = 8192
DEPTH = 2
CHUNK = 64
A_WIDTH = 512
A_HEADS = 4
A_BLOCK = 128
B_WIDTH = 512
POOL_WINDOWS = (2, 4, 8, 16)
POOL_HALO = 16
C_WIDTH = 1024
C_HEADS = 8
HEAD = 128
N_EXPERTS = 16
N_GROUPS = 4
GROUP = 4
D_EXPERT = 1024
EPS = 1e-6
LOG2E = 1.4426950408889634

TM = 512
TQ = 512
TK = 512
ATTN_HEADS = 4
ATTN_DROP_LOG2 = 64.0
MOE_BLK = 256
MOE_CHUNKS = 4
N_ASSIGN = 2 * SEQ
MOE_ROWS = N_ASSIGN + N_EXPERTS * MOE_BLK
MOE_NB = MOE_ROWS // MOE_BLK

VMEM_LIMIT = 56 * 1024 * 1024


def _nt_dot(a, b):
    return lax.dot_general(a, b, (((1,), (1,)), ((), ())), preferred_element_type=F32)


def _rms_mod(x, gain, scale, shift):
    ms = jnp.mean(x * x, axis=-1, keepdims=True)
    return (x * lax.rsqrt(ms + EPS) * gain) * (1.0 + scale) + shift


def _ada_kernel(c_ref, w_ref, b_ref, o_ref):
    ca = jax.nn.silu(c_ref[...]).astype(BF16)
    r = jnp.dot(ca, w_ref[...].astype(BF16), preferred_element_type=F32)
    o_ref[...] = r[0:1, :] + b_ref[...]


def _ada(c, w_ada, b_ada):
    tn = 1024
    c8 = jnp.broadcast_to(c, (16, D))
    return pl.pallas_call(
        _ada_kernel,
        out_shape=jax.ShapeDtypeStruct((DEPTH, 1, 6 * D), F32),
        grid=(DEPTH, 6 * D // tn),
        in_specs=[
            pl.BlockSpec((16, D), lambda l, j: (0, 0)),
            pl.BlockSpec((None, D, tn), lambda l, j: (l, 0, j)),
            pl.BlockSpec((None, 1, tn), lambda l, j: (l, 0, j)),
        ],
        out_specs=pl.BlockSpec((None, 1, tn), lambda l, j: (l, 0, j)),
        compiler_params=pltpu.CompilerParams(
            dimension_semantics=("arbitrary", "arbitrary"), vmem_limit_bytes=VMEM_LIMIT),
        name="ada",
    )(c8, w_ada, b_ada.reshape(DEPTH, 1, 6 * D))


def _mix_ab_kernel(x_ref, gain_ref, sc_ref, sh_ref, w_ref, ws_ref, bsb_ref, vg_ref, bw_ref,
                   bscale_ref, y_ref, pext_ref):
    i = pl.program_id(0)
    h = _rms_mod(x_ref[...], gain_ref[...], sc_ref[...], sh_ref[...])
    z = jnp.dot(h.astype(BF16), w_ref[...], preferred_element_type=F32)
    u = jax.nn.gelu(z[:, :A_WIDTH])
    v = jax.nn.gelu(z[:, A_WIDTH:2 * A_WIDTH])
    p = z[:, 2 * A_WIDTH:]

    cid_t = lax.broadcasted_iota(jnp.int32, (A_BLOCK, A_BLOCK), 0) // CHUNK
    cid_s = lax.broadcasted_iota(jnp.int32, (A_BLOCK, A_BLOCK), 1) // CHUNK
    mask = cid_s <= cid_t
    for hh in range(A_HEADS):
        cs = slice(hh * HEAD, (hh + 1) * HEAD)
        vh = v[:, cs]
        vn = vh * lax.rsqrt(jnp.mean(vh * vh, axis=-1, keepdims=True) + EPS) * vg_ref[:, cs]
        vnb = vn.astype(BF16)
        wm = jnp.where(mask, ws_ref[hh], 0.0).astype(BF16)
        for n in range(TM // A_BLOCK):
            rs = slice(n * A_BLOCK, (n + 1) * A_BLOCK)
            sp = jnp.dot(wm, vnb[rs], preferred_element_type=F32) + bsb_ref[:, cs]
            y_ref[rs, cs] = (u[rs, cs] * sp).astype(BF16)

    @pl.when(i == 0)
    def _():
        pext_ref[0:POOL_HALO, :] = jnp.zeros((POOL_HALO, B_WIDTH), F32)

    pext_ref[POOL_HALO:POOL_HALO + TM, :] = p
    t1 = i * TM + lax.broadcasted_iota(jnp.int32, (TM, 1), 0) + 1
    for g, w in enumerate(POOL_WINDOWS):
        cs = slice(g * HEAD, (g + 1) * HEAD)
        acc = p[:, cs]
        for j in range(1, w):
            acc = acc + pext_ref[POOL_HALO - j:POOL_HALO - j + TM, cs]
        cnt = jnp.minimum(t1, w).astype(F32)
        d = acc / cnt - p[:, cs]
        yb = jnp.dot(d.astype(BF16), bw_ref[g].astype(BF16), preferred_element_type=F32)
        y_ref[:, A_WIDTH + g * HEAD:A_WIDTH + (g + 1) * HEAD] = (yb * bscale_ref[:, cs]).astype(BF16)
    pext_ref[0:POOL_HALO, :] = pext_ref[TM:TM + POOL_HALO, :]


def _mix_ab(layer, x, gain, sc, sh, w_in_bf, a_ws, bsb, vg, b_w, bscale):
    row = lambda i: (0, 0)
    return pl.pallas_call(
        _mix_ab_kernel,
        out_shape=jax.ShapeDtypeStruct((SEQ, A_WIDTH + B_WIDTH), BF16),
        grid=(SEQ // TM,),
        in_specs=[
            pl.BlockSpec((TM, D), lambda i: (i, 0)),
            pl.BlockSpec((1, D), row), pl.BlockSpec((1, D), row), pl.BlockSpec((1, D), row),
            pl.BlockSpec((None, D, 2 * A_WIDTH + B_WIDTH), lambda i: (layer, 0, 0),
                         pipeline_mode=pl.Buffered(1)),
            pl.BlockSpec((A_HEADS, A_BLOCK, A_BLOCK), lambda i: (0, 0, 0)),
            pl.BlockSpec((A_BLOCK, A_WIDTH), row),
            pl.BlockSpec((1, A_WIDTH), row),
            pl.BlockSpec((len(POOL_WINDOWS), HEAD, HEAD), lambda i: (0, 0, 0)),
            pl.BlockSpec((1, B_WIDTH), row),
        ],
        out_specs=pl.BlockSpec((TM, A_WIDTH + B_WIDTH), lambda i: (i, 0)),
        scratch_shapes=[pltpu.VMEM((TM + POOL_HALO, B_WIDTH), F32)],
        compiler_params=pltpu.CompilerParams(
            dimension_semantics=("arbitrary",), vmem_limit_bytes=VMEM_LIMIT),
        name="mix_ab",
    )(x, gain, sc, sh, w_in_bf, a_ws, bsb, vg, b_w, bscale)


def _split3(x):
    hi = x.astype(BF16)
    rest = x - hi.astype(F32)
    mid = rest.astype(BF16)
    lo = (rest - mid.astype(F32)).astype(BF16)
    return jnp.concatenate([hi, mid, lo], axis=1)


def _mix_c_kernel(x_ref, gain_ref, sc_ref, sh_ref, w1_ref, w2_ref, wf_ref, place_ref, qg_ref, kg_ref,
                  bf_ref, q_ref, k_ref, vt_ref, f2_ref, carry_ref):
    i = pl.program_id(0)
    h = _rms_mod(x_ref[...], gain_ref[...], sc_ref[...], sh_ref[...])
    hb = h.astype(BF16)
    z1 = jnp.dot(hb, w1_ref[...], preferred_element_type=F32)
    half = C_WIDTH // 2
    z2 = jnp.dot(hb, w2_ref[:, :half], preferred_element_type=F32)
    lane = lax.broadcasted_iota(jnp.int32, (TM, HEAD), 1)
    q_aug = jnp.where(lane < 3, -1.0, 0.0).astype(BF16)
    qscale = HEAD ** -0.5 * LOG2E
    for hh in range(C_HEADS):
        qh = z1[:, hh * HEAD:(hh + 1) * HEAD]
        qn = qh * lax.rsqrt(jnp.mean(qh * qh, axis=-1, keepdims=True) + EPS) * qg_ref[...]
        q_ref[:, 2 * hh * HEAD:(2 * hh + 1) * HEAD] = (qn * qscale).astype(BF16)
        q_ref[:, (2 * hh + 1) * HEAD:(2 * hh + 2) * HEAD] = q_aug
        if hh < C_HEADS // 2:
            kh = z1[:, C_WIDTH + hh * HEAD:C_WIDTH + (hh + 1) * HEAD]
        else:
            kh = z2[:, hh * HEAD - half:(hh + 1) * HEAD - half]
        kn = kh * lax.rsqrt(jnp.mean(kh * kh, axis=-1, keepdims=True) + EPS) * kg_ref[...]
        k_ref[:, 2 * hh * HEAD:(2 * hh + 1) * HEAD] = kn.astype(BF16)
    vt = lax.dot_general(w2_ref[:, half:], hb, (((0,), (1,)), ((), ())), preferred_element_type=F32)
    vt_ref[...] = vt.astype(BF16)

    @pl.when(i == 0)
    def _():
        carry_ref[...] = jnp.zeros((1, HEAD), F32)

    fz = jnp.dot(hb, wf_ref[...], preferred_element_type=F32)
    logf = jax.nn.log_sigmoid(fz + bf_ref[...])
    r = lax.broadcasted_iota(jnp.int32, (TM, TM), 0)
    c = lax.broadcasted_iota(jnp.int32, (TM, TM), 1)
    lower = jnp.where(c <= r, 1.0, 0.0).astype(BF16)
    cum = jnp.dot(lower, _split3(logf), preferred_element_type=F32)
    f_cum = (cum[:, :HEAD] + cum[:, HEAD:2 * HEAD]) + cum[:, 2 * HEAD:] + carry_ref[...]
    carry_ref[...] = f_cum[TM - 1:TM, :]
    f2 = f_cum * LOG2E
    f2_ref[...] = f2
    aug = jnp.dot(_split3(f2), place_ref[...], preferred_element_type=F32)
    for hh in range(C_HEADS):
        k_ref[:, (2 * hh + 1) * HEAD:(2 * hh + 2) * HEAD] = aug[:, hh * HEAD:(hh + 1) * HEAD].astype(BF16)


def _mix_c(layer, x, gain, sc, sh, w_in_bf, wf, place, qg, kg, bf):
    row = lambda i: (0, 0)
    wblk = 2 * A_WIDTH + B_WIDTH
    tile = pl.BlockSpec((TM, 2 * C_WIDTH), lambda i: (i, 0))
    return pl.pallas_call(
        _mix_c_kernel,
        out_shape=(
            jax.ShapeDtypeStruct((SEQ, 2 * C_WIDTH), BF16),
            jax.ShapeDtypeStruct((SEQ, 2 * C_WIDTH), BF16),
            jax.ShapeDtypeStruct((C_WIDTH, SEQ), BF16),
            jax.ShapeDtypeStruct((SEQ, HEAD), F32),
        ),
        grid=(SEQ // TM,),
        in_specs=[
            pl.BlockSpec((TM, D), lambda i: (i, 0)),
            pl.BlockSpec((1, D), row), pl.BlockSpec((1, D), row), pl.BlockSpec((1, D), row),
            pl.BlockSpec((None, D, wblk), lambda i: (layer, 0, 1), pipeline_mode=pl.Buffered(1)),
            pl.BlockSpec((None, D, wblk), lambda i: (layer, 0, 2), pipeline_mode=pl.Buffered(1)),
            pl.BlockSpec((D, HEAD), row),
            pl.BlockSpec((3 * HEAD, C_WIDTH), row),
            pl.BlockSpec((1, HEAD), row), pl.BlockSpec((1, HEAD), row),
            pl.BlockSpec((1, HEAD), row),
        ],
        out_specs=(tile, tile, pl.BlockSpec((C_WIDTH, TM), lambda i: (0, i)),
                   pl.BlockSpec((TM, HEAD), lambda i: (i, 0))),
        scratch_shapes=[pltpu.VMEM((1, HEAD), F32)],
        compiler_params=pltpu.CompilerParams(
            dimension_semantics=("arbitrary",), vmem_limit_bytes=VMEM_LIMIT),
        name="mix_c",
    )(x, gain, sc, sh, w_in_bf, w_in_bf, wf, place, qg, kg, bf)


def _attn_kernel(first_ref, q_ref, k_ref, vt_ref, o_ref, st_ref):
    i = pl.program_id(1)
    first = first_ref[pl.program_id(0) * (SEQ // TQ) + i]

    def scores_t(g, j):
        off = pl.multiple_of(j * TK, TK)
        cols = slice(2 * g * HEAD, 2 * (g + 1) * HEAD)
        return _nt_dot(k_ref[pl.ds(off, TK), cols], q_ref[:, cols])

    def update(g, read_st, j, m, l, acc):
        m_new = jnp.maximum(m, jnp.max(read_st(), axis=0, keepdims=True))
        alpha = jnp.exp2(m - m_new)
        p = jnp.exp2(read_st() - m_new)
        l_new = alpha * l + jnp.sum(p, axis=0, keepdims=True)
        off = pl.multiple_of(j * TK, TK)
        pv = jnp.dot(vt_ref[g * HEAD:(g + 1) * HEAD, pl.ds(off, TK)], p.astype(BF16),
                     preferred_element_type=F32)
        return m_new, l_new, alpha * acc + pv

    for g in range(ATTN_HEADS):
        st_ref[g] = scores_t(g, first)

    def body(j, carry):
        out = []
        for g in range(ATTN_HEADS):
            out.append(update(g, lambda g=g: st_ref[g], j, *carry[g]))
            st_ref[g] = scores_t(g, j + 1)
        return tuple(out)

    init = (jnp.full((1, TQ), -jnp.inf, F32), jnp.zeros((1, TQ), F32), jnp.zeros((HEAD, TQ), F32))
    carry = lax.fori_loop(first, i, body, (init,) * ATTN_HEADS)
    kpos = lax.broadcasted_iota(jnp.int32, (TK, TQ), 0)
    qpos = lax.broadcasted_iota(jnp.int32, (TK, TQ), 1)
    for g in range(ATTN_HEADS):
        m, l, acc = update(g, lambda g=g: jnp.where(kpos <= qpos, st_ref[g], -jnp.inf), i, *carry[g])
        o_ref[:, g * HEAD:(g + 1) * HEAD] = (acc / l).T.astype(BF16)


def _first_key_block(f2, qg, kg):
    nq = SEQ // TQ
    bound = LOG2E * HEAD ** 0.5 * jnp.max(jnp.abs(qg)) * jnp.max(jnp.abs(kg)) * 1.02
    f_start = f2[0::TQ, :C_HEADS]
    f_end = f2[TK - 1::TK, :C_HEADS]
    gap = f_end[None, :, :] - f_start[:, None, :]
    needed = gap <= 2.0 * bound + ATTN_DROP_LOG2
    needed = jnp.any(needed.reshape(nq, SEQ // TK, C_HEADS // ATTN_HEADS, ATTN_HEADS), axis=-1)
    first = jnp.argmax(needed, axis=1)
    return first.T.reshape(-1).astype(jnp.int32)


def _attn(first, qa, ka, vt):
    gw = ATTN_HEADS * HEAD
    return pl.pallas_call(
        _attn_kernel,
        out_shape=jax.ShapeDtypeStruct((SEQ, C_WIDTH), BF16),
        grid_spec=pltpu.PrefetchScalarGridSpec(
            num_scalar_prefetch=1,
            grid=(C_HEADS // ATTN_HEADS, SEQ // TQ),
            in_specs=[
                pl.BlockSpec((TQ, 2 * gw), lambda h, i, f: (i, h)),
                pl.BlockSpec((SEQ, 2 * gw), lambda h, i, f: (0, h), pipeline_mode=pl.Buffered(1)),
                pl.BlockSpec((gw, SEQ), lambda h, i, f: (h, 0), pipeline_mode=pl.Buffered(1)),
            ],
            out_specs=pl.BlockSpec((TQ, gw), lambda h, i, f: (i, h)),
            scratch_shapes=[pltpu.VMEM((ATTN_HEADS, TK, TQ), F32)],
        ),
        compiler_params=pltpu.CompilerParams(
            dimension_semantics=("arbitrary", "arbitrary"), vmem_limit_bytes=VMEM_LIMIT),
        name="attn",
    )(first, qa, ka, vt)


def _first_max(vals):
    best, idx = vals[0], jnp.zeros(vals[0].shape, jnp.int32)
    for j in range(1, len(vals)):
        better = vals[j] > best
        idx = jnp.where(better, j, idx)
        best = jnp.where(better, vals[j], best)
    return idx, best


def _pick(idx, vals):
    out = vals[-1]
    for j in range(len(vals) - 2, -1, -1):
        out = jnp.where(idx == j, vals[j], out)
    return out


def _out_kernel(yab_ref, yc_ref, x_ref, w_ref, g1_ref, gain_ref, sc_ref, sh_ref, wr_ref, br_ref,
                x1_ref, h2_ref, info_ref, cnt_ref, carry_ref):
    i = pl.program_id(0)
    y = jnp.dot(yab_ref[...], w_ref[0:A_WIDTH + B_WIDTH, :], preferred_element_type=F32)
    y = y + jnp.dot(yc_ref[...], w_ref[A_WIDTH + B_WIDTH:, :], preferred_element_type=F32)
    x1 = x_ref[...] + g1_ref[...] * y
    x1_ref[...] = x1
    h2 = _rms_mod(x1, gain_ref[...], sc_ref[...], sh_ref[...])
    h2_ref[:, 0, :] = h2

    logits = _nt_dot(wr_ref[...], h2.astype(BF16))
    scores = jax.nn.sigmoid(logits)
    sel = scores + br_ref[...]
    sel_r = [sel[k:k + 1, :] for k in range(N_EXPERTS)]
    sc_r = [scores[k:k + 1, :] for k in range(N_EXPERTS)]
    grp = []
    for g in range(N_GROUPS):
        a = sel_r[GROUP * g:GROUP * (g + 1)]
        pair = [a[p] + a[q] for p in range(GROUP) for q in range(p + 1, GROUP)]
        grp.append(functools.reduce(jnp.maximum, pair))
    gi, _ = _first_max(grp)
    cand = [_pick(gi, [sel_r[GROUP * g + j] for g in range(N_GROUPS)]) for j in range(GROUP)]
    cand_s = [_pick(gi, [sc_r[GROUP * g + j] for g in range(N_GROUPS)]) for j in range(GROUP)]
    i0, _ = _first_max(cand)
    i1, _ = _first_max([jnp.where(i0 == j, -jnp.inf, cand[j]) for j in range(GROUP)])
    s0, s1 = _pick(i0, cand_s), _pick(i1, cand_s)
    e0, e1 = GROUP * gi + i0, GROUP * gi + i1
    den = s0 + s1

    @pl.when(i == 0)
    def _():
        carry_ref[...] = jnp.zeros((N_EXPERTS, HEAD), F32)

    ek = lax.broadcasted_iota(jnp.int32, (N_EXPERTS, TM), 0)
    oh0 = (ek == e0).astype(F32)
    oh1 = (ek == e1).astype(F32)
    both = oh0 + oh1
    r = lax.broadcasted_iota(jnp.int32, (TM, TM), 0)
    c = lax.broadcasted_iota(jnp.int32, (TM, TM), 1)
    before = (r < c).astype(BF16)
    run = jnp.dot(both.astype(BF16), before, preferred_element_type=F32) + carry_ref[:, 0:1]
    rank0 = jnp.sum(run * oh0, axis=0, keepdims=True)
    rank1 = jnp.sum(run * oh1, axis=0, keepdims=True)
    total = carry_ref[...] + jnp.sum(both, axis=-1, keepdims=True)
    carry_ref[...] = total
    cnt_ref[...] = total

    info_ref[0:1, :] = e0.astype(F32)
    info_ref[1:2, :] = e1.astype(F32)
    info_ref[2:3, :] = s0 / den
    info_ref[3:4, :] = s1 / den
    info_ref[4:5, :] = rank0
    info_ref[5:6, :] = rank1
    info_ref[6:8, :] = jnp.zeros((2, TM), F32)


def _out_proj(layer, yab, yc, x, w_out, g1, gain, sc, sh, wr_t, br):
    row = lambda i: (0, 0)
    return pl.pallas_call(
        _out_kernel,
        out_shape=(
            jax.ShapeDtypeStruct((SEQ, D), F32),
            jax.ShapeDtypeStruct((SEQ, 1, D), F32),
            jax.ShapeDtypeStruct((8, SEQ), F32),
            jax.ShapeDtypeStruct((N_EXPERTS, HEAD), F32),
        ),
        grid=(SEQ // TM,),
        in_specs=[
            pl.BlockSpec((TM, A_WIDTH + B_WIDTH), lambda i: (i, 0)),
            pl.BlockSpec((TM, C_WIDTH), lambda i: (i, 0)),
            pl.BlockSpec((TM, D), lambda i: (i, 0)),
            pl.BlockSpec((None, D, D), lambda i: (layer, 0, 0), pipeline_mode=pl.Buffered(1)),
            pl.BlockSpec((1, D), row), pl.BlockSpec((1, D), row),
            pl.BlockSpec((1, D), row), pl.BlockSpec((1, D), row),
            pl.BlockSpec((N_EXPERTS, D), row),
            pl.BlockSpec((N_EXPERTS, 1), row),
        ],
        out_specs=(
            pl.BlockSpec((TM, D), lambda i: (i, 0)),
            pl.BlockSpec((TM, 1, D), lambda i: (i, 0, 0)),
            pl.BlockSpec((8, TM), lambda i: (0, i)),
            pl.BlockSpec((N_EXPERTS, HEAD), row),
        ),
        scratch_shapes=[pltpu.VMEM((N_EXPERTS, HEAD), F32)],
        compiler_params=pltpu.CompilerParams(
            dimension_semantics=("arbitrary",), vmem_limit_bytes=VMEM_LIMIT),
        name="out_proj",
    )(yab, yc, x, w_out, g1, gain, sc, sh, wr_t, br)


def _moe_kernel(layer, dest_ref, bexp_ref, nvalid_ref, nused_ref, newexp_ref, nextexp_ref,
                h_hbm, wg_hbm, wu_hbm, wd_hbm, zeros_hbm, y_hbm,
                code_ref, xs_ref, ys_ref, xb_ref, hid_ref, wg32_ref, wu32_ref, wd32_ref,
                wg_ref, wu_ref, wd_ref, gsem, ssem, wsem):
    b = pl.program_id(0)
    n_used = nused_ref[0]
    slot = b % 2

    def weight_copies(e):
        return (pltpu.make_async_copy(wg_hbm.at[layer, e], wg32_ref, wsem.at[0]),
                pltpu.make_async_copy(wu_hbm.at[layer, e], wu32_ref, wsem.at[1]),
                pltpu.make_async_copy(wd_hbm.at[layer, e], wd32_ref, wsem.at[2]))

    def switch_expert():
        for cp in weight_copies(bexp_ref[b]):
            cp.wait()

        def cast(k, _):
            r = pl.multiple_of(k * 256, 256)
            wg_ref[pl.ds(r, 256), :] = wg32_ref[pl.ds(r, 256), :].astype(BF16)
            wu_ref[pl.ds(r, 256), :] = wu32_ref[pl.ds(r, 256), :].astype(BF16)
            r = pl.multiple_of(k * 128, 128)
            wd_ref[pl.ds(r, 128), :] = wd32_ref[pl.ds(r, 128), :].astype(BF16)
            return 0
        lax.fori_loop(0, D // 256, cast, 0)

        @pl.when(nextexp_ref[b] >= 0)
        def _():
            for cp in weight_copies(jnp.maximum(nextexp_ref[b], 0)):
                cp.start(priority=1)

    def gather_copy(blk, r, s):
        tok = code_ref[blk * MOE_BLK + r] & (SEQ - 1)
        return pltpu.make_async_copy(h_hbm.at[tok], xs_ref.at[s, pl.ds(r, 1), :],
                                     gsem.at[s])

    def scatter_copy(blk, r, s):
        dst = code_ref[blk * MOE_BLK + r]
        return pltpu.make_async_copy(ys_ref.at[s, pl.ds(r, 1), :], y_hbm.at[dst],
                                     ssem.at[s])

    def for_rows(n, fn):
        def body(r, _):
            fn(r)
            return 0
        lax.fori_loop(0, n, body, 0)

    def scatter_wait(n, s):
        n8 = pl.multiple_of((n >> 3) << 3, 8)

        @pl.when(n8 > 0)
        def _():
            pltpu.make_async_copy(ys_ref.at[s, pl.ds(0, n8), :], ys_ref.at[s, pl.ds(0, n8), :],
                                  ssem.at[s]).wait()

        for r in range(7):
            @pl.when(n8 + r < n)
            def _():
                pltpu.make_async_copy(ys_ref.at[s, pl.ds(r, 1), :], y_hbm.at[r],
                                      ssem.at[s]).wait()

    @pl.when(b == 0)
    def _():
        for cp in weight_copies(bexp_ref[0]):
            cp.start()

        fill = pltpu.make_async_copy(zeros_hbm, code_ref, wsem.at[3])
        fill.start()
        fill.wait()

        def place(a, _):
            code_ref[dest_ref[a]] = a
            return 0
        lax.fori_loop(0, N_ASSIGN, place, 0, unroll=8)
        for_rows(MOE_BLK, lambda r: gather_copy(0, r, 0).start())

    def step(prefetch):
        @pl.when(newexp_ref[b] == 1)
        def _():
            switch_expert()

        for r in range(MOE_BLK):
            gather_copy(b, r, slot).wait()

        xb_ref[...] = xs_ref[slot].astype(BF16)
        prev = jnp.maximum(b - 1, 0)
        n_prev = jnp.where(b >= 1, nvalid_ref[prev], 0)
        x = xb_ref[...]
        rows_per = MOE_BLK // MOE_CHUNKS

        hc = D_EXPERT // MOE_CHUNKS
        for c in range(MOE_CHUNKS):
            cs = slice(c * hc, (c + 1) * hc)
            gate = jnp.dot(x, wg_ref[:, cs], preferred_element_type=F32)
            up = jnp.dot(x, wu_ref[:, cs], preferred_element_type=F32)
            hid = jax.nn.silu(gate) * up
            if prefetch:
                for r in range(c * rows_per, (c + 1) * rows_per):
                    gather_copy(b + 1, r, 1 - slot).start()
                tie = pltpu.bitcast(xs_ref[slot, 0:8, 0:hc], jnp.uint32)
                tie = pltpu.bitcast((tie >> 16) >> 16, F32)
                hid = jnp.concatenate([hid[0:8] + tie, hid[8:]], axis=0)
            hid_ref[:, cs] = hid.astype(BF16)

        scatter_wait(jnp.where(b >= 2, nvalid_ref[jnp.maximum(b - 2, 0)], 0), slot)
        hidb = hid_ref[...]
        oc = D // MOE_CHUNKS
        for c in range(MOE_CHUNKS):
            for r in range(c * rows_per, (c + 1) * rows_per):
                @pl.when(r < n_prev)
                def _():
                    scatter_copy(prev, r, 1 - slot).start(priority=1)
            cs = slice(c * oc, (c + 1) * oc)
            ys_ref[slot, :, cs] = jnp.dot(hidb, wd_ref[:, cs], preferred_element_type=F32)

    @pl.when(b < n_used - 1)
    def _():
        step(True)

    @pl.when(b == n_used - 1)
    def _():
        step(False)
        for_rows(nvalid_ref[b], lambda r: scatter_copy(b, r, slot).start())
        scatter_wait(nvalid_ref[b], slot)
        scatter_wait(jnp.where(b >= 1, nvalid_ref[jnp.maximum(b - 1, 0)], 0), 1 - slot)


def _moe(layer, dest, bexp, nvalid, nused, newexp, nextexp, h2, wg, wu, wd):
    hbm = pl.BlockSpec(memory_space=pl.ANY)
    return pl.pallas_call(
        functools.partial(_moe_kernel, layer),
        out_shape=jax.ShapeDtypeStruct((N_ASSIGN, 1, D), F32),
        grid_spec=pltpu.PrefetchScalarGridSpec(
            num_scalar_prefetch=6,
            grid=(MOE_NB,),
            in_specs=[hbm, hbm, hbm, hbm, hbm],
            out_specs=hbm,
            scratch_shapes=[
                pltpu.SMEM((MOE_ROWS,), jnp.int32),
                pltpu.VMEM((2, MOE_BLK, D), F32),
                pltpu.VMEM((2, MOE_BLK, D), F32),
                pltpu.VMEM((MOE_BLK, D), BF16),
                pltpu.VMEM((MOE_BLK, D_EXPERT), BF16),
                pltpu.VMEM((D, D_EXPERT), F32),
                pltpu.VMEM((D, D_EXPERT), F32),
                pltpu.VMEM((D_EXPERT, D), F32),
                pltpu.VMEM((D, D_EXPERT), BF16),
                pltpu.VMEM((D, D_EXPERT), BF16),
                pltpu.VMEM((D_EXPERT, D), BF16),
                pltpu.SemaphoreType.DMA((2,)),
                pltpu.SemaphoreType.DMA((2,)),
                pltpu.SemaphoreType.DMA((4,)),
            ],
        ),
        compiler_params=pltpu.CompilerParams(
            dimension_semantics=("arbitrary",), vmem_limit_bytes=VMEM_LIMIT),
        name="moe",
    )(dest, bexp, nvalid, nused, newexp, nextexp, h2, wg, wu, wd, jnp.zeros((MOE_ROWS,), jnp.int32))


def _combine_kernel(x1_ref, y0_ref, y1_ref, info_ref, g2_ref, o_ref):
    gw = info_ref[...].T
    moe = gw[:, 2:3] * y0_ref[:, 0, :] + gw[:, 3:4] * y1_ref[:, 0, :]
    o_ref[...] = x1_ref[...] + g2_ref[...] * moe


def _combine(x1, y, info, g2):
    nt = SEQ // TM
    return pl.pallas_call(
        _combine_kernel,
        out_shape=jax.ShapeDtypeStruct((SEQ, D), F32),
        grid=(nt,),
        in_specs=[
            pl.BlockSpec((TM, D), lambda i: (i, 0)),
            pl.BlockSpec((TM, 1, D), lambda i: (i, 0, 0)),
            pl.BlockSpec((TM, 1, D), lambda i: (i + nt, 0, 0)),
            pl.BlockSpec((8, TM), lambda i: (0, i)),
            pl.BlockSpec((1, D), lambda i: (0, 0)),
        ],
        out_specs=pl.BlockSpec((TM, D), lambda i: (i, 0)),
        compiler_params=pltpu.CompilerParams(
            dimension_semantics=("arbitrary",), vmem_limit_bytes=VMEM_LIMIT),
        name="combine",
    )(x1, y, y, info, g2)


def _dispatch_plan(info, counts):
    cnt = counts[:, 0].astype(jnp.int32)
    padded = (cnt + MOE_BLK - 1) // MOE_BLK * MOE_BLK
    ends = jnp.cumsum(padded)
    pad_start = ends - padded
    e = info[0:2].astype(jnp.int32)
    rank = info[4:6].astype(jnp.int32)
    onehot = e[:, :, None] == jnp.arange(N_EXPERTS, dtype=jnp.int32)
    dest = jnp.sum(jnp.where(onehot, pad_start, 0), axis=-1) + rank
    blk_start = jnp.arange(MOE_NB, dtype=jnp.int32) * MOE_BLK
    bexp = jnp.sum(blk_start[:, None] >= ends[None, :], axis=-1)
    bexp = jnp.minimum(bexp, N_EXPERTS - 1).astype(jnp.int32)
    in_blk = jnp.arange(N_EXPERTS, dtype=jnp.int32)[None, :] == bexp[:, None]
    left = jnp.sum(jnp.where(in_blk, cnt + pad_start, 0), axis=-1) - blk_start
    nvalid = jnp.clip(left, 0, MOE_BLK).astype(jnp.int32)
    nused = (ends[-1:] // MOE_BLK).astype(jnp.int32)
    used = jnp.arange(MOE_NB, dtype=jnp.int32) < nused[0]
    newexp = jnp.concatenate([jnp.ones((1,), jnp.int32), (bexp[1:] != bexp[:-1]).astype(jnp.int32)])
    later = (bexp[None, :] > bexp[:, None]) & used[None, :]
    nextexp = jnp.min(jnp.where(later, bexp[None, :], N_EXPERTS), axis=1)
    nextexp = jnp.where(nextexp < N_EXPERTS, nextexp, -1).astype(jnp.int32)
    return dest.reshape(N_ASSIGN), bexp, nvalid, nused, newexp, nextexp


def kernel(x, c, w_ada, b_ada, g_mix, g_ffn, w_in, a_ws, a_bs, a_vg, b_w, b_scale,
           c_qg, c_kg, c_bf, w_out, w_router, b_router, e_gate, e_up, e_down):
    xs = x.reshape(SEQ, D)
    mod = _ada(c, w_ada, b_ada)
    wr_t = w_router.T.astype(BF16)
    br = b_router.reshape(N_EXPERTS, 1).astype(F32)
    n_uvp = 2 * A_WIDTH + B_WIDTH
    place = np.zeros((3 * HEAD, C_WIDTH), np.float32)
    for pj in range(3):
        for ph in range(C_HEADS):
            place[pj * HEAD + ph, ph * HEAD + pj] = 1.0
    place = jnp.asarray(place, BF16)
    w_in_bf, w_out_bf = w_in.astype(BF16), w_out.astype(BF16)
    for l in range(DEPTH):
        sh1, sc1, g1, sh2, sc2, g2 = [mod[l, :, j * D:(j + 1) * D] for j in range(6)]
        gain1 = g_mix[l].reshape(1, D)
        wf = jnp.pad(w_in_bf[l, :, n_uvp + 3 * C_WIDTH:], ((0, 0), (0, HEAD - C_HEADS)))
        bf = jnp.pad(c_bf[l], (0, HEAD - C_HEADS)).reshape(1, HEAD)
        bsb = jnp.repeat(a_bs[l].T, HEAD, axis=1)
        yab = _mix_ab(l, xs, gain1, sc1, sh1, w_in_bf, a_ws[l], bsb, a_vg[l].reshape(1, A_WIDTH),
                      b_w[l], b_scale[l].reshape(1, B_WIDTH))
        qa, ka, vt, f2 = _mix_c(l, xs, gain1, sc1, sh1, w_in_bf, wf, place, c_qg[l].reshape(1, HEAD),
                                c_kg[l].reshape(1, HEAD), bf)
        yc = _attn(_first_key_block(f2, c_qg[l], c_kg[l]), qa, ka, vt)
        x1, h2, info, counts = _out_proj(l, yab, yc, xs, w_out_bf, g1,
                                         g_ffn[l].reshape(1, D), sc2, sh2, wr_t, br)
        plan = _dispatch_plan(info, counts)
        y = _moe(l, *plan, h2, e_gate, e_up, e_down)
        xs = _combine(x1, y, info, g2)
    return xs.reshape(1, SEQ, D)
```

```python
import functools

import jax
import jax.numpy as jnp
import numpy as np
from jax import lax
from jax.experimental import pallas as pl
from jax.experimental.pallas import tpu as pltpu

F32 = jnp.float32
BF16 = jnp.bfloat16

D = 2048
SEQ = 8192
DEPTH = 2
CHUNK = 64
A_WIDTH = 512
A_HEADS = 4
A_BLOCK = 128
B_WIDTH = 512
POOL_WINDOWS = (2, 4, 8, 16)
POOL_HALO = 16
C_WIDTH = 1024
C_HEADS = 8
HEAD = 128
N_EXPERTS = 16
N_GROUPS = 4
GROUP = 4
D_EXPERT = 1024
EPS = 1e-6
LOG2E = 1.4426950408889634

TM = 512
TQ = 512
TK = 512
ATTN_HEADS = 4
ATTN_DROP_LOG2 = 64.0
MOE_BLK = 256
MOE_CHUNKS = 4
N_ASSIGN = 2 * SEQ
MOE_ROWS = N_ASSIGN + N_EXPERTS * MOE_BLK
MOE_NB = MOE_ROWS // MOE_BLK

VMEM_LIMIT = 56 * 1024 * 1024


def _nt_dot(a, b):
    return lax.dot_general(a, b, (((1,), (1,)), ((), ())), preferred_element_type=F32)


def _rms_mod(x, gain, scale, shift):
    ms = jnp.mean(x * x, axis=-1, keepdims=True)
    return (x * lax.rsqrt(ms + EPS) * gain) * (1.0 + scale) + shift


def _ada_kernel(c_ref, w_ref, b_ref, o_ref):
    ca = jax.nn.silu(c_ref[...]).astype(BF16)
    r = jnp.dot(ca, w_ref[...].astype(BF16), preferred_element_type=F32)
    o_ref[...] = r[0:1, :] + b_ref[...]


def _ada(c, w_ada, b_ada):
    tn = 1024
    c8 = jnp.broadcast_to(c, (16, D))
    return pl.pallas_call(
        _ada_kernel,
        out_shape=jax.ShapeDtypeStruct((DEPTH, 1, 6 * D), F32),
        grid=(DEPTH, 6 * D // tn),
        in_specs=[
            pl.BlockSpec((16, D), lambda l, j: (0, 0)),
            pl.BlockSpec((None, D, tn), lambda l, j: (l, 0, j)),
            pl.BlockSpec((None, 1, tn), lambda l, j: (l, 0, j)),
        ],
        out_specs=pl.BlockSpec((None, 1, tn), lambda l, j: (l, 0, j)),
        compiler_params=pltpu.CompilerParams(
            dimension_semantics=("arbitrary", "arbitrary"), vmem_limit_bytes=VMEM_LIMIT),
        name="ada",
    )(c8, w_ada, b_ada.reshape(DEPTH, 1, 6 * D))


def _moe_residual(x1_ref, y0_ref, y1_ref, info_ref, g2_ref):
    gw = info_ref[...].T
    moe = gw[:, 2:3] * y0_ref[:, 0, :] + gw[:, 3:4] * y1_ref[:, 0, :]
    return x1_ref[...] + g2_ref[...] * moe


def _mix_ab_kernel(after_moe, *refs):
    if after_moe:
        x1_ref, y0_ref, y1_ref, info_ref, g2_ref = refs[:5]
        (gain_ref, sc_ref, sh_ref, w_ref, ws_ref, bsb_ref, vg_ref, bw_ref, bscale_ref,
         y_ref, x_out_ref, pext_ref) = refs[5:]
        x = _moe_residual(x1_ref, y0_ref, y1_ref, info_ref, g2_ref)
        x_out_ref[...] = x
    else:
        (x_ref, gain_ref, sc_ref, sh_ref, w_ref, ws_ref, bsb_ref, vg_ref, bw_ref, bscale_ref,
         y_ref, pext_ref) = refs
        x = x_ref[...]
    i = pl.program_id(0)
    h = _rms_mod(x, gain_ref[...], sc_ref[...], sh_ref[...])
    z = jnp.dot(h.astype(BF16), w_ref[...], preferred_element_type=F32)
    u = jax.nn.gelu(z[:, :A_WIDTH])
    v = jax.nn.gelu(z[:, A_WIDTH:2 * A_WIDTH])
    p = z[:, 2 * A_WIDTH:]

    cid_t = lax.broadcasted_iota(jnp.int32, (A_BLOCK, A_BLOCK), 0) // CHUNK
    cid_s = lax.broadcasted_iota(jnp.int32, (A_BLOCK, A_BLOCK), 1) // CHUNK
    mask = cid_s <= cid_t
    for hh in range(A_HEADS):
        cs = slice(hh * HEAD, (hh + 1) * HEAD)
        vh = v[:, cs]
        vn = vh * lax.rsqrt(jnp.mean(vh * vh, axis=-1, keepdims=True) + EPS) * vg_ref[:, cs]
        vnb = vn.astype(BF16)
        wm = jnp.where(mask, ws_ref[hh], 0.0).astype(BF16)
        for n in range(TM // A_BLOCK):
            rs = slice(n * A_BLOCK, (n + 1) * A_BLOCK)
            sp = jnp.dot(wm, vnb[rs], preferred_element_type=F32) + bsb_ref[:, cs]
            y_ref[rs, cs] = (u[rs, cs] * sp).astype(BF16)

    @pl.when(i == 0)
    def _():
        pext_ref[0:POOL_HALO, :] = jnp.zeros((POOL_HALO, B_WIDTH), F32)

    pext_ref[POOL_HALO:POOL_HALO + TM, :] = p
    t1 = i * TM + lax.broadcasted_iota(jnp.int32, (TM, 1), 0) + 1
    for g, w in enumerate(POOL_WINDOWS):
        cs = slice(g * HEAD, (g + 1) * HEAD)
        acc = p[:, cs]
        for j in range(1, w):
            acc = acc + pext_ref[POOL_HALO - j:POOL_HALO - j + TM, cs]
        cnt = jnp.minimum(t1, w).astype(F32)
        d = acc / cnt - p[:, cs]
        yb = jnp.dot(d.astype(BF16), bw_ref[g].astype(BF16), preferred_element_type=F32)
        y_ref[:, A_WIDTH + g * HEAD:A_WIDTH + (g + 1) * HEAD] = (yb * bscale_ref[:, cs]).astype(BF16)
    pext_ref[0:POOL_HALO, :] = pext_ref[TM:TM + POOL_HALO, :]


def _mix_ab(layer, x, gain, sc, sh, w_in_bf, a_ws, bsb, vg, b_w, bscale, moe=None):
    row = lambda i: (0, 0)
    nt = SEQ // TM
    tile = pl.BlockSpec((TM, D), lambda i: (i, 0))
    y_spec = pl.BlockSpec((TM, A_WIDTH + B_WIDTH), lambda i: (i, 0))
    y_shape = jax.ShapeDtypeStruct((SEQ, A_WIDTH + B_WIDTH), BF16)
    if moe is None:
        lead_specs, lead_args = [tile], (x,)
        out_shape, out_specs = y_shape, y_spec
    else:
        y, info, g2 = moe
        lead_specs = [tile,
                      pl.BlockSpec((TM, 1, D), lambda i: (i, 0, 0)),
                      pl.BlockSpec((TM, 1, D), lambda i: (i + nt, 0, 0)),
                      pl.BlockSpec((8, TM), lambda i: (0, i)),
                      pl.BlockSpec((1, D), row)]
        lead_args = (x, y, y, info, g2)
        out_shape, out_specs = (y_shape, jax.ShapeDtypeStruct((SEQ, D), F32)), (y_spec, tile)
    return pl.pallas_call(
        functools.partial(_mix_ab_kernel, moe is not None),
        out_shape=out_shape,
        grid=(nt,),
        in_specs=lead_specs + [
            pl.BlockSpec((1, D), row), pl.BlockSpec((1, D), row), pl.BlockSpec((1, D), row),
            pl.BlockSpec((None, D, 2 * A_WIDTH + B_WIDTH), lambda i: (layer, 0, 0),
                         pipeline_mode=pl.Buffered(1)),
            pl.BlockSpec((A_HEADS, A_BLOCK, A_BLOCK), lambda i: (0, 0, 0)),
            pl.BlockSpec((A_BLOCK, A_WIDTH), row),
            pl.BlockSpec((1, A_WIDTH), row),
            pl.BlockSpec((len(POOL_WINDOWS), HEAD, HEAD), lambda i: (0, 0, 0)),
            pl.BlockSpec((1, B_WIDTH), row),
        ],
        out_specs=out_specs,
        scratch_shapes=[pltpu.VMEM((TM + POOL_HALO, B_WIDTH), F32)],
        compiler_params=pltpu.CompilerParams(
            dimension_semantics=("arbitrary",), vmem_limit_bytes=VMEM_LIMIT),
        name="mix_ab",
    )(*lead_args, gain, sc, sh, w_in_bf, a_ws, bsb, vg, b_w, bscale)


def _split3(x):
    hi = x.astype(BF16)
    rest = x - hi.astype(F32)
    mid = rest.astype(BF16)
    lo = (rest - mid.astype(F32)).astype(BF16)
    return jnp.concatenate([hi, mid, lo], axis=1)


def _mix_c_kernel(x_ref, gain_ref, sc_ref, sh_ref, w1_ref, w2_ref, wf_ref, place_ref, qg_ref, kg_ref,
                  bf_ref, q_ref, k_ref, vt_ref, f2_ref, carry_ref):
    i = pl.program_id(0)
    h = _rms_mod(x_ref[...], gain_ref[...], sc_ref[...], sh_ref[...])
    hb = h.astype(BF16)
    z1 = jnp.dot(hb, w1_ref[...], preferred_element_type=F32)
    half = C_WIDTH // 2
    z2 = jnp.dot(hb, w2_ref[:, :half], preferred_element_type=F32)
    lane = lax.broadcasted_iota(jnp.int32, (TM, HEAD), 1)
    q_aug = jnp.where(lane < 3, -1.0, 0.0).astype(BF16)
    qscale = HEAD ** -0.5 * LOG2E
    for hh in range(C_HEADS):
        qh = z1[:, hh * HEAD:(hh + 1) * HEAD]
        qn = qh * lax.rsqrt(jnp.mean(qh * qh, axis=-1, keepdims=True) + EPS) * qg_ref[...]
        q_ref[:, 2 * hh * HEAD:(2 * hh + 1) * HEAD] = (qn * qscale).astype(BF16)
        q_ref[:, (2 * hh + 1) * HEAD:(2 * hh + 2) * HEAD] = q_aug
        if hh < C_HEADS // 2:
            kh = z1[:, C_WIDTH + hh * HEAD:C_WIDTH + (hh + 1) * HEAD]
        else:
            kh = z2[:, hh * HEAD - half:(hh + 1) * HEAD - half]
        kn = kh * lax.rsqrt(jnp.mean(kh * kh, axis=-1, keepdims=True) + EPS) * kg_ref[...]
        k_ref[:, 2 * hh * HEAD:(2 * hh + 1) * HEAD] = kn.astype(BF16)
    vt = lax.dot_general(w2_ref[:, half:], hb, (((0,), (1,)), ((), ())), preferred_element_type=F32)
    vt_ref[...] = vt.astype(BF16)

    @pl.when(i == 0)
    def _():
        carry_ref[...] = jnp.zeros((1, HEAD), F32)

    fz = jnp.dot(hb, wf_ref[...], preferred_element_type=F32)
    logf = jax.nn.log_sigmoid(fz + bf_ref[...])
    r = lax.broadcasted_iota(jnp.int32, (TM, TM), 0)
    c = lax.broadcasted_iota(jnp.int32, (TM, TM), 1)
    lower = jnp.where(c <= r, 1.0, 0.0).astype(BF16)
    cum = jnp.dot(lower, _split3(logf), preferred_element_type=F32)
    f_cum = (cum[:, :HEAD] + cum[:, HEAD:2 * HEAD]) + cum[:, 2 * HEAD:] + carry_ref[...]
    carry_ref[...] = f_cum[TM - 1:TM, :]
    f2 = f_cum * LOG2E
    f2_ref[...] = f2
    aug = jnp.dot(_split3(f2), place_ref[...], preferred_element_type=F32)
    for hh in range(C_HEADS):
        k_ref[:, (2 * hh + 1) * HEAD:(2 * hh + 2) * HEAD] = aug[:, hh * HEAD:(hh + 1) * HEAD].astype(BF16)


def _mix_c(layer, x, gain, sc, sh, w_in_bf, wf, place, qg, kg, bf):
    row = lambda i: (0, 0)
    wblk = 2 * A_WIDTH + B_WIDTH
    tile = pl.BlockSpec((TM, 2 * C_WIDTH), lambda i: (i, 0))
    return pl.pallas_call(
        _mix_c_kernel,
        out_shape=(
            jax.ShapeDtypeStruct((SEQ, 2 * C_WIDTH), BF16),
            jax.ShapeDtypeStruct((SEQ, 2 * C_WIDTH), BF16),
            jax.ShapeDtypeStruct((C_WIDTH, SEQ), BF16),
            jax.ShapeDtypeStruct((SEQ, HEAD), F32),
        ),
        grid=(SEQ // TM,),
        in_specs=[
            pl.BlockSpec((TM, D), lambda i: (i, 0)),
            pl.BlockSpec((1, D), row), pl.BlockSpec((1, D), row), pl.BlockSpec((1, D), row),
            pl.BlockSpec((None, D, wblk), lambda i: (layer, 0, 1), pipeline_mode=pl.Buffered(1)),
            pl.BlockSpec((None, D, wblk), lambda i: (layer, 0, 2), pipeline_mode=pl.Buffered(1)),
            pl.BlockSpec((D, HEAD), row),
            pl.BlockSpec((3 * HEAD, C_WIDTH), row),
            pl.BlockSpec((1, HEAD), row), pl.BlockSpec((1, HEAD), row),
            pl.BlockSpec((1, HEAD), row),
        ],
        out_specs=(tile, tile, pl.BlockSpec((C_WIDTH, TM), lambda i: (0, i)),
                   pl.BlockSpec((TM, HEAD), lambda i: (i, 0))),
        scratch_shapes=[pltpu.VMEM((1, HEAD), F32)],
        compiler_params=pltpu.CompilerParams(
            dimension_semantics=("arbitrary",), vmem_limit_bytes=VMEM_LIMIT),
        name="mix_c",
    )(x, gain, sc, sh, w_in_bf, w_in_bf, wf, place, qg, kg, bf)


def _attn_kernel(first_ref, q_ref, k_ref, vt_ref, o_ref, st_ref):
    i = pl.program_id(1)
    first = first_ref[pl.program_id(0) * (SEQ // TQ) + i]

    def scores_t(g, j):
        off = pl.multiple_of(j * TK, TK)
        cols = slice(2 * g * HEAD, 2 * (g + 1) * HEAD)
        return _nt_dot(k_ref[pl.ds(off, TK), cols], q_ref[:, cols])

    def update(g, read_st, j, m, l, acc):
        m_new = jnp.maximum(m, jnp.max(read_st(), axis=0, keepdims=True))
        alpha = jnp.exp2(m - m_new)
        p = jnp.exp2(read_st() - m_new)
        l_new = alpha * l + jnp.sum(p, axis=0, keepdims=True)
        off = pl.multiple_of(j * TK, TK)
        pv = jnp.dot(vt_ref[g * HEAD:(g + 1) * HEAD, pl.ds(off, TK)], p.astype(BF16),
                     preferred_element_type=F32)
        return m_new, l_new, alpha * acc + pv

    for g in range(ATTN_HEADS):
        st_ref[g] = scores_t(g, first)

    def body(j, carry):
        out = []
        for g in range(ATTN_HEADS):
            out.append(update(g, lambda g=g: st_ref[g], j, *carry[g]))
            st_ref[g] = scores_t(g, j + 1)
        return tuple(out)

    init = (jnp.full((1, TQ), -jnp.inf, F32), jnp.zeros((1, TQ), F32), jnp.zeros((HEAD, TQ), F32))
    carry = lax.fori_loop(first, i, body, (init,) * ATTN_HEADS)
    kpos = lax.broadcasted_iota(jnp.int32, (TK, TQ), 0)
    qpos = lax.broadcasted_iota(jnp.int32, (TK, TQ), 1)
    for g in range(ATTN_HEADS):
        m, l, acc = update(g, lambda g=g: jnp.where(kpos <= qpos, st_ref[g], -jnp.inf), i, *carry[g])
        o_ref[:, g * HEAD:(g + 1) * HEAD] = (acc / l).T.astype(BF16)


def _first_key_block(f2, qg, kg):
    nq = SEQ // TQ
    bound = LOG2E * HEAD ** 0.5 * jnp.max(jnp.abs(qg)) * jnp.max(jnp.abs(kg)) * 1.02
    f_start = f2[0::TQ, :C_HEADS]
    f_end = f2[TK - 1::TK, :C_HEADS]
    gap = f_end[None, :, :] - f_start[:, None, :]
    needed = gap <= 2.0 * bound + ATTN_DROP_LOG2
    needed = jnp.any(needed.reshape(nq, SEQ // TK, C_HEADS // ATTN_HEADS, ATTN_HEADS), axis=-1)
    first = jnp.argmax(needed, axis=1)
    return first.T.reshape(-1).astype(jnp.int32)


def _attn(first, qa, ka, vt):
    gw = ATTN_HEADS * HEAD
    return pl.pallas_call(
        _attn_kernel,
        out_shape=jax.ShapeDtypeStruct((SEQ, C_WIDTH), BF16),
        grid_spec=pltpu.PrefetchScalarGridSpec(
            num_scalar_prefetch=1,
            grid=(C_HEADS // ATTN_HEADS, SEQ // TQ),
            in_specs=[
                pl.BlockSpec((TQ, 2 * gw), lambda h, i, f: (i, h)),
                pl.BlockSpec((SEQ, 2 * gw), lambda h, i, f: (0, h), pipeline_mode=pl.Buffered(1)),
                pl.BlockSpec((gw, SEQ), lambda h, i, f: (h, 0), pipeline_mode=pl.Buffered(1)),
            ],
            out_specs=pl.BlockSpec((TQ, gw), lambda h, i, f: (i, h)),
            scratch_shapes=[pltpu.VMEM((ATTN_HEADS, TK, TQ), F32)],
        ),
        compiler_params=pltpu.CompilerParams(
            dimension_semantics=("arbitrary", "arbitrary"), vmem_limit_bytes=VMEM_LIMIT),
        name="attn",
    )(first, qa, ka, vt)


def _first_max(vals):
    best, idx = vals[0], jnp.zeros(vals[0].shape, jnp.int32)
    for j in range(1, len(vals)):
        better = vals[j] > best
        idx = jnp.where(better, j, idx)
        best = jnp.where(better, vals[j], best)
    return idx, best


def _pick(idx, vals):
    out = vals[-1]
    for j in range(len(vals) - 2, -1, -1):
        out = jnp.where(idx == j, vals[j], out)
    return out


def _out_kernel(yab_ref, yc_ref, x_ref, w_ref, g1_ref, gain_ref, sc_ref, sh_ref, wr_ref, br_ref,
                x1_ref, h2_ref, info_ref, cnt_ref, carry_ref):
    i = pl.program_id(0)
    y = jnp.dot(yab_ref[...], w_ref[0:A_WIDTH + B_WIDTH, :], preferred_element_type=F32)
    y = y + jnp.dot(yc_ref[...], w_ref[A_WIDTH + B_WIDTH:, :], preferred_element_type=F32)
    x1 = x_ref[...] + g1_ref[...] * y
    x1_ref[...] = x1
    h2 = _rms_mod(x1, gain_ref[...], sc_ref[...], sh_ref[...])
    h2_ref[:, 0, :] = h2

    logits = _nt_dot(wr_ref[...], h2.astype(BF16))
    scores = jax.nn.sigmoid(logits)
    sel = scores + br_ref[...]
    sel_r = [sel[k:k + 1, :] for k in range(N_EXPERTS)]
    sc_r = [scores[k:k + 1, :] for k in range(N_EXPERTS)]
    grp = []
    for g in range(N_GROUPS):
        a = sel_r[GROUP * g:GROUP * (g + 1)]
        pair = [a[p] + a[q] for p in range(GROUP) for q in range(p + 1, GROUP)]
        grp.append(functools.reduce(jnp.maximum, pair))
    gi, _ = _first_max(grp)
    cand = [_pick(gi, [sel_r[GROUP * g + j] for g in range(N_GROUPS)]) for j in range(GROUP)]
    cand_s = [_pick(gi, [sc_r[GROUP * g + j] for g in range(N_GROUPS)]) for j in range(GROUP)]
    i0, _ = _first_max(cand)
    i1, _ = _first_max([jnp.where(i0 == j, -jnp.inf, cand[j]) for j in range(GROUP)])
    s0, s1 = _pick(i0, cand_s), _pick(i1, cand_s)
    e0, e1 = GROUP * gi + i0, GROUP * gi + i1
    den = s0 + s1

    @pl.when(i == 0)
    def _():
        carry_ref[...] = jnp.zeros((N_EXPERTS, HEAD), F32)

    ek = lax.broadcasted_iota(jnp.int32, (N_EXPERTS, TM), 0)
    oh0 = (ek == e0).astype(F32)
    oh1 = (ek == e1).astype(F32)
    both = oh0 + oh1
    r = lax.broadcasted_iota(jnp.int32, (TM, TM), 0)
    c = lax.broadcasted_iota(jnp.int32, (TM, TM), 1)
    before = (r < c).astype(BF16)
    run = jnp.dot(both.astype(BF16), before, preferred_element_type=F32) + carry_ref[:, 0:1]
    rank0 = jnp.sum(run * oh0, axis=0, keepdims=True)
    rank1 = jnp.sum(run * oh1, axis=0, keepdims=True)
    total = carry_ref[...] + jnp.sum(both, axis=-1, keepdims=True)
    carry_ref[...] = total
    cnt_ref[...] = total

    info_ref[0:1, :] = e0.astype(F32)
    info_ref[1:2, :] = e1.astype(F32)
    info_ref[2:3, :] = s0 / den
    info_ref[3:4, :] = s1 / den
    info_ref[4:5, :] = rank0
    info_ref[5:6, :] = rank1
    info_ref[6:8, :] = jnp.zeros((2, TM), F32)


def _out_proj(layer, yab, yc, x, w_out, g1, gain, sc, sh, wr_t, br):
    row = lambda i: (0, 0)
    return pl.pallas_call(
        _out_kernel,
        out_shape=(
            jax.ShapeDtypeStruct((SEQ, D), F32),
            jax.ShapeDtypeStruct((SEQ, 1, D), F32),
            jax.ShapeDtypeStruct((8, SEQ), F32),
            jax.ShapeDtypeStruct((N_EXPERTS, HEAD), F32),
        ),
        grid=(SEQ // TM,),
        in_specs=[
            pl.BlockSpec((TM, A_WIDTH + B_WIDTH), lambda i: (i, 0)),
            pl.BlockSpec((TM, C_WIDTH), lambda i: (i, 0)),
            pl.BlockSpec((TM, D), lambda i: (i, 0)),
            pl.BlockSpec((None, D, D), lambda i: (layer, 0, 0), pipeline_mode=pl.Buffered(1)),
            pl.BlockSpec((1, D), row), pl.BlockSpec((1, D), row),
            pl.BlockSpec((1, D), row), pl.BlockSpec((1, D), row),
            pl.BlockSpec((N_EXPERTS, D), row),
            pl.BlockSpec((N_EXPERTS, 1), row),
        ],
        out_specs=(
            pl.BlockSpec((TM, D), lambda i: (i, 0)),
            pl.BlockSpec((TM, 1, D), lambda i: (i, 0, 0)),
            pl.BlockSpec((8, TM), lambda i: (0, i)),
            pl.BlockSpec((N_EXPERTS, HEAD), row),
        ),
        scratch_shapes=[pltpu.VMEM((N_EXPERTS, HEAD), F32)],
        compiler_params=pltpu.CompilerParams(
            dimension_semantics=("arbitrary",), vmem_limit_bytes=VMEM_LIMIT),
        name="out_proj",
    )(yab, yc, x, w_out, g1, gain, sc, sh, wr_t, br)


def _moe_kernel(layer, dest_ref, bexp_ref, nvalid_ref, nused_ref, newexp_ref, nextexp_ref,
                h_hbm, wg_hbm, wu_hbm, wd_hbm, zeros_hbm, y_hbm,
                code_ref, xs_ref, ys_ref, xb_ref, hid_ref, wg32_ref, wu32_ref, wd32_ref,
                wg_ref, wu_ref, wd_ref, gsem, ssem, wsem):
    b = pl.program_id(0)
    n_used = nused_ref[0]
    slot = b % 2

    def weight_copies(e):
        return (pltpu.make_async_copy(wg_hbm.at[layer, e], wg32_ref, wsem.at[0]),
                pltpu.make_async_copy(wu_hbm.at[layer, e], wu32_ref, wsem.at[1]),
                pltpu.make_async_copy(wd_hbm.at[layer, e], wd32_ref, wsem.at[2]))

    def switch_expert():
        for cp in weight_copies(bexp_ref[b]):
            cp.wait()

        def cast(k, _):
            r = pl.multiple_of(k * 256, 256)
            wg_ref[pl.ds(r, 256), :] = wg32_ref[pl.ds(r, 256), :].astype(BF16)
            wu_ref[pl.ds(r, 256), :] = wu32_ref[pl.ds(r, 256), :].astype(BF16)
            r = pl.multiple_of(k * 128, 128)
            wd_ref[pl.ds(r, 128), :] = wd32_ref[pl.ds(r, 128), :].astype(BF16)
            return 0
        lax.fori_loop(0, D // 256, cast, 0)

        @pl.when(nextexp_ref[b] >= 0)
        def _():
            for cp in weight_copies(jnp.maximum(nextexp_ref[b], 0)):
                cp.start(priority=1)

    def gather_copy(blk, r, s):
        tok = code_ref[blk * MOE_BLK + r] & (SEQ - 1)
        return pltpu.make_async_copy(h_hbm.at[tok], xs_ref.at[s, pl.ds(r, 1), :],
                                     gsem.at[s])

    def scatter_copy(blk, r, s):
        dst = code_ref[blk * MOE_BLK + r]
        return pltpu.make_async_copy(ys_ref.at[s, pl.ds(r, 1), :], y_hbm.at[dst],
                                     ssem.at[s])

    def for_rows(n, fn):
        def body(r, _):
            fn(r)
            return 0
        lax.fori_loop(0, n, body, 0)

    def scatter_wait(n, s):
        n8 = pl.multiple_of((n >> 3) << 3, 8)

        @pl.when(n8 > 0)
        def _():
            pltpu.make_async_copy(ys_ref.at[s, pl.ds(0, n8), :], ys_ref.at[s, pl.ds(0, n8), :],
                                  ssem.at[s]).wait()

        for r in range(7):
            @pl.when(n8 + r < n)
            def _():
                pltpu.make_async_copy(ys_ref.at[s, pl.ds(r, 1), :], y_hbm.at[r],
                                      ssem.at[s]).wait()

    @pl.when(b == 0)
    def _():
        for cp in weight_copies(bexp_ref[0]):
            cp.start()

        fill = pltpu.make_async_copy(zeros_hbm, code_ref, wsem.at[3])
        fill.start()
        fill.wait()

        def place(a, _):
            code_ref[dest_ref[a]] = a
            return 0
        lax.fori_loop(0, N_ASSIGN, place, 0, unroll=8)
        for_rows(MOE_BLK, lambda r: gather_copy(0, r, 0).start())

    def step(prefetch):
        @pl.when(newexp_ref[b] == 1)
        def _():
            switch_expert()

        for r in range(MOE_BLK):
            gather_copy(b, r, slot).wait()

        xb_ref[...] = xs_ref[slot].astype(BF16)
        prev = jnp.maximum(b - 1, 0)
        n_prev = jnp.where(b >= 1, nvalid_ref[prev], 0)
        x = xb_ref[...]
        rows_per = MOE_BLK // MOE_CHUNKS

        hc = D_EXPERT // MOE_CHUNKS
        for c in range(MOE_CHUNKS):
            cs = slice(c * hc, (c + 1) * hc)
            gate = jnp.dot(x, wg_ref[:, cs], preferred_element_type=F32)
            up = jnp.dot(x, wu_ref[:, cs], preferred_element_type=F32)
            hid = jax.nn.silu(gate) * up
            if prefetch:
                for r in range(c * rows_per, (c + 1) * rows_per):
                    gather_copy(b + 1, r, 1 - slot).start()
                tie = pltpu.bitcast(xs_ref[slot, 0:8, 0:hc], jnp.uint32)
                tie = pltpu.bitcast((tie >> 16) >> 16, F32)
                hid = jnp.concatenate([hid[0:8] + tie, hid[8:]], axis=0)
            hid_ref[:, cs] = hid.astype(BF16)

        scatter_wait(jnp.where(b >= 2, nvalid_ref[jnp.maximum(b - 2, 0)], 0), slot)
        hidb = hid_ref[...]
        oc = D // MOE_CHUNKS
        for c in range(MOE_CHUNKS):
            for r in range(c * rows_per, (c + 1) * rows_per):
                @pl.when(r < n_prev)
                def _():
                    scatter_copy(prev, r, 1 - slot).start(priority=1)
            cs = slice(c * oc, (c + 1) * oc)
            ys_ref[slot, :, cs] = jnp.dot(hidb, wd_ref[:, cs], preferred_element_type=F32)

    @pl.when(b < n_used - 1)
    def _():
        step(True)

    @pl.when(b == n_used - 1)
    def _():
        step(False)
        for_rows(nvalid_ref[b], lambda r: scatter_copy(b, r, slot).start())
        scatter_wait(nvalid_ref[b], slot)
        scatter_wait(jnp.where(b >= 1, nvalid_ref[jnp.maximum(b - 1, 0)], 0), 1 - slot)


def _moe(layer, dest, bexp, nvalid, nused, newexp, nextexp, h2, wg, wu, wd):
    hbm = pl.BlockSpec(memory_space=pl.ANY)
    return pl.pallas_call(
        functools.partial(_moe_kernel, layer),
        out_shape=jax.ShapeDtypeStruct((N_ASSIGN, 1, D), F32),
        grid_spec=pltpu.PrefetchScalarGridSpec(
            num_scalar_prefetch=6,
            grid=(MOE_NB,),
            in_specs=[hbm, hbm, hbm, hbm, hbm],
            out_specs=hbm,
            scratch_shapes=[
                pltpu.SMEM((MOE_ROWS,), jnp.int32),
                pltpu.VMEM((2, MOE_BLK, D), F32),
                pltpu.VMEM((2, MOE_BLK, D), F32),
                pltpu.VMEM((MOE_BLK, D), BF16),
                pltpu.VMEM((MOE_BLK, D_EXPERT), BF16),
                pltpu.VMEM((D, D_EXPERT), F32),
                pltpu.VMEM((D, D_EXPERT), F32),
                pltpu.VMEM((D_EXPERT, D), F32),
                pltpu.VMEM((D, D_EXPERT), BF16),
                pltpu.VMEM((D, D_EXPERT), BF16),
                pltpu.VMEM((D_EXPERT, D), BF16),
                pltpu.SemaphoreType.DMA((2,)),
                pltpu.SemaphoreType.DMA((2,)),
                pltpu.SemaphoreType.DMA((4,)),
            ],
        ),
        compiler_params=pltpu.CompilerParams(
            dimension_semantics=("arbitrary",), vmem_limit_bytes=VMEM_LIMIT),
        name="moe",
    )(dest, bexp, nvalid, nused, newexp, nextexp, h2, wg, wu, wd, jnp.zeros((MOE_ROWS,), jnp.int32))


def _combine_kernel(x1_ref, y0_ref, y1_ref, info_ref, g2_ref, o_ref):
    o_ref[...] = _moe_residual(x1_ref, y0_ref, y1_ref, info_ref, g2_ref)


def _combine(x1, y, info, g2):
    nt = SEQ // TM
    return pl.pallas_call(
        _combine_kernel,
        out_shape=jax.ShapeDtypeStruct((SEQ, D), F32),
        grid=(nt,),
        in_specs=[
            pl.BlockSpec((TM, D), lambda i: (i, 0)),
            pl.BlockSpec((TM, 1, D), lambda i: (i, 0, 0)),
            pl.BlockSpec((TM, 1, D), lambda i: (i + nt, 0, 0)),
            pl.BlockSpec((8, TM), lambda i: (0, i)),
            pl.BlockSpec((1, D), lambda i: (0, 0)),
        ],
        out_specs=pl.BlockSpec((TM, D), lambda i: (i, 0)),
        compiler_params=pltpu.CompilerParams(
            dimension_semantics=("arbitrary",), vmem_limit_bytes=VMEM_LIMIT),
        name="combine",
    )(x1, y, y, info, g2)


def _dispatch_plan(info, counts):
    cnt = counts[:, 0].astype(jnp.int32)
    padded = (cnt + MOE_BLK - 1) // MOE_BLK * MOE_BLK
    ends = jnp.cumsum(padded)
    pad_start = ends - padded
    e = info[0:2].astype(jnp.int32)
    rank = info[4:6].astype(jnp.int32)
    onehot = e[:, :, None] == jnp.arange(N_EXPERTS, dtype=jnp.int32)
    dest = jnp.sum(jnp.where(onehot, pad_start, 0), axis=-1) + rank
    blk_start = jnp.arange(MOE_NB, dtype=jnp.int32) * MOE_BLK
    bexp = jnp.sum(blk_start[:, None] >= ends[None, :], axis=-1)
    bexp = jnp.minimum(bexp, N_EXPERTS - 1).astype(jnp.int32)
    in_blk = jnp.arange(N_EXPERTS, dtype=jnp.int32)[None, :] == bexp[:, None]
    left = jnp.sum(jnp.where(in_blk, cnt + pad_start, 0), axis=-1) - blk_start
    nvalid = jnp.clip(left, 0, MOE_BLK).astype(jnp.int32)
    nused = (ends[-1:] // MOE_BLK).astype(jnp.int32)
    used = jnp.arange(MOE_NB, dtype=jnp.int32) < nused[0]
    newexp = jnp.concatenate([jnp.ones((1,), jnp.int32), (bexp[1:] != bexp[:-1]).astype(jnp.int32)])
    later = (bexp[None, :] > bexp[:, None]) & used[None, :]
    nextexp = jnp.min(jnp.where(later, bexp[None, :], N_EXPERTS), axis=1)
    nextexp = jnp.where(nextexp < N_EXPERTS, nextexp, -1).astype(jnp.int32)
    return dest.reshape(N_ASSIGN), bexp, nvalid, nused, newexp, nextexp


def kernel(x, c, w_ada, b_ada, g_mix, g_ffn, w_in, a_ws, a_bs, a_vg, b_w, b_scale,
           c_qg, c_kg, c_bf, w_out, w_router, b_router, e_gate, e_up, e_down):
    xs = x.reshape(SEQ, D)
    mod = _ada(c, w_ada, b_ada)
    wr_t = w_router.T.astype(BF16)
    br = b_router.reshape(N_EXPERTS, 1).astype(F32)
    n_uvp = 2 * A_WIDTH + B_WIDTH
    place = np.zeros((3 * HEAD, C_WIDTH), np.float32)
    for pj in range(3):
        for ph in range(C_HEADS):
            place[pj * HEAD + ph, ph * HEAD + pj] = 1.0
    place = jnp.asarray(place, BF16)
    w_in_bf, w_out_bf = w_in.astype(BF16), w_out.astype(BF16)
    pending = None
    for l in range(DEPTH):
        sh1, sc1, g1, sh2, sc2, g2 = [mod[l, :, j * D:(j + 1) * D] for j in range(6)]
        gain1 = g_mix[l].reshape(1, D)
        wf = jnp.pad(w_in_bf[l, :, n_uvp + 3 * C_WIDTH:], ((0, 0), (0, HEAD - C_HEADS)))
        bf = jnp.pad(c_bf[l], (0, HEAD - C_HEADS)).reshape(1, HEAD)
        bsb = jnp.repeat(a_bs[l].T, HEAD, axis=1)
        yab = _mix_ab(l, xs, gain1, sc1, sh1, w_in_bf, a_ws[l], bsb, a_vg[l].reshape(1, A_WIDTH),
                      b_w[l], b_scale[l].reshape(1, B_WIDTH), moe=pending)
        if pending is not None:
            yab, xs = yab
        qa, ka, vt, f2 = _mix_c(l, xs, gain1, sc1, sh1, w_in_bf, wf, place, c_qg[l].reshape(1, HEAD),
                                c_kg[l].reshape(1, HEAD), bf)
        yc = _attn(_first_key_block(f2, c_qg[l], c_kg[l]), qa, ka, vt)
        x1, h2, info, counts = _out_proj(l, yab, yc, xs, w_out_bf, g1,
                                         g_ffn[l].reshape(1, D), sc2, sh2, wr_t, br)
        plan = _dispatch_plan(info, counts)
        y = _moe(l, *plan, h2, e_gate, e_up, e_down)
        xs, pending = x1, (y, info, g2)
    return _combine(xs, *pending).reshape(1, SEQ, D)
```

```python
import functools

import jax
import jax.numpy as jnp
import numpy as np
from jax import lax
from jax.experimental import pallas as pl
from jax.experimental.pallas import tpu as pltpu

F32 = jnp.float32
BF16 = jnp.bfloat16

D = 2048
SEQ = 8192
DEPTH = 2
CHUNK = 64
A_WIDTH = 512
A_HEADS = 4
A_BLOCK = 128
B_WIDTH = 512
POOL_WINDOWS = (2, 4, 8, 16)
POOL_HALO = 16
C_WIDTH = 1024
C_HEADS = 8
HEAD = 128
N_EXPERTS = 16
N_GROUPS = 4
GROUP = 4
D_EXPERT = 1024
EPS = 1e-6
LOG2E = 1.4426950408889634

TM = 512
TQ = 512
TK = 512
ATTN_HEADS = 2
ATTN_DROP_LOG2 = 64.0
MOE_BLK = 256
MOE_CHUNKS = 4
N_ASSIGN = 2 * SEQ
MOE_ROWS = N_ASSIGN + N_EXPERTS * MOE_BLK
MOE_NB = MOE_ROWS // MOE_BLK

VMEM_LIMIT = 56 * 1024 * 1024


def _nt_dot(a, b):
    return lax.dot_general(a, b, (((1,), (1,)), ((), ())), preferred_element_type=F32)


def _rms_mod(x, gain, scale, shift):
    ms = jnp.mean(x * x, axis=-1, keepdims=True)
    return (x * lax.rsqrt(ms + EPS) * gain) * (1.0 + scale) + shift


def _ada_kernel(c_ref, w_ref, b_ref, o_ref):
    ca = jax.nn.silu(c_ref[...]).astype(BF16)
    r = jnp.dot(ca, w_ref[...].astype(BF16), preferred_element_type=F32)
    o_ref[...] = r[0:1, :] + b_ref[...]


def _ada(c, w_ada, b_ada):
    tn = 1024
    c8 = jnp.broadcast_to(c, (16, D))
    return pl.pallas_call(
        _ada_kernel,
        out_shape=jax.ShapeDtypeStruct((DEPTH, 1, 6 * D), F32),
        grid=(DEPTH, 6 * D // tn),
        in_specs=[
            pl.BlockSpec((16, D), lambda l, j: (0, 0)),
            pl.BlockSpec((None, D, tn), lambda l, j: (l, 0, j)),
            pl.BlockSpec((None, 1, tn), lambda l, j: (l, 0, j)),
        ],
        out_specs=pl.BlockSpec((None, 1, tn), lambda l, j: (l, 0, j)),
        compiler_params=pltpu.CompilerParams(
            dimension_semantics=("arbitrary", "arbitrary"), vmem_limit_bytes=VMEM_LIMIT),
        name="ada",
    )(c8, w_ada, b_ada.reshape(DEPTH, 1, 6 * D))


def _moe_residual(x1_ref, y0_ref, y1_ref, info_ref, g2_ref):
    gw = info_ref[...].T
    moe = gw[:, 2:3] * y0_ref[:, 0, :] + gw[:, 3:4] * y1_ref[:, 0, :]
    return x1_ref[...] + g2_ref[...] * moe


def _mix_ab_kernel(after_moe, *refs):
    if after_moe:
        x1_ref, y0_ref, y1_ref, info_ref, g2_ref = refs[:5]
        (gain_ref, sc_ref, sh_ref, w_ref, ws_ref, bsb_ref, vg_ref, bw_ref, bscale_ref,
         y_ref, x_out_ref, pext_ref) = refs[5:]
        x = _moe_residual(x1_ref, y0_ref, y1_ref, info_ref, g2_ref)
        x_out_ref[...] = x
    else:
        (x_ref, gain_ref, sc_ref, sh_ref, w_ref, ws_ref, bsb_ref, vg_ref, bw_ref, bscale_ref,
         y_ref, pext_ref) = refs
        x = x_ref[...]
    i = pl.program_id(0)
    h = _rms_mod(x, gain_ref[...], sc_ref[...], sh_ref[...])
    z = jnp.dot(h.astype(BF16), w_ref[...], preferred_element_type=F32)
    u = jax.nn.gelu(z[:, :A_WIDTH])
    v = jax.nn.gelu(z[:, A_WIDTH:2 * A_WIDTH])
    p = z[:, 2 * A_WIDTH:]

    cid_t = lax.broadcasted_iota(jnp.int32, (A_BLOCK, A_BLOCK), 0) // CHUNK
    cid_s = lax.broadcasted_iota(jnp.int32, (A_BLOCK, A_BLOCK), 1) // CHUNK
    mask = cid_s <= cid_t
    for hh in range(A_HEADS):
        cs = slice(hh * HEAD, (hh + 1) * HEAD)
        vh = v[:, cs]
        vn = vh * lax.rsqrt(jnp.mean(vh * vh, axis=-1, keepdims=True) + EPS) * vg_ref[:, cs]
        vnb = vn.astype(BF16)
        wm = jnp.where(mask, ws_ref[hh], 0.0).astype(BF16)
        for n in range(TM // A_BLOCK):
            rs = slice(n * A_BLOCK, (n + 1) * A_BLOCK)
            sp = jnp.dot(wm, vnb[rs], preferred_element_type=F32) + bsb_ref[:, cs]
            y_ref[rs, cs] = (u[rs, cs] * sp).astype(BF16)

    @pl.when(i == 0)
    def _():
        pext_ref[0:POOL_HALO, :] = jnp.zeros((POOL_HALO, B_WIDTH), F32)

    pext_ref[POOL_HALO:POOL_HALO + TM, :] = p
    t1 = i * TM + lax.broadcasted_iota(jnp.int32, (TM, 1), 0) + 1
    for g, w in enumerate(POOL_WINDOWS):
        cs = slice(g * HEAD, (g + 1) * HEAD)
        acc = p[:, cs]
        for j in range(1, w):
            acc = acc + pext_ref[POOL_HALO - j:POOL_HALO - j + TM, cs]
        cnt = jnp.minimum(t1, w).astype(F32)
        d = acc / cnt - p[:, cs]
        yb = jnp.dot(d.astype(BF16), bw_ref[g].astype(BF16), preferred_element_type=F32)
        y_ref[:, A_WIDTH + g * HEAD:A_WIDTH + (g + 1) * HEAD] = (yb * bscale_ref[:, cs]).astype(BF16)
    pext_ref[0:POOL_HALO, :] = pext_ref[TM:TM + POOL_HALO, :]


def _mix_ab(layer, x, gain, sc, sh, w_in_bf, a_ws, bsb, vg, b_w, bscale, moe=None):
    row = lambda i: (0, 0)
    nt = SEQ // TM
    tile = pl.BlockSpec((TM, D), lambda i: (i, 0))
    y_spec = pl.BlockSpec((TM, A_WIDTH + B_WIDTH), lambda i: (i, 0))
    y_shape = jax.ShapeDtypeStruct((SEQ, A_WIDTH + B_WIDTH), BF16)
    if moe is None:
        lead_specs, lead_args = [tile], (x,)
        out_shape, out_specs = y_shape, y_spec
    else:
        y, info, g2 = moe
        lead_specs = [tile,
                      pl.BlockSpec((TM, 1, D), lambda i: (i, 0, 0)),
                      pl.BlockSpec((TM, 1, D), lambda i: (i + nt, 0, 0)),
                      pl.BlockSpec((8, TM), lambda i: (0, i)),
                      pl.BlockSpec((1, D), row)]
        lead_args = (x, y, y, info, g2)
        out_shape, out_specs = (y_shape, jax.ShapeDtypeStruct((SEQ, D), F32)), (y_spec, tile)
    return pl.pallas_call(
        functools.partial(_mix_ab_kernel, moe is not None),
        out_shape=out_shape,
        grid=(nt,),
        in_specs=lead_specs + [
            pl.BlockSpec((1, D), row), pl.BlockSpec((1, D), row), pl.BlockSpec((1, D), row),
            pl.BlockSpec((None, D, 2 * A_WIDTH + B_WIDTH), lambda i: (layer, 0, 0),
                         pipeline_mode=pl.Buffered(1)),
            pl.BlockSpec((A_HEADS, A_BLOCK, A_BLOCK), lambda i: (0, 0, 0)),
            pl.BlockSpec((A_BLOCK, A_WIDTH), row),
            pl.BlockSpec((1, A_WIDTH), row),
            pl.BlockSpec((len(POOL_WINDOWS), HEAD, HEAD), lambda i: (0, 0, 0)),
            pl.BlockSpec((1, B_WIDTH), row),
        ],
        out_specs=out_specs,
        scratch_shapes=[pltpu.VMEM((TM + POOL_HALO, B_WIDTH), F32)],
        compiler_params=pltpu.CompilerParams(
            dimension_semantics=("arbitrary",), vmem_limit_bytes=VMEM_LIMIT),
        name="mix_ab",
    )(*lead_args, gain, sc, sh, w_in_bf, a_ws, bsb, vg, b_w, bscale)


def _split3(x):
    hi = x.astype(BF16)
    rest = x - hi.astype(F32)
    mid = rest.astype(BF16)
    lo = (rest - mid.astype(F32)).astype(BF16)
    return jnp.concatenate([hi, mid, lo], axis=1)


def _mix_c_kernel(x_ref, gain_ref, sc_ref, sh_ref, w1_ref, w2_ref, wf_ref, place_ref, qg_ref, kg_ref,
                  bf_ref, q_ref, k_ref, vt_ref, f2_ref, carry_ref):
    i = pl.program_id(0)
    h = _rms_mod(x_ref[...], gain_ref[...], sc_ref[...], sh_ref[...])
    hb = h.astype(BF16)
    z1 = jnp.dot(hb, w1_ref[...], preferred_element_type=F32)
    half = C_WIDTH // 2
    z2 = jnp.dot(hb, w2_ref[:, :half], preferred_element_type=F32)
    lane = lax.broadcasted_iota(jnp.int32, (TM, HEAD), 1)
    q_aug = jnp.where(lane < 3, -1.0, 0.0).astype(BF16)
    qscale = HEAD ** -0.5 * LOG2E
    for hh in range(C_HEADS):
        qh = z1[:, hh * HEAD:(hh + 1) * HEAD]
        qn = qh * lax.rsqrt(jnp.mean(qh * qh, axis=-1, keepdims=True) + EPS) * qg_ref[...]
        q_ref[:, 2 * hh * HEAD:(2 * hh + 1) * HEAD] = (qn * qscale).astype(BF16)
        q_ref[:, (2 * hh + 1) * HEAD:(2 * hh + 2) * HEAD] = q_aug
        if hh < C_HEADS // 2:
            kh = z1[:, C_WIDTH + hh * HEAD:C_WIDTH + (hh + 1) * HEAD]
        else:
            kh = z2[:, hh * HEAD - half:(hh + 1) * HEAD - half]
        kn = kh * lax.rsqrt(jnp.mean(kh * kh, axis=-1, keepdims=True) + EPS) * kg_ref[...]
        k_ref[:, 2 * hh * HEAD:(2 * hh + 1) * HEAD] = kn.astype(BF16)
    vt = lax.dot_general(w2_ref[:, half:], hb, (((0,), (1,)), ((), ())), preferred_element_type=F32)
    vt_ref[...] = vt.astype(BF16)

    @pl.when(i == 0)
    def _():
        carry_ref[...] = jnp.zeros((1, HEAD), F32)

    fz = jnp.dot(hb, wf_ref[...], preferred_element_type=F32)
    logf = jax.nn.log_sigmoid(fz + bf_ref[...])
    r = lax.broadcasted_iota(jnp.int32, (TM, TM), 0)
    c = lax.broadcasted_iota(jnp.int32, (TM, TM), 1)
    lower = jnp.where(c <= r, 1.0, 0.0).astype(BF16)
    cum = jnp.dot(lower, _split3(logf), preferred_element_type=F32)
    f_cum = (cum[:, :HEAD] + cum[:, HEAD:2 * HEAD]) + cum[:, 2 * HEAD:] + carry_ref[...]
    carry_ref[...] = f_cum[TM - 1:TM, :]
    f2 = f_cum * LOG2E
    f2_ref[...] = f2
    aug = jnp.dot(_split3(f2), place_ref[...], preferred_element_type=F32)
    for hh in range(C_HEADS):
        k_ref[:, (2 * hh + 1) * HEAD:(2 * hh + 2) * HEAD] = aug[:, hh * HEAD:(hh + 1) * HEAD].astype(BF16)


def _mix_c(layer, x, gain, sc, sh, w_in_bf, wf, place, qg, kg, bf):
    row = lambda i: (0, 0)
    wblk = 2 * A_WIDTH + B_WIDTH
    tile = pl.BlockSpec((TM, 2 * C_WIDTH), lambda i: (i, 0))
    return pl.pallas_call(
        _mix_c_kernel,
        out_shape=(
            jax.ShapeDtypeStruct((SEQ, 2 * C_WIDTH), BF16),
            jax.ShapeDtypeStruct((SEQ, 2 * C_WIDTH), BF16),
            jax.ShapeDtypeStruct((C_WIDTH, SEQ), BF16),
            jax.ShapeDtypeStruct((SEQ, HEAD), F32),
        ),
        grid=(SEQ // TM,),
        in_specs=[
            pl.BlockSpec((TM, D), lambda i: (i, 0)),
            pl.BlockSpec((1, D), row), pl.BlockSpec((1, D), row), pl.BlockSpec((1, D), row),
            pl.BlockSpec((None, D, wblk), lambda i: (layer, 0, 1), pipeline_mode=pl.Buffered(1)),
            pl.BlockSpec((None, D, wblk), lambda i: (layer, 0, 2), pipeline_mode=pl.Buffered(1)),
            pl.BlockSpec((D, HEAD), row),
            pl.BlockSpec((3 * HEAD, C_WIDTH), row),
            pl.BlockSpec((1, HEAD), row), pl.BlockSpec((1, HEAD), row),
            pl.BlockSpec((1, HEAD), row),
        ],
        out_specs=(tile, tile, pl.BlockSpec((C_WIDTH, TM), lambda i: (0, i)),
                   pl.BlockSpec((TM, HEAD), lambda i: (i, 0))),
        scratch_shapes=[pltpu.VMEM((1, HEAD), F32)],
        compiler_params=pltpu.CompilerParams(
            dimension_semantics=("arbitrary",), vmem_limit_bytes=VMEM_LIMIT),
        name="mix_c",
    )(x, gain, sc, sh, w_in_bf, w_in_bf, wf, place, qg, kg, bf)


def _attn_kernel(first_ref, q_ref, k_ref, vt_ref, o_ref, st_ref):
    i = pl.program_id(1)
    first = first_ref[pl.program_id(0) * (SEQ // TQ) + i]

    def scores_t(g, j):
        off = pl.multiple_of(j * TK, TK)
        cols = slice(2 * g * HEAD, 2 * (g + 1) * HEAD)
        return _nt_dot(k_ref[pl.ds(off, TK), cols], q_ref[:, cols])

    def update(g, read_st, j, m, l, acc):
        m_new = jnp.maximum(m, jnp.max(read_st(), axis=0, keepdims=True))
        alpha = jnp.exp2(m - m_new)
        p = jnp.exp2(read_st() - m_new)
        l_new = alpha * l + jnp.sum(p, axis=0, keepdims=True)
        off = pl.multiple_of(j * TK, TK)
        pv = jnp.dot(vt_ref[g * HEAD:(g + 1) * HEAD, pl.ds(off, TK)], p.astype(BF16),
                     preferred_element_type=F32)
        return m_new, l_new, alpha * acc + pv

    for g in range(ATTN_HEADS):
        st_ref[g] = scores_t(g, first)

    def body(j, carry):
        out = []
        for g in range(ATTN_HEADS):
            out.append(update(g, lambda g=g: st_ref[g], j, *carry[g]))
            st_ref[g] = scores_t(g, j + 1)
        return tuple(out)

    init = (jnp.full((1, TQ), -jnp.inf, F32), jnp.zeros((1, TQ), F32), jnp.zeros((HEAD, TQ), F32))
    carry = lax.fori_loop(first, i, body, (init,) * ATTN_HEADS)
    kpos = lax.broadcasted_iota(jnp.int32, (TK, TQ), 0)
    qpos = lax.broadcasted_iota(jnp.int32, (TK, TQ), 1)
    for g in range(ATTN_HEADS):
        m, l, acc = update(g, lambda g=g: jnp.where(kpos <= qpos, st_ref[g], -jnp.inf), i, *carry[g])
        o_ref[:, g * HEAD:(g + 1) * HEAD] = (acc / l).T.astype(BF16)


def _first_key_block(f2, qg, kg):
    nq = SEQ // TQ
    bound = LOG2E * HEAD ** 0.5 * jnp.max(jnp.abs(qg)) * jnp.max(jnp.abs(kg)) * 1.02
    f_start = f2[0::TQ, :C_HEADS]
    f_end = f2[TK - 1::TK, :C_HEADS]
    gap = f_end[None, :, :] - f_start[:, None, :]
    needed = gap <= 2.0 * bound + ATTN_DROP_LOG2
    needed = jnp.any(needed.reshape(nq, SEQ // TK, C_HEADS // ATTN_HEADS, ATTN_HEADS), axis=-1)
    first = jnp.argmax(needed, axis=1)
    return first.T.reshape(-1).astype(jnp.int32)


def _attn(first, qa, ka, vt):
    gw = ATTN_HEADS * HEAD
    return pl.pallas_call(
        _attn_kernel,
        out_shape=jax.ShapeDtypeStruct((SEQ, C_WIDTH), BF16),
        grid_spec=pltpu.PrefetchScalarGridSpec(
            num_scalar_prefetch=1,
            grid=(C_HEADS // ATTN_HEADS, SEQ // TQ),
            in_specs=[
                pl.BlockSpec((TQ, 2 * gw), lambda h, i, f: (i, h)),
                pl.BlockSpec((SEQ, 2 * gw), lambda h, i, f: (0, h)),
                pl.BlockSpec((gw, SEQ), lambda h, i, f: (h, 0)),
            ],
            out_specs=pl.BlockSpec((TQ, gw), lambda h, i, f: (i, h)),
            scratch_shapes=[pltpu.VMEM((ATTN_HEADS, TK, TQ), F32)],
        ),
        compiler_params=pltpu.CompilerParams(
            dimension_semantics=("arbitrary", "arbitrary"), vmem_limit_bytes=VMEM_LIMIT),
        name="attn",
    )(first, qa, ka, vt)


def _first_max(vals):
    best, idx = vals[0], jnp.zeros(vals[0].shape, jnp.int32)
    for j in range(1, len(vals)):
        better = vals[j] > best
        idx = jnp.where(better, j, idx)
        best = jnp.where(better, vals[j], best)
    return idx, best


def _pick(idx, vals):
    out = vals[-1]
    for j in range(len(vals) - 2, -1, -1):
        out = jnp.where(idx == j, vals[j], out)
    return out


def _out_kernel(yab_ref, yc_ref, x_ref, w_ref, g1_ref, gain_ref, sc_ref, sh_ref, wr_ref, br_ref,
                x1_ref, h2_ref, info_ref, cnt_ref, carry_ref):
    i = pl.program_id(0)
    y = jnp.dot(yab_ref[...], w_ref[0:A_WIDTH + B_WIDTH, :], preferred_element_type=F32)
    y = y + jnp.dot(yc_ref[...], w_ref[A_WIDTH + B_WIDTH:, :], preferred_element_type=F32)
    x1 = x_ref[...] + g1_ref[...] * y
    x1_ref[...] = x1
    h2 = _rms_mod(x1, gain_ref[...], sc_ref[...], sh_ref[...])
    h2_ref[:, 0, :] = h2

    logits = _nt_dot(wr_ref[...], h2.astype(BF16))
    scores = jax.nn.sigmoid(logits)
    sel = scores + br_ref[...]
    sel_r = [sel[k:k + 1, :] for k in range(N_EXPERTS)]
    sc_r = [scores[k:k + 1, :] for k in range(N_EXPERTS)]
    grp = []
    for g in range(N_GROUPS):
        a = sel_r[GROUP * g:GROUP * (g + 1)]
        pair = [a[p] + a[q] for p in range(GROUP) for q in range(p + 1, GROUP)]
        grp.append(functools.reduce(jnp.maximum, pair))
    gi, _ = _first_max(grp)
    cand = [_pick(gi, [sel_r[GROUP * g + j] for g in range(N_GROUPS)]) for j in range(GROUP)]
    cand_s = [_pick(gi, [sc_r[GROUP * g + j] for g in range(N_GROUPS)]) for j in range(GROUP)]
    i0, _ = _first_max(cand)
    i1, _ = _first_max([jnp.where(i0 == j, -jnp.inf, cand[j]) for j in range(GROUP)])
    s0, s1 = _pick(i0, cand_s), _pick(i1, cand_s)
    e0, e1 = GROUP * gi + i0, GROUP * gi + i1
    den = s0 + s1

    @pl.when(i == 0)
    def _():
        carry_ref[...] = jnp.zeros((N_EXPERTS, HEAD), F32)

    ek = lax.broadcasted_iota(jnp.int32, (N_EXPERTS, TM), 0)
    oh0 = (ek == e0).astype(F32)
    oh1 = (ek == e1).astype(F32)
    both = oh0 + oh1
    r = lax.broadcasted_iota(jnp.int32, (TM, TM), 0)
    c = lax.broadcasted_iota(jnp.int32, (TM, TM), 1)
    before = (r < c).astype(BF16)
    run = jnp.dot(both.astype(BF16), before, preferred_element_type=F32) + carry_ref[:, 0:1]
    rank0 = jnp.sum(run * oh0, axis=0, keepdims=True)
    rank1 = jnp.sum(run * oh1, axis=0, keepdims=True)
    total = carry_ref[...] + jnp.sum(both, axis=-1, keepdims=True)
    carry_ref[...] = total
    cnt_ref[...] = total

    info_ref[0:1, :] = e0.astype(F32)
    info_ref[1:2, :] = e1.astype(F32)
    info_ref[2:3, :] = s0 / den
    info_ref[3:4, :] = s1 / den
    info_ref[4:5, :] = rank0
    info_ref[5:6, :] = rank1
    info_ref[6:8, :] = jnp.zeros((2, TM), F32)


def _out_proj(layer, yab, yc, x, w_out, g1, gain, sc, sh, wr_t, br):
    row = lambda i: (0, 0)
    return pl.pallas_call(
        _out_kernel,
        out_shape=(
            jax.ShapeDtypeStruct((SEQ, D), F32),
            jax.ShapeDtypeStruct((SEQ, 1, D), F32),
            jax.ShapeDtypeStruct((8, SEQ), F32),
            jax.ShapeDtypeStruct((N_EXPERTS, HEAD), F32),
        ),
        grid=(SEQ // TM,),
        in_specs=[
            pl.BlockSpec((TM, A_WIDTH + B_WIDTH), lambda i: (i, 0)),
            pl.BlockSpec((TM, C_WIDTH), lambda i: (i, 0)),
            pl.BlockSpec((TM, D), lambda i: (i, 0)),
            pl.BlockSpec((None, D, D), lambda i: (layer, 0, 0), pipeline_mode=pl.Buffered(1)),
            pl.BlockSpec((1, D), row), pl.BlockSpec((1, D), row),
            pl.BlockSpec((1, D), row), pl.BlockSpec((1, D), row),
            pl.BlockSpec((N_EXPERTS, D), row),
            pl.BlockSpec((N_EXPERTS, 1), row),
        ],
        out_specs=(
            pl.BlockSpec((TM, D), lambda i: (i, 0)),
            pl.BlockSpec((TM, 1, D), lambda i: (i, 0, 0)),
            pl.BlockSpec((8, TM), lambda i: (0, i)),
            pl.BlockSpec((N_EXPERTS, HEAD), row),
        ),
        scratch_shapes=[pltpu.VMEM((N_EXPERTS, HEAD), F32)],
        compiler_params=pltpu.CompilerParams(
            dimension_semantics=("arbitrary",), vmem_limit_bytes=VMEM_LIMIT),
        name="out_proj",
    )(yab, yc, x, w_out, g1, gain, sc, sh, wr_t, br)


def _moe_kernel(layer, dest_ref, bexp_ref, nvalid_ref, nused_ref, newexp_ref, nextexp_ref,
                h_hbm, wg_hbm, wu_hbm, wd_hbm, zeros_hbm, y_hbm,
                code_ref, xs_ref, ys_ref, xb_ref, hid_ref, wg32_ref, wu32_ref, wd32_ref,
                wg_ref, wu_ref, wd_ref, gsem, ssem, wsem):
    b = pl.program_id(0)
    n_used = nused_ref[0]
    slot = b % 2

    def weight_copies(e):
        return (pltpu.make_async_copy(wg_hbm.at[layer, e], wg32_ref, wsem.at[0]),
                pltpu.make_async_copy(wu_hbm.at[layer, e], wu32_ref, wsem.at[1]),
                pltpu.make_async_copy(wd_hbm.at[layer, e], wd32_ref, wsem.at[2]))

    def switch_expert():
        for cp in weight_copies(bexp_ref[b]):
            cp.wait()

        def cast(k, _):
            r = pl.multiple_of(k * 256, 256)
            wg_ref[pl.ds(r, 256), :] = wg32_ref[pl.ds(r, 256), :].astype(BF16)
            wu_ref[pl.ds(r, 256), :] = wu32_ref[pl.ds(r, 256), :].astype(BF16)
            r = pl.multiple_of(k * 128, 128)
            wd_ref[pl.ds(r, 128), :] = wd32_ref[pl.ds(r, 128), :].astype(BF16)
            return 0
        lax.fori_loop(0, D // 256, cast, 0)

        @pl.when(nextexp_ref[b] >= 0)
        def _():
            for cp in weight_copies(jnp.maximum(nextexp_ref[b], 0)):
                cp.start(priority=1)

    def gather_copy(blk, r, s):
        tok = code_ref[blk * MOE_BLK + r] & (SEQ - 1)
        return pltpu.make_async_copy(h_hbm.at[tok], xs_ref.at[s, pl.ds(r, 1), :],
                                     gsem.at[s])

    def scatter_copy(blk, r, s):
        dst = code_ref[blk * MOE_BLK + r]
        return pltpu.make_async_copy(ys_ref.at[s, pl.ds(r, 1), :], y_hbm.at[dst],
                                     ssem.at[s])

    def for_rows(n, fn):
        def body(r, _):
            fn(r)
            return 0
        lax.fori_loop(0, n, body, 0)

    def scatter_wait(n, s):
        n8 = pl.multiple_of((n >> 3) << 3, 8)

        @pl.when(n8 > 0)
        def _():
            pltpu.make_async_copy(ys_ref.at[s, pl.ds(0, n8), :], ys_ref.at[s, pl.ds(0, n8), :],
                                  ssem.at[s]).wait()

        for r in range(7):
            @pl.when(n8 + r < n)
            def _():
                pltpu.make_async_copy(ys_ref.at[s, pl.ds(r, 1), :], y_hbm.at[r],
                                      ssem.at[s]).wait()

    @pl.when(b == 0)
    def _():
        for cp in weight_copies(bexp_ref[0]):
            cp.start()

        fill = pltpu.make_async_copy(zeros_hbm, code_ref, wsem.at[3])
        fill.start()
        fill.wait()

        def place(a, _):
            code_ref[dest_ref[a]] = a
            return 0
        lax.fori_loop(0, N_ASSIGN, place, 0, unroll=8)
        for_rows(MOE_BLK, lambda r: gather_copy(0, r, 0).start())

    def step(prefetch):
        @pl.when(newexp_ref[b] == 1)
        def _():
            switch_expert()

        for r in range(MOE_BLK):
            gather_copy(b, r, slot).wait()

        xb_ref[...] = xs_ref[slot].astype(BF16)
        prev = jnp.maximum(b - 1, 0)
        n_prev = jnp.where(b >= 1, nvalid_ref[prev], 0)
        x = xb_ref[...]
        rows_per = MOE_BLK // MOE_CHUNKS

        hc = D_EXPERT // MOE_CHUNKS
        for c in range(MOE_CHUNKS):
            cs = slice(c * hc, (c + 1) * hc)
            gate = jnp.dot(x, wg_ref[:, cs], preferred_element_type=F32)
            up = jnp.dot(x, wu_ref[:, cs], preferred_element_type=F32)
            hid = jax.nn.silu(gate) * up
            if prefetch:
                for r in range(c * rows_per, (c + 1) * rows_per):
                    gather_copy(b + 1, r, 1 - slot).start()
                tie = pltpu.bitcast(xs_ref[slot, 0:8, 0:hc], jnp.uint32)
                tie = pltpu.bitcast((tie >> 16) >> 16, F32)
                hid = jnp.concatenate([hid[0:8] + tie, hid[8:]], axis=0)
            hid_ref[:, cs] = hid.astype(BF16)

        scatter_wait(jnp.where(b >= 2, nvalid_ref[jnp.maximum(b - 2, 0)], 0), slot)
        hidb = hid_ref[...]
        oc = D // MOE_CHUNKS
        for c in range(MOE_CHUNKS):
            for r in range(c * rows_per, (c + 1) * rows_per):
                @pl.when(r < n_prev)
                def _():
                    scatter_copy(prev, r, 1 - slot).start(priority=1)
            cs = slice(c * oc, (c + 1) * oc)
            ys_ref[slot, :, cs] = jnp.dot(hidb, wd_ref[:, cs], preferred_element_type=F32)

    @pl.when(b < n_used - 1)
    def _():
        step(True)

    @pl.when(b == n_used - 1)
    def _():
        step(False)
        for_rows(nvalid_ref[b], lambda r: scatter_copy(b, r, slot).start())
        scatter_wait(nvalid_ref[b], slot)
        scatter_wait(jnp.where(b >= 1, nvalid_ref[jnp.maximum(b - 1, 0)], 0), 1 - slot)


def _moe(layer, dest, bexp, nvalid, nused, newexp, nextexp, h2, wg, wu, wd):
    hbm = pl.BlockSpec(memory_space=pl.ANY)
    return pl.pallas_call(
        functools.partial(_moe_kernel, layer),
        out_shape=jax.ShapeDtypeStruct((N_ASSIGN, 1, D), F32),
        grid_spec=pltpu.PrefetchScalarGridSpec(
            num_scalar_prefetch=6,
            grid=(MOE_NB,),
            in_specs=[hbm, hbm, hbm, hbm, hbm],
            out_specs=hbm,
            scratch_shapes=[
                pltpu.SMEM((MOE_ROWS,), jnp.int32),
                pltpu.VMEM((2, MOE_BLK, D), F32),
                pltpu.VMEM((2, MOE_BLK, D), F32),
                pltpu.VMEM((MOE_BLK, D), BF16),
                pltpu.VMEM((MOE_BLK, D_EXPERT), BF16),
                pltpu.VMEM((D, D_EXPERT), F32),
                pltpu.VMEM((D, D_EXPERT), F32),
                pltpu.VMEM((D_EXPERT, D), F32),
                pltpu.VMEM((D, D_EXPERT), BF16),
                pltpu.VMEM((D, D_EXPERT), BF16),
                pltpu.VMEM((D_EXPERT, D), BF16),
                pltpu.SemaphoreType.DMA((2,)),
                pltpu.SemaphoreType.DMA((2,)),
                pltpu.SemaphoreType.DMA((4,)),
            ],
        ),
        compiler_params=pltpu.CompilerParams(
            dimension_semantics=("arbitrary",), vmem_limit_bytes=VMEM_LIMIT),
        name="moe",
    )(dest, bexp, nvalid, nused, newexp, nextexp, h2, wg, wu, wd, jnp.zeros((MOE_ROWS,), jnp.int32))


def _combine_kernel(x1_ref, y0_ref, y1_ref, info_ref, g2_ref, o_ref):
    o_ref[...] = _moe_residual(x1_ref, y0_ref, y1_ref, info_ref, g2_ref)


def _combine(x1, y, info, g2):
    nt = SEQ // TM
    return pl.pallas_call(
        _combine_kernel,
        out_shape=jax.ShapeDtypeStruct((SEQ, D), F32),
        grid=(nt,),
        in_specs=[
            pl.BlockSpec((TM, D), lambda i: (i, 0)),
            pl.BlockSpec((TM, 1, D), lambda i: (i, 0, 0)),
            pl.BlockSpec((TM, 1, D), lambda i: (i + nt, 0, 0)),
            pl.BlockSpec((8, TM), lambda i: (0, i)),
            pl.BlockSpec((1, D), lambda i: (0, 0)),
        ],
        out_specs=pl.BlockSpec((TM, D), lambda i: (i, 0)),
        compiler_params=pltpu.CompilerParams(
            dimension_semantics=("arbitrary",), vmem_limit_bytes=VMEM_LIMIT),
        name="combine",
    )(x1, y, y, info, g2)


def _dispatch_plan(info, counts):
    cnt = counts[:, 0].astype(jnp.int32)
    padded = (cnt + MOE_BLK - 1) // MOE_BLK * MOE_BLK
    ends = jnp.cumsum(padded)
    pad_start = ends - padded
    e = info[0:2].astype(jnp.int32)
    rank = info[4:6].astype(jnp.int32)
    onehot = e[:, :, None] == jnp.arange(N_EXPERTS, dtype=jnp.int32)
    dest = jnp.sum(jnp.where(onehot, pad_start, 0), axis=-1) + rank
    blk_start = jnp.arange(MOE_NB, dtype=jnp.int32) * MOE_BLK
    bexp = jnp.sum(blk_start[:, None] >= ends[None, :], axis=-1)
    bexp = jnp.minimum(bexp, N_EXPERTS - 1).astype(jnp.int32)
    in_blk = jnp.arange(N_EXPERTS, dtype=jnp.int32)[None, :] == bexp[:, None]
    left = jnp.sum(jnp.where(in_blk, cnt + pad_start, 0), axis=-1) - blk_start
    nvalid = jnp.clip(left, 0, MOE_BLK).astype(jnp.int32)
    nused = (ends[-1:] // MOE_BLK).astype(jnp.int32)
    used = jnp.arange(MOE_NB, dtype=jnp.int32) < nused[0]
    newexp = jnp.concatenate([jnp.ones((1,), jnp.int32), (bexp[1:] != bexp[:-1]).astype(jnp.int32)])
    later = (bexp[None, :] > bexp[:, None]) & used[None, :]
    nextexp = jnp.min(jnp.where(later, bexp[None, :], N_EXPERTS), axis=1)
    nextexp = jnp.where(nextexp < N_EXPERTS, nextexp, -1).astype(jnp.int32)
    return dest.reshape(N_ASSIGN), bexp, nvalid, nused, newexp, nextexp


def kernel(x, c, w_ada, b_ada, g_mix, g_ffn, w_in, a_ws, a_bs, a_vg, b_w, b_scale,
           c_qg, c_kg, c_bf, w_out, w_router, b_router, e_gate, e_up, e_down):
    xs = x.reshape(SEQ, D)
    mod = _ada(c, w_ada, b_ada)
    wr_t = w_router.T.astype(BF16)
    br = b_router.reshape(N_EXPERTS, 1).astype(F32)
    n_uvp = 2 * A_WIDTH + B_WIDTH
    place = np.zeros((3 * HEAD, C_WIDTH), np.float32)
    for pj in range(3):
        for ph in range(C_HEADS):
            place[pj * HEAD + ph, ph * HEAD + pj] = 1.0
    place = jnp.asarray(place, BF16)
    w_in_bf, w_out_bf = w_in.astype(BF16), w_out.astype(BF16)
    pending = None
    for l in range(DEPTH):
        sh1, sc1, g1, sh2, sc2, g2 = [mod[l, :, j * D:(j + 1) * D] for j in range(6)]
        gain1 = g_mix[l].reshape(1, D)
        wf = jnp.pad(w_in_bf[l, :, n_uvp + 3 * C_WIDTH:], ((0, 0), (0, HEAD - C_HEADS)))
        bf = jnp.pad(c_bf[l], (0, HEAD - C_HEADS)).reshape(1, HEAD)
        bsb = jnp.repeat(a_bs[l].T, HEAD, axis=1)
        yab = _mix_ab(l, xs, gain1, sc1, sh1, w_in_bf, a_ws[l], bsb, a_vg[l].reshape(1, A_WIDTH),
                      b_w[l], b_scale[l].reshape(1, B_WIDTH), moe=pending)
        if pending is not None:
            yab, xs = yab
        qa, ka, vt, f2 = _mix_c(l, xs, gain1, sc1, sh1, w_in_bf, wf, place, c_qg[l].reshape(1, HEAD),
                                c_kg[l].reshape(1, HEAD), bf)
        yc = _attn(_first_key_block(f2, c_qg[l], c_kg[l]), qa, ka, vt)
        x1, h2, info, counts = _out_proj(l, yab, yc, xs, w_out_bf, g1,
                                         g_ffn[l].reshape(1, D), sc2, sh2, wr_t, br)
        plan = _dispatch_plan(info, counts)
        y = _moe(l, *plan, h2, e_gate, e_up, e_down)
        xs, pending = x1, (y, info, g2)
    return _combine(xs, *pending).reshape(1, SEQ, D)
```

```python
import functools

import jax
import jax.numpy as jnp
import numpy as np
from jax import lax
from jax.experimental import pallas as pl
from jax.experimental.pallas import tpu as pltpu

F32 = jnp.float32
BF16 = jnp.bfloat16

D = 2048
SEQ = 8192
DEPTH = 2
CHUNK = 64
A_WIDTH = 512
A_HEADS = 4
A_BLOCK = 128
B_WIDTH = 512
POOL_WINDOWS = (2, 4, 8, 16)
POOL_HALO = 16
C_WIDTH = 1024
C_HEADS = 8
HEAD = 128
N_EXPERTS = 16
N_GROUPS = 4
GROUP = 4
D_EXPERT = 1024
EPS = 1e-6
LOG2E = 1.4426950408889634

TM = 512
TQ = 512
TK = 512
ATTN_HEADS = 2
ADA_TN = 256
ADA_BLOCKS = 6 * D // ADA_TN
ATTN_DROP_LOG2 = 64.0
MOE_BLK = 256
MOE_CHUNKS = 4
N_ASSIGN = 2 * SEQ
MOE_ROWS = N_ASSIGN + N_EXPERTS * MOE_BLK
MOE_NB = MOE_ROWS // MOE_BLK

VMEM_LIMIT = 56 * 1024 * 1024


def _nt_dot(a, b):
    return lax.dot_general(a, b, (((1,), (1,)), ((), ())), preferred_element_type=F32)


def _rms_mod(x, gain, scale, shift):
    ms = jnp.mean(x * x, axis=-1, keepdims=True)
    return (x * lax.rsqrt(ms + EPS) * gain) * (1.0 + scale) + shift


def _ada_kernel(c_ref, w_ref, b_ref, o_ref):
    ca = jax.nn.silu(c_ref[...]).astype(BF16)
    r = jnp.dot(ca, w_ref[...].astype(BF16), preferred_element_type=F32)
    o_ref[...] = r[0:1, :] + b_ref[...]


def _ada(c16, w_ada, b_ada3):
    tn = 1024
    return pl.pallas_call(
        _ada_kernel,
        out_shape=jax.ShapeDtypeStruct((1, 6 * D), F32),
        grid=(6 * D // tn,),
        in_specs=[
            pl.BlockSpec((16, D), lambda j: (0, 0)),
            pl.BlockSpec((None, D, tn), lambda j: (0, 0, j)),
            pl.BlockSpec((None, 1, tn), lambda j: (0, 0, j)),
        ],
        out_specs=pl.BlockSpec((1, tn), lambda j: (0, j)),
        compiler_params=pltpu.CompilerParams(
            dimension_semantics=("arbitrary",), vmem_limit_bytes=VMEM_LIMIT),
        name="ada",
    )(c16, w_ada, b_ada3)


def _moe_residual(x1_ref, y0_ref, y1_ref, info_ref, g2_ref):
    gw = info_ref[...].T
    moe = gw[:, 2:3] * y0_ref[:, 0, :] + gw[:, 3:4] * y1_ref[:, 0, :]
    return x1_ref[...] + g2_ref[...] * moe


def _mix_ab_kernel(after_moe, *refs):
    if after_moe:
        x1_ref, y0_ref, y1_ref, info_ref, g2_ref = refs[:5]
        (gain_ref, sc_ref, sh_ref, w_ref, ws_ref, bsb_ref, vg_ref, bw_ref, bscale_ref,
         y_ref, x_out_ref, pext_ref) = refs[5:]
        x = _moe_residual(x1_ref, y0_ref, y1_ref, info_ref, g2_ref)
        x_out_ref[...] = x
    else:
        (x_ref, gain_ref, sc_ref, sh_ref, w_ref, ws_ref, bsb_ref, vg_ref, bw_ref, bscale_ref,
         y_ref, pext_ref) = refs
        x = x_ref[...]
    i = pl.program_id(0)
    h = _rms_mod(x, gain_ref[...], sc_ref[...], sh_ref[...])
    z = jnp.dot(h.astype(BF16), w_ref[...], preferred_element_type=F32)
    u = jax.nn.gelu(z[:, :A_WIDTH])
    v = jax.nn.gelu(z[:, A_WIDTH:2 * A_WIDTH])
    p = z[:, 2 * A_WIDTH:]

    cid_t = lax.broadcasted_iota(jnp.int32, (A_BLOCK, A_BLOCK), 0) // CHUNK
    cid_s = lax.broadcasted_iota(jnp.int32, (A_BLOCK, A_BLOCK), 1) // CHUNK
    mask = cid_s <= cid_t
    for hh in range(A_HEADS):
        cs = slice(hh * HEAD, (hh + 1) * HEAD)
        vh = v[:, cs]
        vn = vh * lax.rsqrt(jnp.mean(vh * vh, axis=-1, keepdims=True) + EPS) * vg_ref[:, cs]
        vnb = vn.astype(BF16)
        wm = jnp.where(mask, ws_ref[hh], 0.0).astype(BF16)
        for n in range(TM // A_BLOCK):
            rs = slice(n * A_BLOCK, (n + 1) * A_BLOCK)
            sp = jnp.dot(wm, vnb[rs], preferred_element_type=F32) + bsb_ref[:, cs]
            y_ref[rs, cs] = (u[rs, cs] * sp).astype(BF16)

    @pl.when(i == 0)
    def _():
        pext_ref[0:POOL_HALO, :] = jnp.zeros((POOL_HALO, B_WIDTH), F32)

    pext_ref[POOL_HALO:POOL_HALO + TM, :] = p
    t1 = i * TM + lax.broadcasted_iota(jnp.int32, (TM, 1), 0) + 1
    for g, w in enumerate(POOL_WINDOWS):
        cs = slice(g * HEAD, (g + 1) * HEAD)
        acc = p[:, cs]
        for j in range(1, w):
            acc = acc + pext_ref[POOL_HALO - j:POOL_HALO - j + TM, cs]
        cnt = jnp.minimum(t1, w).astype(F32)
        d = acc / cnt - p[:, cs]
        yb = jnp.dot(d.astype(BF16), bw_ref[g].astype(BF16), preferred_element_type=F32)
        y_ref[:, A_WIDTH + g * HEAD:A_WIDTH + (g + 1) * HEAD] = (yb * bscale_ref[:, cs]).astype(BF16)
    pext_ref[0:POOL_HALO, :] = pext_ref[TM:TM + POOL_HALO, :]


def _mix_ab(layer, x, gain, sc, sh, w_in_bf, a_ws, bsb, vg, b_w, bscale, moe=None):
    row = lambda i: (0, 0)
    nt = SEQ // TM
    tile = pl.BlockSpec((TM, D), lambda i: (i, 0))
    y_spec = pl.BlockSpec((TM, A_WIDTH + B_WIDTH), lambda i: (i, 0))
    y_shape = jax.ShapeDtypeStruct((SEQ, A_WIDTH + B_WIDTH), BF16)
    if moe is None:
        lead_specs, lead_args = [tile], (x,)
        out_shape, out_specs = y_shape, y_spec
    else:
        y, info, g2 = moe
        lead_specs = [tile,
                      pl.BlockSpec((TM, 1, D), lambda i: (i, 0, 0)),
                      pl.BlockSpec((TM, 1, D), lambda i: (i + nt, 0, 0)),
                      pl.BlockSpec((8, TM), lambda i: (0, i)),
                      pl.BlockSpec((1, D), row)]
        lead_args = (x, y, y, info, g2)
        out_shape, out_specs = (y_shape, jax.ShapeDtypeStruct((SEQ, D), F32)), (y_spec, tile)
    return pl.pallas_call(
        functools.partial(_mix_ab_kernel, moe is not None),
        out_shape=out_shape,
        grid=(nt,),
        in_specs=lead_specs + [
            pl.BlockSpec((1, D), row), pl.BlockSpec((1, D), row), pl.BlockSpec((1, D), row),
            pl.BlockSpec((None, D, 2 * A_WIDTH + B_WIDTH), lambda i: (layer, 0, 0),
                         pipeline_mode=pl.Buffered(1)),
            pl.BlockSpec((A_HEADS, A_BLOCK, A_BLOCK), lambda i: (0, 0, 0)),
            pl.BlockSpec((A_BLOCK, A_WIDTH), row),
            pl.BlockSpec((1, A_WIDTH), row),
            pl.BlockSpec((len(POOL_WINDOWS), HEAD, HEAD), lambda i: (0, 0, 0)),
            pl.BlockSpec((1, B_WIDTH), row),
        ],
        out_specs=out_specs,
        scratch_shapes=[pltpu.VMEM((TM + POOL_HALO, B_WIDTH), F32)],
        compiler_params=pltpu.CompilerParams(
            dimension_semantics=("arbitrary",), vmem_limit_bytes=VMEM_LIMIT),
        name="mix_ab",
    )(*lead_args, gain, sc, sh, w_in_bf, a_ws, bsb, vg, b_w, bscale)


def _split3(x):
    hi = x.astype(BF16)
    rest = x - hi.astype(F32)
    mid = rest.astype(BF16)
    lo = (rest - mid.astype(F32)).astype(BF16)
    return jnp.concatenate([hi, mid, lo], axis=1)


def _mix_c_kernel(x_ref, gain_ref, sc_ref, sh_ref, w1_ref, w2_ref, wf_ref, place_ref, qg_ref, kg_ref,
                  bf_ref, q_ref, k_ref, vt_ref, f2_ref, carry_ref):
    i = pl.program_id(0)
    h = _rms_mod(x_ref[...], gain_ref[...], sc_ref[...], sh_ref[...])
    hb = h.astype(BF16)
    z1 = jnp.dot(hb, w1_ref[...], preferred_element_type=F32)
    half = C_WIDTH // 2
    z2 = jnp.dot(hb, w2_ref[:, :half], preferred_element_type=F32)
    lane = lax.broadcasted_iota(jnp.int32, (TM, HEAD), 1)
    q_aug = jnp.where(lane < 3, -1.0, 0.0).astype(BF16)
    qscale = HEAD ** -0.5 * LOG2E
    for hh in range(C_HEADS):
        qh = z1[:, hh * HEAD:(hh + 1) * HEAD]
        qn = qh * lax.rsqrt(jnp.mean(qh * qh, axis=-1, keepdims=True) + EPS) * qg_ref[...]
        q_ref[:, 2 * hh * HEAD:(2 * hh + 1) * HEAD] = (qn * qscale).astype(BF16)
        q_ref[:, (2 * hh + 1) * HEAD:(2 * hh + 2) * HEAD] = q_aug
        if hh < C_HEADS // 2:
            kh = z1[:, C_WIDTH + hh * HEAD:C_WIDTH + (hh + 1) * HEAD]
        else:
            kh = z2[:, hh * HEAD - half:(hh + 1) * HEAD - half]
        kn = kh * lax.rsqrt(jnp.mean(kh * kh, axis=-1, keepdims=True) + EPS) * kg_ref[...]
        k_ref[:, 2 * hh * HEAD:(2 * hh + 1) * HEAD] = kn.astype(BF16)
    vt = lax.dot_general(w2_ref[:, half:], hb, (((0,), (1,)), ((), ())), preferred_element_type=F32)
    vt_ref[...] = vt.astype(BF16)

    @pl.when(i == 0)
    def _():
        carry_ref[...] = jnp.zeros((1, HEAD), F32)

    fz = jnp.dot(hb, wf_ref[...], preferred_element_type=F32)
    logf = jax.nn.log_sigmoid(fz + bf_ref[...])
    r = lax.broadcasted_iota(jnp.int32, (TM, TM), 0)
    c = lax.broadcasted_iota(jnp.int32, (TM, TM), 1)
    lower = jnp.where(c <= r, 1.0, 0.0).astype(BF16)
    cum = jnp.dot(lower, _split3(logf), preferred_element_type=F32)
    f_cum = (cum[:, :HEAD] + cum[:, HEAD:2 * HEAD]) + cum[:, 2 * HEAD:] + carry_ref[...]
    carry_ref[...] = f_cum[TM - 1:TM, :]
    f2 = f_cum * LOG2E
    f2_ref[...] = f2
    aug = jnp.dot(_split3(f2), place_ref[...], preferred_element_type=F32)
    for hh in range(C_HEADS):
        k_ref[:, (2 * hh + 1) * HEAD:(2 * hh + 2) * HEAD] = aug[:, hh * HEAD:(hh + 1) * HEAD].astype(BF16)


def _mix_c(layer, x, gain, sc, sh, w_in_bf, wf, place, qg, kg, bf):
    row = lambda i: (0, 0)
    wblk = 2 * A_WIDTH + B_WIDTH
    tile = pl.BlockSpec((TM, 2 * C_WIDTH), lambda i: (i, 0))
    return pl.pallas_call(
        _mix_c_kernel,
        out_shape=(
            jax.ShapeDtypeStruct((SEQ, 2 * C_WIDTH), BF16),
            jax.ShapeDtypeStruct((SEQ, 2 * C_WIDTH), BF16),
            jax.ShapeDtypeStruct((C_WIDTH, SEQ), BF16),
            jax.ShapeDtypeStruct((SEQ, HEAD), F32),
        ),
        grid=(SEQ // TM,),
        in_specs=[
            pl.BlockSpec((TM, D), lambda i: (i, 0)),
            pl.BlockSpec((1, D), row), pl.BlockSpec((1, D), row), pl.BlockSpec((1, D), row),
            pl.BlockSpec((None, D, wblk), lambda i: (layer, 0, 1), pipeline_mode=pl.Buffered(1)),
            pl.BlockSpec((None, D, wblk), lambda i: (layer, 0, 2), pipeline_mode=pl.Buffered(1)),
            pl.BlockSpec((D, HEAD), row),
            pl.BlockSpec((3 * HEAD, C_WIDTH), row),
            pl.BlockSpec((1, HEAD), row), pl.BlockSpec((1, HEAD), row),
            pl.BlockSpec((1, HEAD), row),
        ],
        out_specs=(tile, tile, pl.BlockSpec((C_WIDTH, TM), lambda i: (0, i)),
                   pl.BlockSpec((TM, HEAD), lambda i: (i, 0))),
        scratch_shapes=[pltpu.VMEM((1, HEAD), F32)],
        compiler_params=pltpu.CompilerParams(
            dimension_semantics=("arbitrary",), vmem_limit_bytes=VMEM_LIMIT),
        name="mix_c",
    )(x, gain, sc, sh, w_in_bf, w_in_bf, wf, place, qg, kg, bf)


def _attn_kernel(with_ada, first_ref, q_ref, k_ref, vt_ref, *refs):
    i = pl.program_id(1)
    step = pl.program_id(0) * (SEQ // TQ) + i
    first = first_ref[step]
    if with_ada:
        c_ref, wa_ref, ba_ref, o_ref, mod_ref, st_ref = refs

        @pl.when(step < ADA_BLOCKS)
        def _():
            _ada_kernel(c_ref, wa_ref, ba_ref, mod_ref)
    else:
        o_ref, st_ref = refs

    def scores_t(g, j):
        off = pl.multiple_of(j * TK, TK)
        cols = slice(2 * g * HEAD, 2 * (g + 1) * HEAD)
        return _nt_dot(k_ref[pl.ds(off, TK), cols], q_ref[:, cols])

    def update(g, read_st, j, m, l, acc):
        m_new = jnp.maximum(m, jnp.max(read_st(), axis=0, keepdims=True))
        alpha = jnp.exp2(m - m_new)
        p = jnp.exp2(read_st() - m_new)
        l_new = alpha * l + jnp.sum(p, axis=0, keepdims=True)
        off = pl.multiple_of(j * TK, TK)
        pv = jnp.dot(vt_ref[g * HEAD:(g + 1) * HEAD, pl.ds(off, TK)], p.astype(BF16),
                     preferred_element_type=F32)
        return m_new, l_new, alpha * acc + pv

    for g in range(ATTN_HEADS):
        st_ref[g] = scores_t(g, first)

    def body(j, carry):
        out = []
        for g in range(ATTN_HEADS):
            out.append(update(g, lambda g=g: st_ref[g], j, *carry[g]))
            st_ref[g] = scores_t(g, j + 1)
        return tuple(out)

    init = (jnp.full((1, TQ), -jnp.inf, F32), jnp.zeros((1, TQ), F32), jnp.zeros((HEAD, TQ), F32))
    carry = lax.fori_loop(first, i, body, (init,) * ATTN_HEADS)
    kpos = lax.broadcasted_iota(jnp.int32, (TK, TQ), 0)
    qpos = lax.broadcasted_iota(jnp.int32, (TK, TQ), 1)
    for g in range(ATTN_HEADS):
        m, l, acc = update(g, lambda g=g: jnp.where(kpos <= qpos, st_ref[g], -jnp.inf), i, *carry[g])
        o_ref[:, g * HEAD:(g + 1) * HEAD] = (acc / l).T.astype(BF16)


def _first_key_block(f2, qg, kg):
    nq = SEQ // TQ
    bound = LOG2E * HEAD ** 0.5 * jnp.max(jnp.abs(qg)) * jnp.max(jnp.abs(kg)) * 1.02
    f_start = f2[0::TQ, :C_HEADS]
    f_end = f2[TK - 1::TK, :C_HEADS]
    gap = f_end[None, :, :] - f_start[:, None, :]
    needed = gap <= 2.0 * bound + ATTN_DROP_LOG2
    needed = jnp.any(needed.reshape(nq, SEQ // TK, C_HEADS // ATTN_HEADS, ATTN_HEADS), axis=-1)
    first = jnp.argmax(needed, axis=1)
    return first.T.reshape(-1).astype(jnp.int32)


def _attn(first, qa, ka, vt, next_ada=None):
    gw = ATTN_HEADS * HEAD
    nq = SEQ // TQ
    in_specs = [
        pl.BlockSpec((TQ, 2 * gw), lambda h, i, f: (i, h)),
        pl.BlockSpec((SEQ, 2 * gw), lambda h, i, f: (0, h)),
        pl.BlockSpec((gw, SEQ), lambda h, i, f: (h, 0)),
    ]
    out_shape = jax.ShapeDtypeStruct((SEQ, C_WIDTH), BF16)
    out_specs = pl.BlockSpec((TQ, gw), lambda h, i, f: (i, h))
    args = (first, qa, ka, vt)
    if next_ada is not None:
        layer, c16, w_ada, b_ada3 = next_ada
        assert ADA_BLOCKS <= (C_HEADS // ATTN_HEADS) * nq
        blk = lambda h, i, f: jnp.minimum(h * nq + i, ADA_BLOCKS - 1)
        in_specs += [
            pl.BlockSpec((16, D), lambda h, i, f: (0, 0)),
            pl.BlockSpec((None, D, ADA_TN), lambda h, i, f: (layer, 0, blk(h, i, f))),
            pl.BlockSpec((None, 1, ADA_TN), lambda h, i, f: (layer, 0, blk(h, i, f))),
        ]
        out_shape = (out_shape, jax.ShapeDtypeStruct((1, 6 * D), F32))
        out_specs = (out_specs, pl.BlockSpec((1, ADA_TN), lambda h, i, f: (0, blk(h, i, f))))
        args += (c16, w_ada, b_ada3)
    return pl.pallas_call(
        functools.partial(_attn_kernel, next_ada is not None),
        out_shape=out_shape,
        grid_spec=pltpu.PrefetchScalarGridSpec(
            num_scalar_prefetch=1,
            grid=(C_HEADS // ATTN_HEADS, nq),
            in_specs=in_specs,
            out_specs=out_specs,
            scratch_shapes=[pltpu.VMEM((ATTN_HEADS, TK, TQ), F32)],
        ),
        compiler_params=pltpu.CompilerParams(
            dimension_semantics=("arbitrary", "arbitrary"), vmem_limit_bytes=VMEM_LIMIT),
        name="attn",
    )(*args)


def _first_max(vals):
    best, idx = vals[0], jnp.zeros(vals[0].shape, jnp.int32)
    for j in range(1, len(vals)):
        better = vals[j] > best
        idx = jnp.where(better, j, idx)
        best = jnp.where(better, vals[j], best)
    return idx, best


def _pick(idx, vals):
    out = vals[-1]
    for j in range(len(vals) - 2, -1, -1):
        out = jnp.where(idx == j, vals[j], out)
    return out


def _out_kernel(yab_ref, yc_ref, x_ref, w_ref, g1_ref, gain_ref, sc_ref, sh_ref, wr_ref, br_ref,
                x1_ref, h2_ref, info_ref, cnt_ref, carry_ref):
    i = pl.program_id(0)
    y = jnp.dot(yab_ref[...], w_ref[0:A_WIDTH + B_WIDTH, :], preferred_element_type=F32)
    y = y + jnp.dot(yc_ref[...], w_ref[A_WIDTH + B_WIDTH:, :], preferred_element_type=F32)
    x1 = x_ref[...] + g1_ref[...] * y
    x1_ref[...] = x1
    h2 = _rms_mod(x1, gain_ref[...], sc_ref[...], sh_ref[...])
    h2b = h2.astype(BF16)
    lo = pltpu.bitcast(h2b[:, :D // 2].astype(F32), jnp.uint32) >> 16
    hi = pltpu.bitcast(h2b[:, D // 2:].astype(F32), jnp.uint32) & jnp.uint32(0xFFFF0000)
    h2_ref[:, 0, :] = hi | lo

    logits = _nt_dot(wr_ref[...], h2b)
    scores = jax.nn.sigmoid(logits)
    sel = scores + br_ref[...]
    sel_r = [sel[k:k + 1, :] for k in range(N_EXPERTS)]
    sc_r = [scores[k:k + 1, :] for k in range(N_EXPERTS)]
    grp = []
    for g in range(N_GROUPS):
        a = sel_r[GROUP * g:GROUP * (g + 1)]
        pair = [a[p] + a[q] for p in range(GROUP) for q in range(p + 1, GROUP)]
        grp.append(functools.reduce(jnp.maximum, pair))
    gi, _ = _first_max(grp)
    cand = [_pick(gi, [sel_r[GROUP * g + j] for g in range(N_GROUPS)]) for j in range(GROUP)]
    cand_s = [_pick(gi, [sc_r[GROUP * g + j] for g in range(N_GROUPS)]) for j in range(GROUP)]
    i0, _ = _first_max(cand)
    i1, _ = _first_max([jnp.where(i0 == j, -jnp.inf, cand[j]) for j in range(GROUP)])
    s0, s1 = _pick(i0, cand_s), _pick(i1, cand_s)
    e0, e1 = GROUP * gi + i0, GROUP * gi + i1
    den = s0 + s1

    @pl.when(i == 0)
    def _():
        carry_ref[...] = jnp.zeros((N_EXPERTS, HEAD), F32)

    ek = lax.broadcasted_iota(jnp.int32, (N_EXPERTS, TM), 0)
    oh0 = (ek == e0).astype(F32)
    oh1 = (ek == e1).astype(F32)
    both = oh0 + oh1
    r = lax.broadcasted_iota(jnp.int32, (TM, TM), 0)
    c = lax.broadcasted_iota(jnp.int32, (TM, TM), 1)
    before = (r < c).astype(BF16)
    run = jnp.dot(both.astype(BF16), before, preferred_element_type=F32) + carry_ref[:, 0:1]
    rank0 = jnp.sum(run * oh0, axis=0, keepdims=True)
    rank1 = jnp.sum(run * oh1, axis=0, keepdims=True)
    total = carry_ref[...] + jnp.sum(both, axis=-1, keepdims=True)
    carry_ref[...] = total
    cnt_ref[...] = total

    info_ref[0:1, :] = e0.astype(F32)
    info_ref[1:2, :] = e1.astype(F32)
    info_ref[2:3, :] = s0 / den
    info_ref[3:4, :] = s1 / den
    info_ref[4:5, :] = rank0
    info_ref[5:6, :] = rank1
    info_ref[6:8, :] = jnp.zeros((2, TM), F32)


def _out_proj(layer, yab, yc, x, w_out, g1, gain, sc, sh, wr_t, br):
    row = lambda i: (0, 0)
    return pl.pallas_call(
        _out_kernel,
        out_shape=(
            jax.ShapeDtypeStruct((SEQ, D), F32),
            jax.ShapeDtypeStruct((SEQ, 1, D // 2), jnp.uint32),
            jax.ShapeDtypeStruct((8, SEQ), F32),
            jax.ShapeDtypeStruct((N_EXPERTS, HEAD), F32),
        ),
        grid=(SEQ // TM,),
        in_specs=[
            pl.BlockSpec((TM, A_WIDTH + B_WIDTH), lambda i: (i, 0)),
            pl.BlockSpec((TM, C_WIDTH), lambda i: (i, 0)),
            pl.BlockSpec((TM, D), lambda i: (i, 0)),
            pl.BlockSpec((None, D, D), lambda i: (layer, 0, 0), pipeline_mode=pl.Buffered(1)),
            pl.BlockSpec((1, D), row), pl.BlockSpec((1, D), row),
            pl.BlockSpec((1, D), row), pl.BlockSpec((1, D), row),
            pl.BlockSpec((N_EXPERTS, D), row),
            pl.BlockSpec((N_EXPERTS, 1), row),
        ],
        out_specs=(
            pl.BlockSpec((TM, D), lambda i: (i, 0)),
            pl.BlockSpec((TM, 1, D // 2), lambda i: (i, 0, 0)),
            pl.BlockSpec((8, TM), lambda i: (0, i)),
            pl.BlockSpec((N_EXPERTS, HEAD), row),
        ),
        scratch_shapes=[pltpu.VMEM((N_EXPERTS, HEAD), F32)],
        compiler_params=pltpu.CompilerParams(
            dimension_semantics=("arbitrary",), vmem_limit_bytes=VMEM_LIMIT),
        name="out_proj",
    )(yab, yc, x, w_out, g1, gain, sc, sh, wr_t, br)


def _moe_kernel(layer, dest_ref, bexp_ref, nvalid_ref, nused_ref, newexp_ref, nextexp_ref,
                h_hbm, wg_hbm, wu_hbm, wd_hbm, zeros_hbm, y_hbm,
                code_ref, xs_ref, ys_ref, xb_ref, hid_ref, wg32_ref, wu32_ref, wd32_ref,
                wg_ref, wu_ref, wd_ref, gsem, ssem, wsem):
    b = pl.program_id(0)
    n_used = nused_ref[0]
    slot = b % 2

    def weight_copies(e):
        return (pltpu.make_async_copy(wg_hbm.at[layer, e], wg32_ref, wsem.at[0]),
                pltpu.make_async_copy(wu_hbm.at[layer, e], wu32_ref, wsem.at[1]),
                pltpu.make_async_copy(wd_hbm.at[layer, e], wd32_ref, wsem.at[2]))

    def switch_expert():
        for cp in weight_copies(bexp_ref[b]):
            cp.wait()

        def cast(k, _):
            r = pl.multiple_of(k * 256, 256)
            wg_ref[pl.ds(r, 256), :] = wg32_ref[pl.ds(r, 256), :].astype(BF16)
            wu_ref[pl.ds(r, 256), :] = wu32_ref[pl.ds(r, 256), :].astype(BF16)
            r = pl.multiple_of(k * 128, 128)
            wd_ref[pl.ds(r, 128), :] = wd32_ref[pl.ds(r, 128), :].astype(BF16)
            return 0
        lax.fori_loop(0, D // 256, cast, 0)

        @pl.when(nextexp_ref[b] >= 0)
        def _():
            for cp in weight_copies(jnp.maximum(nextexp_ref[b], 0)):
                cp.start(priority=1)

    def gather_copy(blk, r, s):
        tok = code_ref[blk * MOE_BLK + r] & (SEQ - 1)
        return pltpu.make_async_copy(h_hbm.at[tok], xs_ref.at[s, pl.ds(r, 1), :],
                                     gsem.at[s])

    def scatter_copy(blk, r, s):
        dst = code_ref[blk * MOE_BLK + r]
        return pltpu.make_async_copy(ys_ref.at[s, pl.ds(r, 1), :], y_hbm.at[dst],
                                     ssem.at[s])

    def for_rows(n, fn):
        def body(r, _):
            fn(r)
            return 0
        lax.fori_loop(0, n, body, 0)

    def scatter_wait(n, s):
        n8 = pl.multiple_of((n >> 3) << 3, 8)

        @pl.when(n8 > 0)
        def _():
            pltpu.make_async_copy(ys_ref.at[s, pl.ds(0, n8), :], ys_ref.at[s, pl.ds(0, n8), :],
                                  ssem.at[s]).wait()

        for r in range(7):
            @pl.when(n8 + r < n)
            def _():
                pltpu.make_async_copy(ys_ref.at[s, pl.ds(r, 1), :], y_hbm.at[r],
                                      ssem.at[s]).wait()

    @pl.when(b == 0)
    def _():
        for cp in weight_copies(bexp_ref[0]):
            cp.start()

        fill = pltpu.make_async_copy(zeros_hbm, code_ref, wsem.at[3])
        fill.start()
        fill.wait()

        def place(a, _):
            code_ref[dest_ref[a]] = a
            return 0
        lax.fori_loop(0, N_ASSIGN, place, 0, unroll=8)
        for_rows(MOE_BLK, lambda r: gather_copy(0, r, 0).start())

    def step(prefetch):
        @pl.when(newexp_ref[b] == 1)
        def _():
            switch_expert()

        for r in range(MOE_BLK):
            gather_copy(b, r, slot).wait()

        packed = xs_ref[slot]
        xb_ref[:, :D // 2] = pltpu.bitcast(packed << 16, F32).astype(BF16)
        xb_ref[:, D // 2:] = pltpu.bitcast(packed & jnp.uint32(0xFFFF0000), F32).astype(BF16)
        prev = jnp.maximum(b - 1, 0)
        n_prev = jnp.where(b >= 1, nvalid_ref[prev], 0)
        x = xb_ref[...]
        rows_per = MOE_BLK // MOE_CHUNKS

        hc = D_EXPERT // MOE_CHUNKS
        for c in range(MOE_CHUNKS):
            cs = slice(c * hc, (c + 1) * hc)
            gate = jnp.dot(x, wg_ref[:, cs], preferred_element_type=F32)
            up = jnp.dot(x, wu_ref[:, cs], preferred_element_type=F32)
            hid = jax.nn.silu(gate) * up
            if prefetch:
                for r in range(c * rows_per, (c + 1) * rows_per):
                    gather_copy(b + 1, r, 1 - slot).start()
                tie = xs_ref[slot, 0:8, 0:hc]
                tie = pltpu.bitcast((tie >> 16) >> 16, F32)
                hid = jnp.concatenate([hid[0:8] + tie, hid[8:]], axis=0)
            hid_ref[:, cs] = hid.astype(BF16)

        scatter_wait(jnp.where(b >= 2, nvalid_ref[jnp.maximum(b - 2, 0)], 0), slot)
        hidb = hid_ref[...]
        oc = D // MOE_CHUNKS
        for c in range(MOE_CHUNKS):
            for r in range(c * rows_per, (c + 1) * rows_per):
                @pl.when(r < n_prev)
                def _():
                    scatter_copy(prev, r, 1 - slot).start(priority=1)
            cs = slice(c * oc, (c + 1) * oc)
            ys_ref[slot, :, cs] = jnp.dot(hidb, wd_ref[:, cs], preferred_element_type=F32)

    @pl.when(b < n_used - 1)
    def _():
        step(True)

    @pl.when(b == n_used - 1)
    def _():
        step(False)
        for_rows(nvalid_ref[b], lambda r: scatter_copy(b, r, slot).start())
        scatter_wait(nvalid_ref[b], slot)
        scatter_wait(jnp.where(b >= 1, nvalid_ref[jnp.maximum(b - 1, 0)], 0), 1 - slot)


def _moe(layer, dest, bexp, nvalid, nused, newexp, nextexp, h2, wg, wu, wd):
    hbm = pl.BlockSpec(memory_space=pl.ANY)
    return pl.pallas_call(
        functools.partial(_moe_kernel, layer),
        out_shape=jax.ShapeDtypeStruct((N_ASSIGN, 1, D), F32),
        grid_spec=pltpu.PrefetchScalarGridSpec(
            num_scalar_prefetch=6,
            grid=(MOE_NB,),
            in_specs=[hbm, hbm, hbm, hbm, hbm],
            out_specs=hbm,
            scratch_shapes=[
                pltpu.SMEM((MOE_ROWS,), jnp.int32),
                pltpu.VMEM((2, MOE_BLK, D // 2), jnp.uint32),
                pltpu.VMEM((2, MOE_BLK, D), F32),
                pltpu.VMEM((MOE_BLK, D), BF16),
                pltpu.VMEM((MOE_BLK, D_EXPERT), BF16),
                pltpu.VMEM((D, D_EXPERT), F32),
                pltpu.VMEM((D, D_EXPERT), F32),
                pltpu.VMEM((D_EXPERT, D), F32),
                pltpu.VMEM((D, D_EXPERT), BF16),
                pltpu.VMEM((D, D_EXPERT), BF16),
                pltpu.VMEM((D_EXPERT, D), BF16),
                pltpu.SemaphoreType.DMA((2,)),
                pltpu.SemaphoreType.DMA((2,)),
                pltpu.SemaphoreType.DMA((4,)),
            ],
        ),
        compiler_params=pltpu.CompilerParams(
            dimension_semantics=("arbitrary",), vmem_limit_bytes=VMEM_LIMIT),
        name="moe",
    )(dest, bexp, nvalid, nused, newexp, nextexp, h2, wg, wu, wd, jnp.zeros((MOE_ROWS,), jnp.int32))


def _combine_kernel(x1_ref, y0_ref, y1_ref, info_ref, g2_ref, o_ref):
    o_ref[...] = _moe_residual(x1_ref, y0_ref, y1_ref, info_ref, g2_ref)


def _combine(x1, y, info, g2):
    nt = SEQ // TM
    return pl.pallas_call(
        _combine_kernel,
        out_shape=jax.ShapeDtypeStruct((SEQ, D), F32),
        grid=(nt,),
        in_specs=[
            pl.BlockSpec((TM, D), lambda i: (i, 0)),
            pl.BlockSpec((TM, 1, D), lambda i: (i, 0, 0)),
            pl.BlockSpec((TM, 1, D), lambda i: (i + nt, 0, 0)),
            pl.BlockSpec((8, TM), lambda i: (0, i)),
            pl.BlockSpec((1, D), lambda i: (0, 0)),
        ],
        out_specs=pl.BlockSpec((TM, D), lambda i: (i, 0)),
        compiler_params=pltpu.CompilerParams(
            dimension_semantics=("arbitrary",), vmem_limit_bytes=VMEM_LIMIT),
        name="combine",
    )(x1, y, y, info, g2)


def _dispatch_plan(info, counts):
    cnt = counts[:, 0].astype(jnp.int32)
    padded = (cnt + MOE_BLK - 1) // MOE_BLK * MOE_BLK
    ends = jnp.cumsum(padded)
    pad_start = ends - padded
    e = info[0:2].astype(jnp.int32)
    rank = info[4:6].astype(jnp.int32)
    onehot = e[:, :, None] == jnp.arange(N_EXPERTS, dtype=jnp.int32)
    dest = jnp.sum(jnp.where(onehot, pad_start, 0), axis=-1) + rank
    blk_start = jnp.arange(MOE_NB, dtype=jnp.int32) * MOE_BLK
    bexp = jnp.sum(blk_start[:, None] >= ends[None, :], axis=-1)
    bexp = jnp.minimum(bexp, N_EXPERTS - 1).astype(jnp.int32)
    in_blk = jnp.arange(N_EXPERTS, dtype=jnp.int32)[None, :] == bexp[:, None]
    left = jnp.sum(jnp.where(in_blk, cnt + pad_start, 0), axis=-1) - blk_start
    nvalid = jnp.clip(left, 0, MOE_BLK).astype(jnp.int32)
    nused = (ends[-1:] // MOE_BLK).astype(jnp.int32)
    used = jnp.arange(MOE_NB, dtype=jnp.int32) < nused[0]
    newexp = jnp.concatenate([jnp.ones((1,), jnp.int32), (bexp[1:] != bexp[:-1]).astype(jnp.int32)])
    later = (bexp[None, :] > bexp[:, None]) & used[None, :]
    nextexp = jnp.min(jnp.where(later, bexp[None, :], N_EXPERTS), axis=1)
    nextexp = jnp.where(nextexp < N_EXPERTS, nextexp, -1).astype(jnp.int32)
    return dest.reshape(N_ASSIGN), bexp, nvalid, nused, newexp, nextexp


def kernel(x, c, w_ada, b_ada, g_mix, g_ffn, w_in, a_ws, a_bs, a_vg, b_w, b_scale,
           c_qg, c_kg, c_bf, w_out, w_router, b_router, e_gate, e_up, e_down):
    xs = x.reshape(SEQ, D)
    c16 = jnp.broadcast_to(c, (16, D))
    b_ada3 = b_ada.reshape(DEPTH, 1, 6 * D)
    mod = _ada(c16, w_ada, b_ada3)
    wr_t = w_router.T.astype(BF16)
    br = b_router.reshape(N_EXPERTS, 1).astype(F32)
    n_uvp = 2 * A_WIDTH + B_WIDTH
    place = np.zeros((3 * HEAD, C_WIDTH), np.float32)
    for pj in range(3):
        for ph in range(C_HEADS):
            place[pj * HEAD + ph, ph * HEAD + pj] = 1.0
    place = jnp.asarray(place, BF16)
    w_in_bf, w_out_bf = w_in.astype(BF16), w_out.astype(BF16)
    pending = None
    for l in range(DEPTH):
        sh1, sc1, g1, sh2, sc2, g2 = [mod[:, j * D:(j + 1) * D] for j in range(6)]
        gain1 = g_mix[l].reshape(1, D)
        wf = jnp.pad(w_in_bf[l, :, n_uvp + 3 * C_WIDTH:], ((0, 0), (0, HEAD - C_HEADS)))
        bf = jnp.pad(c_bf[l], (0, HEAD - C_HEADS)).reshape(1, HEAD)
        bsb = jnp.repeat(a_bs[l].T, HEAD, axis=1)
        yab = _mix_ab(l, xs, gain1, sc1, sh1, w_in_bf, a_ws[l], bsb, a_vg[l].reshape(1, A_WIDTH),
                      b_w[l], b_scale[l].reshape(1, B_WIDTH), moe=pending)
        if pending is not None:
            yab, xs = yab
        qa, ka, vt, f2 = _mix_c(l, xs, gain1, sc1, sh1, w_in_bf, wf, place, c_qg[l].reshape(1, HEAD),
                                c_kg[l].reshape(1, HEAD), bf)
        first = _first_key_block(f2, c_qg[l], c_kg[l])
        if l + 1 < DEPTH:
            yc, mod = _attn(first, qa, ka, vt, next_ada=(l + 1, c16, w_ada, b_ada3))
        else:
            yc = _attn(first, qa, ka, vt)
        x1, h2, info, counts = _out_proj(l, yab, yc, xs, w_out_bf, g1,
                                         g_ffn[l].reshape(1, D), sc2, sh2, wr_t, br)
        plan = _dispatch_plan(info, counts)
        y = _moe(l, *plan, h2, e_gate, e_up, e_down)
        xs, pending = x1, (y, info, g2)
    return _combine(xs, *pending).reshape(1, SEQ, D)
```

```python
import functools

import jax
import jax.numpy as jnp
import numpy as np
from jax import lax
from jax.experimental import pallas as pl
from jax.experimental.pallas import tpu as pltpu

F32 = jnp.float32
BF16 = jnp.bfloat16

D = 2048
SEQ = 8192
DEPTH = 2
CHUNK = 64
A_WIDTH = 512
A_HEADS = 4
A_BLOCK = 128
B_WIDTH = 512
POOL_WINDOWS = (2, 4, 8, 16)
POOL_HALO = 16
C_WIDTH = 1024
C_HEADS = 8
HEAD = 128
N_EXPERTS = 16
N_GROUPS = 4
GROUP = 4
D_EXPERT = 1024
EPS = 1e-6
LOG2E = 1.4426950408889634

TM = 512
TQ = 512
TK = 512
ATTN_HEADS = 2
ADA_TN = 256
ADA_BLOCKS = 6 * D // ADA_TN
ATTN_DROP_LOG2 = 64.0
MOE_BLK = 256
MOE_CHUNKS = 4
N_ASSIGN = 2 * SEQ
MOE_ROWS = N_ASSIGN + N_EXPERTS * MOE_BLK
MOE_NB = MOE_ROWS // MOE_BLK

VMEM_LIMIT = 56 * 1024 * 1024


def _nt_dot(a, b):
    return lax.dot_general(a, b, (((1,), (1,)), ((), ())), preferred_element_type=F32)


def _rms_mod(x, gain, scale, shift):
    ms = jnp.mean(x * x, axis=-1, keepdims=True)
    return (x * lax.rsqrt(ms + EPS) * gain) * (1.0 + scale) + shift


def _ada_kernel(c_ref, w_ref, b_ref, o_ref):
    ca = jax.nn.silu(c_ref[...]).astype(BF16)
    r = jnp.dot(ca, w_ref[...].astype(BF16), preferred_element_type=F32)
    o_ref[...] = r[0:1, :] + b_ref[...]


def _ada(c16, w_ada, b_ada3):
    tn = 1024
    return pl.pallas_call(
        _ada_kernel,
        out_shape=jax.ShapeDtypeStruct((1, 6 * D), F32),
        grid=(6 * D // tn,),
        in_specs=[
            pl.BlockSpec((16, D), lambda j: (0, 0)),
            pl.BlockSpec((None, D, tn), lambda j: (0, 0, j)),
            pl.BlockSpec((None, 1, tn), lambda j: (0, 0, j)),
        ],
        out_specs=pl.BlockSpec((1, tn), lambda j: (0, j)),
        compiler_params=pltpu.CompilerParams(
            dimension_semantics=("arbitrary",), vmem_limit_bytes=VMEM_LIMIT),
        name="ada",
    )(c16, w_ada, b_ada3)


def _moe_residual(x1_ref, y0_ref, y1_ref, info_ref, g2_ref):
    gw = info_ref[...].T
    moe = gw[:, 2:3] * y0_ref[:, 0, :] + gw[:, 3:4] * y1_ref[:, 0, :]
    return x1_ref[...] + g2_ref[...] * moe


def _mix_ab_kernel(after_moe, *refs):
    if after_moe:
        x1_ref, y0_ref, y1_ref, info_ref, g2_ref = refs[:5]
        (gain_ref, sc_ref, sh_ref, w_ref, ws_ref, bsb_ref, vg_ref, bw_ref, bscale_ref,
         y_ref, x_out_ref, pext_ref) = refs[5:]
        x = _moe_residual(x1_ref, y0_ref, y1_ref, info_ref, g2_ref)
        x_out_ref[...] = x
    else:
        (x_ref, gain_ref, sc_ref, sh_ref, w_ref, ws_ref, bsb_ref, vg_ref, bw_ref, bscale_ref,
         y_ref, pext_ref) = refs
        x = x_ref[...]
    i = pl.program_id(0)
    h = _rms_mod(x, gain_ref[...], sc_ref[...], sh_ref[...])
    z = jnp.dot(h.astype(BF16), w_ref[...], preferred_element_type=F32)
    u = jax.nn.gelu(z[:, :A_WIDTH])
    v = jax.nn.gelu(z[:, A_WIDTH:2 * A_WIDTH])
    p = z[:, 2 * A_WIDTH:]

    cid_t = lax.broadcasted_iota(jnp.int32, (A_BLOCK, A_BLOCK), 0) // CHUNK
    cid_s = lax.broadcasted_iota(jnp.int32, (A_BLOCK, A_BLOCK), 1) // CHUNK
    mask = cid_s <= cid_t
    for hh in range(A_HEADS):
        cs = slice(hh * HEAD, (hh + 1) * HEAD)
        vh = v[:, cs]
        vn = vh * lax.rsqrt(jnp.mean(vh * vh, axis=-1, keepdims=True) + EPS) * vg_ref[:, cs]
        vnb = vn.astype(BF16)
        wm = jnp.where(mask, ws_ref[hh], 0.0).astype(BF16)
        for n in range(TM // A_BLOCK):
            rs = slice(n * A_BLOCK, (n + 1) * A_BLOCK)
            sp = jnp.dot(wm, vnb[rs], preferred_element_type=F32) + bsb_ref[:, cs]
            y_ref[rs, cs] = (u[rs, cs] * sp).astype(BF16)

    @pl.when(i == 0)
    def _():
        pext_ref[0:POOL_HALO, :] = jnp.zeros((POOL_HALO, B_WIDTH), F32)

    pext_ref[POOL_HALO:POOL_HALO + TM, :] = p
    t1 = i * TM + lax.broadcasted_iota(jnp.int32, (TM, 1), 0) + 1
    for g, w in enumerate(POOL_WINDOWS):
        cs = slice(g * HEAD, (g + 1) * HEAD)
        acc = p[:, cs]
        for j in range(1, w):
            acc = acc + pext_ref[POOL_HALO - j:POOL_HALO - j + TM, cs]
        cnt = jnp.minimum(t1, w).astype(F32)
        d = acc / cnt - p[:, cs]
        yb = jnp.dot(d.astype(BF16), bw_ref[g].astype(BF16), preferred_element_type=F32)
        y_ref[:, A_WIDTH + g * HEAD:A_WIDTH + (g + 1) * HEAD] = (yb * bscale_ref[:, cs]).astype(BF16)
    pext_ref[0:POOL_HALO, :] = pext_ref[TM:TM + POOL_HALO, :]


def _mix_ab(layer, x, gain, sc, sh, w_in_bf, a_ws, bsb, vg, b_w, bscale, moe=None):
    row = lambda i: (0, 0)
    nt = SEQ // TM
    tile = pl.BlockSpec((TM, D), lambda i: (i, 0))
    y_spec = pl.BlockSpec((TM, A_WIDTH + B_WIDTH), lambda i: (i, 0))
    y_shape = jax.ShapeDtypeStruct((SEQ, A_WIDTH + B_WIDTH), BF16)
    if moe is None:
        lead_specs, lead_args = [tile], (x,)
        out_shape, out_specs = y_shape, y_spec
    else:
        y, info, g2 = moe
        lead_specs = [tile,
                      pl.BlockSpec((TM, 1, D), lambda i: (i, 0, 0)),
                      pl.BlockSpec((TM, 1, D), lambda i: (i + nt, 0, 0)),
                      pl.BlockSpec((8, TM), lambda i: (0, i)),
                      pl.BlockSpec((1, D), row)]
        lead_args = (x, y, y, info, g2)
        out_shape, out_specs = (y_shape, jax.ShapeDtypeStruct((SEQ, D), F32)), (y_spec, tile)
    return pl.pallas_call(
        functools.partial(_mix_ab_kernel, moe is not None),
        out_shape=out_shape,
        grid=(nt,),
        in_specs=lead_specs + [
            pl.BlockSpec((1, D), row), pl.BlockSpec((1, D), row), pl.BlockSpec((1, D), row),
            pl.BlockSpec((None, D, 2 * A_WIDTH + B_WIDTH), lambda i: (layer, 0, 0),
                         pipeline_mode=pl.Buffered(1)),
            pl.BlockSpec((A_HEADS, A_BLOCK, A_BLOCK), lambda i: (0, 0, 0)),
            pl.BlockSpec((A_BLOCK, A_WIDTH), row),
            pl.BlockSpec((1, A_WIDTH), row),
            pl.BlockSpec((len(POOL_WINDOWS), HEAD, HEAD), lambda i: (0, 0, 0)),
            pl.BlockSpec((1, B_WIDTH), row),
        ],
        out_specs=out_specs,
        scratch_shapes=[pltpu.VMEM((TM + POOL_HALO, B_WIDTH), F32)],
        compiler_params=pltpu.CompilerParams(
            dimension_semantics=("arbitrary",), vmem_limit_bytes=VMEM_LIMIT),
        name="mix_ab",
    )(*lead_args, gain, sc, sh, w_in_bf, a_ws, bsb, vg, b_w, bscale)


def _split3(x):
    hi = x.astype(BF16)
    rest = x - hi.astype(F32)
    mid = rest.astype(BF16)
    lo = (rest - mid.astype(F32)).astype(BF16)
    return jnp.concatenate([hi, mid, lo], axis=1)


def _mix_c_kernel(x_ref, gain_ref, sc_ref, sh_ref, w1_ref, w2_ref, wf_ref, place_ref, qg_ref, kg_ref,
                  bf_ref, q_ref, k_ref, vt_ref, f2_ref, carry_ref):
    i = pl.program_id(0)
    h = _rms_mod(x_ref[...], gain_ref[...], sc_ref[...], sh_ref[...])
    hb = h.astype(BF16)
    z1 = jnp.dot(hb, w1_ref[...], preferred_element_type=F32)
    half = C_WIDTH // 2
    z2 = jnp.dot(hb, w2_ref[:, :half], preferred_element_type=F32)
    lane = lax.broadcasted_iota(jnp.int32, (TM, HEAD), 1)
    q_aug = jnp.where(lane < 3, -1.0, 0.0).astype(BF16)
    qscale = HEAD ** -0.5 * LOG2E
    for hh in range(C_HEADS):
        qh = z1[:, hh * HEAD:(hh + 1) * HEAD]
        qn = qh * lax.rsqrt(jnp.mean(qh * qh, axis=-1, keepdims=True) + EPS) * qg_ref[...]
        q_ref[:, 2 * hh * HEAD:(2 * hh + 1) * HEAD] = (qn * qscale).astype(BF16)
        q_ref[:, (2 * hh + 1) * HEAD:(2 * hh + 2) * HEAD] = q_aug
        if hh < C_HEADS // 2:
            kh = z1[:, C_WIDTH + hh * HEAD:C_WIDTH + (hh + 1) * HEAD]
        else:
            kh = z2[:, hh * HEAD - half:(hh + 1) * HEAD - half]
        kn = kh * lax.rsqrt(jnp.mean(kh * kh, axis=-1, keepdims=True) + EPS) * kg_ref[...]
        k_ref[:, 2 * hh * HEAD:(2 * hh + 1) * HEAD] = kn.astype(BF16)
    vt = lax.dot_general(w2_ref[:, half:], hb, (((0,), (1,)), ((), ())), preferred_element_type=F32)
    vt_ref[...] = vt.astype(BF16)

    @pl.when(i == 0)
    def _():
        carry_ref[...] = jnp.zeros((1, HEAD), F32)

    fz = jnp.dot(hb, wf_ref[...], preferred_element_type=F32)
    logf = jax.nn.log_sigmoid(fz + bf_ref[...])
    r = lax.broadcasted_iota(jnp.int32, (TM, TM), 0)
    c = lax.broadcasted_iota(jnp.int32, (TM, TM), 1)
    lower = jnp.where(c <= r, 1.0, 0.0).astype(BF16)
    cum = jnp.dot(lower, _split3(logf), preferred_element_type=F32)
    f_cum = (cum[:, :HEAD] + cum[:, HEAD:2 * HEAD]) + cum[:, 2 * HEAD:] + carry_ref[...]
    carry_ref[...] = f_cum[TM - 1:TM, :]
    f2 = f_cum * LOG2E
    f2_ref[...] = f2
    aug = jnp.dot(_split3(f2), place_ref[...], preferred_element_type=F32)
    for hh in range(C_HEADS):
        k_ref[:, (2 * hh + 1) * HEAD:(2 * hh + 2) * HEAD] = aug[:, hh * HEAD:(hh + 1) * HEAD].astype(BF16)


def _mix_c(layer, x, gain, sc, sh, w_in_bf, wf, place, qg, kg, bf):
    row = lambda i: (0, 0)
    wblk = 2 * A_WIDTH + B_WIDTH
    tile = pl.BlockSpec((TM, 2 * C_WIDTH), lambda i: (i, 0))
    return pl.pallas_call(
        _mix_c_kernel,
        out_shape=(
            jax.ShapeDtypeStruct((SEQ, 2 * C_WIDTH), BF16),
            jax.ShapeDtypeStruct((SEQ, 2 * C_WIDTH), BF16),
            jax.ShapeDtypeStruct((C_WIDTH, SEQ), BF16),
            jax.ShapeDtypeStruct((SEQ, HEAD), F32),
        ),
        grid=(SEQ // TM,),
        in_specs=[
            pl.BlockSpec((TM, D), lambda i: (i, 0)),
            pl.BlockSpec((1, D), row), pl.BlockSpec((1, D), row), pl.BlockSpec((1, D), row),
            pl.BlockSpec((None, D, wblk), lambda i: (layer, 0, 1), pipeline_mode=pl.Buffered(1)),
            pl.BlockSpec((None, D, wblk), lambda i: (layer, 0, 2), pipeline_mode=pl.Buffered(1)),
            pl.BlockSpec((D, HEAD), row),
            pl.BlockSpec((3 * HEAD, C_WIDTH), row),
            pl.BlockSpec((1, HEAD), row), pl.BlockSpec((1, HEAD), row),
            pl.BlockSpec((1, HEAD), row),
        ],
        out_specs=(tile, tile, pl.BlockSpec((C_WIDTH, TM), lambda i: (0, i)),
                   pl.BlockSpec((TM, HEAD), lambda i: (i, 0))),
        scratch_shapes=[pltpu.VMEM((1, HEAD), F32)],
        compiler_params=pltpu.CompilerParams(
            dimension_semantics=("arbitrary",), vmem_limit_bytes=VMEM_LIMIT),
        name="mix_c",
    )(x, gain, sc, sh, w_in_bf, w_in_bf, wf, place, qg, kg, bf)


def _attn_kernel(with_ada, first_ref, q_ref, k_ref, vt_ref, *refs):
    i = pl.program_id(1)
    step = pl.program_id(0) * (SEQ // TQ) + i
    first = first_ref[step]
    if with_ada:
        c_ref, wa_ref, ba_ref, o_ref, mod_ref, st_ref = refs

        @pl.when(step < ADA_BLOCKS)
        def _():
            _ada_kernel(c_ref, wa_ref, ba_ref, mod_ref)
    else:
        o_ref, st_ref = refs

    def scores_t(g, j):
        off = pl.multiple_of(j * TK, TK)
        cols = slice(2 * g * HEAD, 2 * (g + 1) * HEAD)
        return _nt_dot(k_ref[pl.ds(off, TK), cols], q_ref[:, cols])

    def update(g, read_st, j, m, l, acc):
        m_new = jnp.maximum(m, jnp.max(read_st(), axis=0, keepdims=True))
        alpha = jnp.exp2(m - m_new)
        p = jnp.exp2(read_st() - m_new)
        l_new = alpha * l + jnp.sum(p, axis=0, keepdims=True)
        off = pl.multiple_of(j * TK, TK)
        pv = jnp.dot(vt_ref[g * HEAD:(g + 1) * HEAD, pl.ds(off, TK)], p.astype(BF16),
                     preferred_element_type=F32)
        return m_new, l_new, alpha * acc + pv

    for g in range(ATTN_HEADS):
        st_ref[g] = scores_t(g, first)

    def body(j, carry):
        out = []
        for g in range(ATTN_HEADS):
            out.append(update(g, lambda g=g: st_ref[g], j, *carry[g]))
            st_ref[g] = scores_t(g, j + 1)
        return tuple(out)

    init = (jnp.full((1, TQ), -jnp.inf, F32), jnp.zeros((1, TQ), F32), jnp.zeros((HEAD, TQ), F32))
    pairs = (i - first) // 2
    carry = lax.fori_loop(0, pairs, lambda t, c: body(first + 2 * t + 1, body(first + 2 * t, c)),
                          (init,) * ATTN_HEADS)
    carry = lax.cond((i - first) % 2 == 1, lambda c: body(i - 1, c), lambda c: c, carry)
    kpos = lax.broadcasted_iota(jnp.int32, (TK, TQ), 0)
    qpos = lax.broadcasted_iota(jnp.int32, (TK, TQ), 1)
    for g in range(ATTN_HEADS):
        m, l, acc = update(g, lambda g=g: jnp.where(kpos <= qpos, st_ref[g], -jnp.inf), i, *carry[g])
        o_ref[:, g * HEAD:(g + 1) * HEAD] = (acc / l).T.astype(BF16)


def _first_key_block(f2, qg, kg):
    nq = SEQ // TQ
    bound = LOG2E * HEAD ** 0.5 * jnp.max(jnp.abs(qg)) * jnp.max(jnp.abs(kg)) * 1.02
    f_start = f2[0::TQ, :C_HEADS]
    f_end = f2[TK - 1::TK, :C_HEADS]
    gap = f_end[None, :, :] - f_start[:, None, :]
    needed = gap <= 2.0 * bound + ATTN_DROP_LOG2
    needed = jnp.any(needed.reshape(nq, SEQ // TK, C_HEADS // ATTN_HEADS, ATTN_HEADS), axis=-1)
    first = jnp.argmax(needed, axis=1)
    return first.T.reshape(-1).astype(jnp.int32)


def _attn(first, qa, ka, vt, next_ada=None):
    gw = ATTN_HEADS * HEAD
    nq = SEQ // TQ
    in_specs = [
        pl.BlockSpec((TQ, 2 * gw), lambda h, i, f: (i, h)),
        pl.BlockSpec((SEQ, 2 * gw), lambda h, i, f: (0, h)),
        pl.BlockSpec((gw, SEQ), lambda h, i, f: (h, 0)),
    ]
    out_shape = jax.ShapeDtypeStruct((SEQ, C_WIDTH), BF16)
    out_specs = pl.BlockSpec((TQ, gw), lambda h, i, f: (i, h))
    args = (first, qa, ka, vt)
    if next_ada is not None:
        layer, c16, w_ada, b_ada3 = next_ada
        assert ADA_BLOCKS <= (C_HEADS // ATTN_HEADS) * nq
        blk = lambda h, i, f: jnp.minimum(h * nq + i, ADA_BLOCKS - 1)
        in_specs += [
            pl.BlockSpec((16, D), lambda h, i, f: (0, 0)),
            pl.BlockSpec((None, D, ADA_TN), lambda h, i, f: (layer, 0, blk(h, i, f))),
            pl.BlockSpec((None, 1, ADA_TN), lambda h, i, f: (layer, 0, blk(h, i, f))),
        ]
        out_shape = (out_shape, jax.ShapeDtypeStruct((1, 6 * D), F32))
        out_specs = (out_specs, pl.BlockSpec((1, ADA_TN), lambda h, i, f: (0, blk(h, i, f))))
        args += (c16, w_ada, b_ada3)
    return pl.pallas_call(
        functools.partial(_attn_kernel, next_ada is not None),
        out_shape=out_shape,
        grid_spec=pltpu.PrefetchScalarGridSpec(
            num_scalar_prefetch=1,
            grid=(C_HEADS // ATTN_HEADS, nq),
            in_specs=in_specs,
            out_specs=out_specs,
            scratch_shapes=[pltpu.VMEM((ATTN_HEADS, TK, TQ), F32)],
        ),
        compiler_params=pltpu.CompilerParams(
            dimension_semantics=("arbitrary", "arbitrary"), vmem_limit_bytes=VMEM_LIMIT),
        name="attn",
    )(*args)


def _first_max(vals):
    best, idx = vals[0], jnp.zeros(vals[0].shape, jnp.int32)
    for j in range(1, len(vals)):
        better = vals[j] > best
        idx = jnp.where(better, j, idx)
        best = jnp.where(better, vals[j], best)
    return idx, best


def _pick(idx, vals):
    out = vals[-1]
    for j in range(len(vals) - 2, -1, -1):
        out = jnp.where(idx == j, vals[j], out)
    return out


def _out_kernel(yab_ref, yc_ref, x_ref, w_ref, g1_ref, gain_ref, sc_ref, sh_ref, wr_ref, br_ref,
                x1_ref, h2_ref, info_ref, cnt_ref, carry_ref):
    i = pl.program_id(0)
    y = jnp.dot(yab_ref[...], w_ref[0:A_WIDTH + B_WIDTH, :], preferred_element_type=F32)
    y = y + jnp.dot(yc_ref[...], w_ref[A_WIDTH + B_WIDTH:, :], preferred_element_type=F32)
    x1 = x_ref[...] + g1_ref[...] * y
    x1_ref[...] = x1
    h2 = _rms_mod(x1, gain_ref[...], sc_ref[...], sh_ref[...])
    h2b = h2.astype(BF16)
    lo = pltpu.bitcast(h2b[:, :D // 2].astype(F32), jnp.uint32) >> 16
    hi = pltpu.bitcast(h2b[:, D // 2:].astype(F32), jnp.uint32) & jnp.uint32(0xFFFF0000)
    h2_ref[:, 0, :] = hi | lo

    logits = _nt_dot(wr_ref[...], h2b)
    scores = jax.nn.sigmoid(logits)
    sel = scores + br_ref[...]
    sel_r = [sel[k:k + 1, :] for k in range(N_EXPERTS)]
    sc_r = [scores[k:k + 1, :] for k in range(N_EXPERTS)]
    grp = []
    for g in range(N_GROUPS):
        a = sel_r[GROUP * g:GROUP * (g + 1)]
        pair = [a[p] + a[q] for p in range(GROUP) for q in range(p + 1, GROUP)]
        grp.append(functools.reduce(jnp.maximum, pair))
    gi, _ = _first_max(grp)
    cand = [_pick(gi, [sel_r[GROUP * g + j] for g in range(N_GROUPS)]) for j in range(GROUP)]
    cand_s = [_pick(gi, [sc_r[GROUP * g + j] for g in range(N_GROUPS)]) for j in range(GROUP)]
    i0, _ = _first_max(cand)
    i1, _ = _first_max([jnp.where(i0 == j, -jnp.inf, cand[j]) for j in range(GROUP)])
    s0, s1 = _pick(i0, cand_s), _pick(i1, cand_s)
    e0, e1 = GROUP * gi + i0, GROUP * gi + i1
    den = s0 + s1

    @pl.when(i == 0)
    def _():
        carry_ref[...] = jnp.zeros((N_EXPERTS, HEAD), F32)

    ek = lax.broadcasted_iota(jnp.int32, (N_EXPERTS, TM), 0)
    oh0 = (ek == e0).astype(F32)
    oh1 = (ek == e1).astype(F32)
    both = oh0 + oh1
    r = lax.broadcasted_iota(jnp.int32, (TM, TM), 0)
    c = lax.broadcasted_iota(jnp.int32, (TM, TM), 1)
    before = (r < c).astype(BF16)
    run = jnp.dot(both.astype(BF16), before, preferred_element_type=F32) + carry_ref[:, 0:1]
    rank0 = jnp.sum(run * oh0, axis=0, keepdims=True)
    rank1 = jnp.sum(run * oh1, axis=0, keepdims=True)
    total = carry_ref[...] + jnp.sum(both, axis=-1, keepdims=True)
    carry_ref[...] = total
    cnt_ref[...] = total

    info_ref[0:1, :] = e0.astype(F32)
    info_ref[1:2, :] = e1.astype(F32)
    info_ref[2:3, :] = s0 / den
    info_ref[3:4, :] = s1 / den
    info_ref[4:5, :] = rank0
    info_ref[5:6, :] = rank1
    info_ref[6:8, :] = jnp.zeros((2, TM), F32)


def _out_proj(layer, yab, yc, x, w_out, g1, gain, sc, sh, wr_t, br):
    row = lambda i: (0, 0)
    return pl.pallas_call(
        _out_kernel,
        out_shape=(
            jax.ShapeDtypeStruct((SEQ, D), F32),
            jax.ShapeDtypeStruct((SEQ, 1, D // 2), jnp.uint32),
            jax.ShapeDtypeStruct((8, SEQ), F32),
            jax.ShapeDtypeStruct((N_EXPERTS, HEAD), F32),
        ),
        grid=(SEQ // TM,),
        in_specs=[
            pl.BlockSpec((TM, A_WIDTH + B_WIDTH), lambda i: (i, 0)),
            pl.BlockSpec((TM, C_WIDTH), lambda i: (i, 0)),
            pl.BlockSpec((TM, D), lambda i: (i, 0)),
            pl.BlockSpec((None, D, D), lambda i: (layer, 0, 0), pipeline_mode=pl.Buffered(1)),
            pl.BlockSpec((1, D), row), pl.BlockSpec((1, D), row),
            pl.BlockSpec((1, D), row), pl.BlockSpec((1, D), row),
            pl.BlockSpec((N_EXPERTS, D), row),
            pl.BlockSpec((N_EXPERTS, 1), row),
        ],
        out_specs=(
            pl.BlockSpec((TM, D), lambda i: (i, 0)),
            pl.BlockSpec((TM, 1, D // 2), lambda i: (i, 0, 0)),
            pl.BlockSpec((8, TM), lambda i: (0, i)),
            pl.BlockSpec((N_EXPERTS, HEAD), row),
        ),
        scratch_shapes=[pltpu.VMEM((N_EXPERTS, HEAD), F32)],
        compiler_params=pltpu.CompilerParams(
            dimension_semantics=("arbitrary",), vmem_limit_bytes=VMEM_LIMIT),
        name="out_proj",
    )(yab, yc, x, w_out, g1, gain, sc, sh, wr_t, br)


def _moe_kernel(layer, dest_ref, bexp_ref, nvalid_ref, nused_ref, newexp_ref, nextexp_ref,
                h_hbm, wg_hbm, wu_hbm, wd_hbm, zeros_hbm, y_hbm,
                code_ref, xs_ref, ys_ref, xb_ref, hid_ref, wg32_ref, wu32_ref, wd32_ref,
                wg_ref, wu_ref, wd_ref, gsem, ssem, wsem):
    b = pl.program_id(0)
    n_used = nused_ref[0]
    slot = b % 2

    def weight_copies(e):
        return (pltpu.make_async_copy(wg_hbm.at[layer, e], wg32_ref, wsem.at[0]),
                pltpu.make_async_copy(wu_hbm.at[layer, e], wu32_ref, wsem.at[1]),
                pltpu.make_async_copy(wd_hbm.at[layer, e], wd32_ref, wsem.at[2]))

    def switch_expert():
        for cp in weight_copies(bexp_ref[b]):
            cp.wait()

        def cast(k, _):
            r = pl.multiple_of(k * 256, 256)
            wg_ref[pl.ds(r, 256), :] = wg32_ref[pl.ds(r, 256), :].astype(BF16)
            wu_ref[pl.ds(r, 256), :] = wu32_ref[pl.ds(r, 256), :].astype(BF16)
            r = pl.multiple_of(k * 128, 128)
            wd_ref[pl.ds(r, 128), :] = wd32_ref[pl.ds(r, 128), :].astype(BF16)
            return 0
        lax.fori_loop(0, D // 256, cast, 0)

        @pl.when(nextexp_ref[b] >= 0)
        def _():
            for cp in weight_copies(jnp.maximum(nextexp_ref[b], 0)):
                cp.start(priority=1)

    def gather_copy(blk, r, s):
        tok = code_ref[blk * MOE_BLK + r] & (SEQ - 1)
        return pltpu.make_async_copy(h_hbm.at[tok], xs_ref.at[s, pl.ds(r, 1), :],
                                     gsem.at[s])

    def scatter_copy(blk, r, s):
        dst = code_ref[blk * MOE_BLK + r]
        return pltpu.make_async_copy(ys_ref.at[s, pl.ds(r, 1), :], y_hbm.at[dst],
                                     ssem.at[s])

    def for_rows(n, fn):
        def body(r, _):
            fn(r)
            return 0
        lax.fori_loop(0, n, body, 0)

    def scatter_wait(n, s):
        n8 = pl.multiple_of((n >> 3) << 3, 8)

        @pl.when(n8 > 0)
        def _():
            pltpu.make_async_copy(ys_ref.at[s, pl.ds(0, n8), :], ys_ref.at[s, pl.ds(0, n8), :],
                                  ssem.at[s]).wait()

        for r in range(7):
            @pl.when(n8 + r < n)
            def _():
                pltpu.make_async_copy(ys_ref.at[s, pl.ds(r, 1), :], y_hbm.at[r],
                                      ssem.at[s]).wait()

    @pl.when(b == 0)
    def _():
        for cp in weight_copies(bexp_ref[0]):
            cp.start()

        fill = pltpu.make_async_copy(zeros_hbm, code_ref, wsem.at[3])
        fill.start()
        fill.wait()

        def place(a, _):
            code_ref[dest_ref[a]] = a
            return 0
        lax.fori_loop(0, N_ASSIGN, place, 0, unroll=8)
        for_rows(MOE_BLK, lambda r: gather_copy(0, r, 0).start())

    def step(prefetch):
        @pl.when(newexp_ref[b] == 1)
        def _():
            switch_expert()

        for r in range(MOE_BLK):
            gather_copy(b, r, slot).wait()

        packed = xs_ref[slot]
        xb_ref[:, :D // 2] = pltpu.bitcast(packed << 16, F32).astype(BF16)
        xb_ref[:, D // 2:] = pltpu.bitcast(packed & jnp.uint32(0xFFFF0000), F32).astype(BF16)
        prev = jnp.maximum(b - 1, 0)
        n_prev = jnp.where(b >= 1, nvalid_ref[prev], 0)
        x = xb_ref[...]
        rows_per = MOE_BLK // MOE_CHUNKS

        hc = D_EXPERT // MOE_CHUNKS
        for c in range(MOE_CHUNKS):
            cs = slice(c * hc, (c + 1) * hc)
            gate = jnp.dot(x, wg_ref[:, cs], preferred_element_type=F32)
            up = jnp.dot(x, wu_ref[:, cs], preferred_element_type=F32)
            hid = jax.nn.silu(gate) * up
            if prefetch:
                for r in range(c * rows_per, (c + 1) * rows_per):
                    gather_copy(b + 1, r, 1 - slot).start()
                tie = xs_ref[slot, 0:8, 0:hc]
                tie = pltpu.bitcast((tie >> 16) >> 16, F32)
                hid = jnp.concatenate([hid[0:8] + tie, hid[8:]], axis=0)
            hid_ref[:, cs] = hid.astype(BF16)

        scatter_wait(jnp.where(b >= 2, nvalid_ref[jnp.maximum(b - 2, 0)], 0), slot)
        hidb = hid_ref[...]
        oc = D // MOE_CHUNKS
        for c in range(MOE_CHUNKS):
            for r in range(c * rows_per, (c + 1) * rows_per):
                @pl.when(r < n_prev)
                def _():
                    scatter_copy(prev, r, 1 - slot).start(priority=1)
            cs = slice(c * oc, (c + 1) * oc)
            ys_ref[slot, :, cs] = jnp.dot(hidb, wd_ref[:, cs], preferred_element_type=F32)

    @pl.when(b < n_used - 1)
    def _():
        step(True)

    @pl.when(b == n_used - 1)
    def _():
        step(False)
        for_rows(nvalid_ref[b], lambda r: scatter_copy(b, r, slot).start())
        scatter_wait(nvalid_ref[b], slot)
        scatter_wait(jnp.where(b >= 1, nvalid_ref[jnp.maximum(b - 1, 0)], 0), 1 - slot)


def _moe(layer, dest, bexp, nvalid, nused, newexp, nextexp, h2, wg, wu, wd):
    hbm = pl.BlockSpec(memory_space=pl.ANY)
    return pl.pallas_call(
        functools.partial(_moe_kernel, layer),
        out_shape=jax.ShapeDtypeStruct((N_ASSIGN, 1, D), F32),
        grid_spec=pltpu.PrefetchScalarGridSpec(
            num_scalar_prefetch=6,
            grid=(MOE_NB,),
            in_specs=[hbm, hbm, hbm, hbm, hbm],
            out_specs=hbm,
            scratch_shapes=[
                pltpu.SMEM((MOE_ROWS,), jnp.int32),
                pltpu.VMEM((2, MOE_BLK, D // 2), jnp.uint32),
                pltpu.VMEM((2, MOE_BLK, D), F32),
                pltpu.VMEM((MOE_BLK, D), BF16),
                pltpu.VMEM((MOE_BLK, D_EXPERT), BF16),
                pltpu.VMEM((D, D_EXPERT), F32),
                pltpu.VMEM((D, D_EXPERT), F32),
                pltpu.VMEM((D_EXPERT, D), F32),
                pltpu.VMEM((D, D_EXPERT), BF16),
                pltpu.VMEM((D, D_EXPERT), BF16),
                pltpu.VMEM((D_EXPERT, D), BF16),
                pltpu.SemaphoreType.DMA((2,)),
                pltpu.SemaphoreType.DMA((2,)),
                pltpu.SemaphoreType.DMA((4,)),
            ],
        ),
        compiler_params=pltpu.CompilerParams(
            dimension_semantics=("arbitrary",), vmem_limit_bytes=VMEM_LIMIT),
        name="moe",
    )(dest, bexp, nvalid, nused, newexp, nextexp, h2, wg, wu, wd, jnp.zeros((MOE_ROWS,), jnp.int32))


def _combine_kernel(x1_ref, y0_ref, y1_ref, info_ref, g2_ref, o_ref):
    o_ref[...] = _moe_residual(x1_ref, y0_ref, y1_ref, info_ref, g2_ref)


def _combine(x1, y, info, g2):
    nt = SEQ // TM
    return pl.pallas_call(
        _combine_kernel,
        out_shape=jax.ShapeDtypeStruct((SEQ, D), F32),
        grid=(nt,),
        in_specs=[
            pl.BlockSpec((TM, D), lambda i: (i, 0)),
            pl.BlockSpec((TM, 1, D), lambda i: (i, 0, 0)),
            pl.BlockSpec((TM, 1, D), lambda i: (i + nt, 0, 0)),
            pl.BlockSpec((8, TM), lambda i: (0, i)),
            pl.BlockSpec((1, D), lambda i: (0, 0)),
        ],
        out_specs=pl.BlockSpec((TM, D), lambda i: (i, 0)),
        compiler_params=pltpu.CompilerParams(
            dimension_semantics=("arbitrary",), vmem_limit_bytes=VMEM_LIMIT),
        name="combine",
    )(x1, y, y, info, g2)


def _dispatch_plan(info, counts):
    cnt = counts[:, 0].astype(jnp.int32)
    padded = (cnt + MOE_BLK - 1) // MOE_BLK * MOE_BLK
    ends = jnp.cumsum(padded)
    pad_start = ends - padded
    e = info[0:2].astype(jnp.int32)
    rank = info[4:6].astype(jnp.int32)
    onehot = e[:, :, None] == jnp.arange(N_EXPERTS, dtype=jnp.int32)
    dest = jnp.sum(jnp.where(onehot, pad_start, 0), axis=-1) + rank
    blk_start = jnp.arange(MOE_NB, dtype=jnp.int32) * MOE_BLK
    bexp = jnp.sum(blk_start[:, None] >= ends[None, :], axis=-1)
    bexp = jnp.minimum(bexp, N_EXPERTS - 1).astype(jnp.int32)
    in_blk = jnp.arange(N_EXPERTS, dtype=jnp.int32)[None, :] == bexp[:, None]
    left = jnp.sum(jnp.where(in_blk, cnt + pad_start, 0), axis=-1) - blk_start
    nvalid = jnp.clip(left, 0, MOE_BLK).astype(jnp.int32)
    nused = (ends[-1:] // MOE_BLK).astype(jnp.int32)
    used = jnp.arange(MOE_NB, dtype=jnp.int32) < nused[0]
    newexp = jnp.concatenate([jnp.ones((1,), jnp.int32), (bexp[1:] != bexp[:-1]).astype(jnp.int32)])
    later = (bexp[None, :] > bexp[:, None]) & used[None, :]
    nextexp = jnp.min(jnp.where(later, bexp[None, :], N_EXPERTS), axis=1)
    nextexp = jnp.where(nextexp < N_EXPERTS, nextexp, -1).astype(jnp.int32)
    return dest.reshape(N_ASSIGN), bexp, nvalid, nused, newexp, nextexp


def kernel(x, c, w_ada, b_ada, g_mix, g_ffn, w_in, a_ws, a_bs, a_vg, b_w, b_scale,
           c_qg, c_kg, c_bf, w_out, w_router, b_router, e_gate, e_up, e_down):
    xs = x.reshape(SEQ, D)
    c16 = jnp.broadcast_to(c, (16, D))
    b_ada3 = b_ada.reshape(DEPTH, 1, 6 * D)
    mod = _ada(c16, w_ada, b_ada3)
    wr_t = w_router.T.astype(BF16)
    br = b_router.reshape(N_EXPERTS, 1).astype(F32)
    n_uvp = 2 * A_WIDTH + B_WIDTH
    place = np.zeros((3 * HEAD, C_WIDTH), np.float32)
    for pj in range(3):
        for ph in range(C_HEADS):
            place[pj * HEAD + ph, ph * HEAD + pj] = 1.0
    place = jnp.asarray(place, BF16)
    w_in_bf, w_out_bf = w_in.astype(BF16), w_out.astype(BF16)
    pending = None
    for l in range(DEPTH):
        sh1, sc1, g1, sh2, sc2, g2 = [mod[:, j * D:(j + 1) * D] for j in range(6)]
        gain1 = g_mix[l].reshape(1, D)
        wf = jnp.pad(w_in_bf[l, :, n_uvp + 3 * C_WIDTH:], ((0, 0), (0, HEAD - C_HEADS)))
        bf = jnp.pad(c_bf[l], (0, HEAD - C_HEADS)).reshape(1, HEAD)
        bsb = jnp.repeat(a_bs[l].T, HEAD, axis=1)
        yab = _mix_ab(l, xs, gain1, sc1, sh1, w_in_bf, a_ws[l], bsb, a_vg[l].reshape(1, A_WIDTH),
                      b_w[l], b_scale[l].reshape(1, B_WIDTH), moe=pending)
        if pending is not None:
            yab, xs = yab
        qa, ka, vt, f2 = _mix_c(l, xs, gain1, sc1, sh1, w_in_bf, wf, place, c_qg[l].reshape(1, HEAD),
                                c_kg[l].reshape(1, HEAD), bf)
        first = _first_key_block(f2, c_qg[l], c_kg[l])
        if l + 1 < DEPTH:
            yc, mod = _attn(first, qa, ka, vt, next_ada=(l + 1, c16, w_ada, b_ada3))
        else:
            yc = _attn(first, qa, ka, vt)
        x1, h2, info, counts = _out_proj(l, yab, yc, xs, w_out_bf, g1,
                                         g_ffn[l].reshape(1, D), sc2, sh2, wr_t, br)
        plan = _dispatch_plan(info, counts)
        y = _moe(l, *plan, h2, e_gate, e_up, e_down)
        xs, pending = x1, (y, info, g2)
    return _combine(xs, *pending).reshape(1, SEQ, D)
```

```python
import functools

import jax
import jax.numpy as jnp
import numpy as np
from jax import lax
from jax.experimental import pallas as pl
from jax.experimental.pallas import tpu as pltpu

F32 = jnp.float32
BF16 = jnp.bfloat16

D = 2048
SEQ = 8192
DEPTH = 2
CHUNK = 64
A_WIDTH = 512
A_HEADS = 4
A_BLOCK = 128
B_WIDTH = 512
POOL_WINDOWS = (2, 4, 8, 16)
POOL_HALO = 16
C_WIDTH = 1024
C_HEADS = 8
HEAD = 128
N_EXPERTS = 16
N_GROUPS = 4
GROUP = 4
D_EXPERT = 1024
EPS = 1e-6
LOG2E = 1.4426950408889634

TM = 512
TQ = 512
TK = 512
ATTN_HEADS = 2
ADA_TN = 256
ADA_BLOCKS = 6 * D // ADA_TN
ATTN_DROP_LOG2 = 64.0
MOE_BLK = 256
MOE_CHUNKS = 4
N_ASSIGN = 2 * SEQ
MOE_ROWS = N_ASSIGN + N_EXPERTS * MOE_BLK
MOE_NB = MOE_ROWS // MOE_BLK

VMEM_LIMIT = 56 * 1024 * 1024


def _nt_dot(a, b):
    return lax.dot_general(a, b, (((1,), (1,)), ((), ())), preferred_element_type=F32)


def _rms_mod(x, gain, scale, shift):
    ms = jnp.mean(x * x, axis=-1, keepdims=True)
    return (x * lax.rsqrt(ms + EPS) * gain) * (1.0 + scale) + shift


def _ada_kernel(c_ref, w_ref, b_ref, o_ref):
    ca = jax.nn.silu(c_ref[...]).astype(BF16)
    r = jnp.dot(ca, w_ref[...].astype(BF16), preferred_element_type=F32)
    o_ref[...] = r[0:1, :] + b_ref[...]


def _ada(c16, w_ada, b_ada3):
    tn = 1024
    return pl.pallas_call(
        _ada_kernel,
        out_shape=jax.ShapeDtypeStruct((1, 6 * D), F32),
        grid=(6 * D // tn,),
        in_specs=[
            pl.BlockSpec((16, D), lambda j: (0, 0)),
            pl.BlockSpec((None, D, tn), lambda j: (0, 0, j)),
            pl.BlockSpec((None, 1, tn), lambda j: (0, 0, j)),
        ],
        out_specs=pl.BlockSpec((1, tn), lambda j: (0, j)),
        compiler_params=pltpu.CompilerParams(
            dimension_semantics=("arbitrary",), vmem_limit_bytes=VMEM_LIMIT),
        name="ada",
    )(c16, w_ada, b_ada3)


def _moe_residual(x1_ref, y0_ref, y1_ref, info_ref, g2_ref):
    gw = info_ref[...].T
    moe = gw[:, 2:3] * y0_ref[:, 0, :] + gw[:, 3:4] * y1_ref[:, 0, :]
    return x1_ref[...] + g2_ref[...] * moe


def _mix_ab_kernel(after_moe, *refs):
    if after_moe:
        x1_ref, y0_ref, y1_ref, info_ref, g2_ref = refs[:5]
        (gain_ref, sc_ref, sh_ref, w_ref, ws_ref, bsb_ref, vg_ref, bw_ref, bscale_ref,
         y_ref, x_out_ref, pext_ref) = refs[5:]
        x = _moe_residual(x1_ref, y0_ref, y1_ref, info_ref, g2_ref)
        x_out_ref[...] = x
    else:
        (x_ref, gain_ref, sc_ref, sh_ref, w_ref, ws_ref, bsb_ref, vg_ref, bw_ref, bscale_ref,
         y_ref, pext_ref) = refs
        x = x_ref[...]
    i = pl.program_id(0)
    h = _rms_mod(x, gain_ref[...], sc_ref[...], sh_ref[...])
    z = jnp.dot(h.astype(BF16), w_ref[...], preferred_element_type=F32)
    u = jax.nn.gelu(z[:, :A_WIDTH])
    v = jax.nn.gelu(z[:, A_WIDTH:2 * A_WIDTH])
    p = z[:, 2 * A_WIDTH:]

    cid_t = lax.broadcasted_iota(jnp.int32, (A_BLOCK, A_BLOCK), 0) // CHUNK
    cid_s = lax.broadcasted_iota(jnp.int32, (A_BLOCK, A_BLOCK), 1) // CHUNK
    mask = cid_s <= cid_t
    for hh in range(A_HEADS):
        cs = slice(hh * HEAD, (hh + 1) * HEAD)
        vh = v[:, cs]
        vn = vh * lax.rsqrt(jnp.mean(vh * vh, axis=-1, keepdims=True) + EPS) * vg_ref[:, cs]
        vnb = vn.astype(BF16)
        wm = jnp.where(mask, ws_ref[hh], 0.0).astype(BF16)
        for n in range(TM // A_BLOCK):
            rs = slice(n * A_BLOCK, (n + 1) * A_BLOCK)
            sp = jnp.dot(wm, vnb[rs], preferred_element_type=F32) + bsb_ref[:, cs]
            y_ref[rs, cs] = (u[rs, cs] * sp).astype(BF16)

    @pl.when(i == 0)
    def _():
        pext_ref[0:POOL_HALO, :] = jnp.zeros((POOL_HALO, B_WIDTH), F32)

    pext_ref[POOL_HALO:POOL_HALO + TM, :] = p
    t1 = i * TM + lax.broadcasted_iota(jnp.int32, (TM, 1), 0) + 1
    for g, w in enumerate(POOL_WINDOWS):
        cs = slice(g * HEAD, (g + 1) * HEAD)
        acc = p[:, cs]
        for j in range(1, w):
            acc = acc + pext_ref[POOL_HALO - j:POOL_HALO - j + TM, cs]
        cnt = jnp.minimum(t1, w).astype(F32)
        d = acc / cnt - p[:, cs]
        yb = jnp.dot(d.astype(BF16), bw_ref[g].astype(BF16), preferred_element_type=F32)
        y_ref[:, A_WIDTH + g * HEAD:A_WIDTH + (g + 1) * HEAD] = (yb * bscale_ref[:, cs]).astype(BF16)
    pext_ref[0:POOL_HALO, :] = pext_ref[TM:TM + POOL_HALO, :]


def _mix_ab(layer, x, gain, sc, sh, w_in_bf, a_ws, bsb, vg, b_w, bscale, moe=None):
    row = lambda i: (0, 0)
    nt = SEQ // TM
    tile = pl.BlockSpec((TM, D), lambda i: (i, 0))
    y_spec = pl.BlockSpec((TM, A_WIDTH + B_WIDTH), lambda i: (i, 0))
    y_shape = jax.ShapeDtypeStruct((SEQ, A_WIDTH + B_WIDTH), BF16)
    if moe is None:
        lead_specs, lead_args = [tile], (x,)
        out_shape, out_specs = y_shape, y_spec
    else:
        y, info, g2 = moe
        lead_specs = [tile,
                      pl.BlockSpec((TM, 1, D), lambda i: (i, 0, 0)),
                      pl.BlockSpec((TM, 1, D), lambda i: (i + nt, 0, 0)),
                      pl.BlockSpec((8, TM), lambda i: (0, i)),
                      pl.BlockSpec((1, D), row)]
        lead_args = (x, y, y, info, g2)
        out_shape, out_specs = (y_shape, jax.ShapeDtypeStruct((SEQ, D), F32)), (y_spec, tile)
    return pl.pallas_call(
        functools.partial(_mix_ab_kernel, moe is not None),
        out_shape=out_shape,
        grid=(nt,),
        in_specs=lead_specs + [
            pl.BlockSpec((1, D), row), pl.BlockSpec((1, D), row), pl.BlockSpec((1, D), row),
            pl.BlockSpec((None, D, 2 * A_WIDTH + B_WIDTH), lambda i: (layer, 0, 0),
                         pipeline_mode=pl.Buffered(1)),
            pl.BlockSpec((A_HEADS, A_BLOCK, A_BLOCK), lambda i: (0, 0, 0)),
            pl.BlockSpec((A_BLOCK, A_WIDTH), row),
            pl.BlockSpec((1, A_WIDTH), row),
            pl.BlockSpec((len(POOL_WINDOWS), HEAD, HEAD), lambda i: (0, 0, 0)),
            pl.BlockSpec((1, B_WIDTH), row),
        ],
        out_specs=out_specs,
        scratch_shapes=[pltpu.VMEM((TM + POOL_HALO, B_WIDTH), F32)],
        compiler_params=pltpu.CompilerParams(
            dimension_semantics=("arbitrary",), vmem_limit_bytes=VMEM_LIMIT),
        name="mix_ab",
    )(*lead_args, gain, sc, sh, w_in_bf, a_ws, bsb, vg, b_w, bscale)


def _split3(x):
    hi = x.astype(BF16)
    rest = x - hi.astype(F32)
    mid = rest.astype(BF16)
    lo = (rest - mid.astype(F32)).astype(BF16)
    return jnp.concatenate([hi, mid, lo], axis=1)


def _mix_c_kernel(x_ref, gain_ref, sc_ref, sh_ref, w1_ref, w2_ref, wf_ref, place_ref, qg_ref, kg_ref,
                  bf_ref, q_ref, k_ref, vt_ref, f2_ref, carry_ref):
    i = pl.program_id(0)
    h = _rms_mod(x_ref[...], gain_ref[...], sc_ref[...], sh_ref[...])
    hb = h.astype(BF16)
    z1 = jnp.dot(hb, w1_ref[...], preferred_element_type=F32)
    half = C_WIDTH // 2
    z2 = jnp.dot(hb, w2_ref[:, :half], preferred_element_type=F32)
    lane = lax.broadcasted_iota(jnp.int32, (TM, HEAD), 1)
    q_aug = jnp.where(lane < 3, -1.0, 0.0).astype(BF16)
    qscale = HEAD ** -0.5 * LOG2E
    for hh in range(C_HEADS):
        qh = z1[:, hh * HEAD:(hh + 1) * HEAD]
        qn = qh * lax.rsqrt(jnp.mean(qh * qh, axis=-1, keepdims=True) + EPS) * qg_ref[...]
        q_ref[:, 2 * hh * HEAD:(2 * hh + 1) * HEAD] = (qn * qscale).astype(BF16)
        q_ref[:, (2 * hh + 1) * HEAD:(2 * hh + 2) * HEAD] = q_aug
        if hh < C_HEADS // 2:
            kh = z1[:, C_WIDTH + hh * HEAD:C_WIDTH + (hh + 1) * HEAD]
        else:
            kh = z2[:, hh * HEAD - half:(hh + 1) * HEAD - half]
        kn = kh * lax.rsqrt(jnp.mean(kh * kh, axis=-1, keepdims=True) + EPS) * kg_ref[...]
        k_ref[:, 2 * hh * HEAD:(2 * hh + 1) * HEAD] = kn.astype(BF16)
    vt = lax.dot_general(w2_ref[:, half:], hb, (((0,), (1,)), ((), ())), preferred_element_type=F32)
    vt_ref[...] = vt.astype(BF16)

    @pl.when(i == 0)
    def _():
        carry_ref[...] = jnp.zeros((1, HEAD), F32)

    fz = jnp.dot(hb, wf_ref[...], preferred_element_type=F32)
    logf = jax.nn.log_sigmoid(fz + bf_ref[...])
    r = lax.broadcasted_iota(jnp.int32, (TM, TM), 0)
    c = lax.broadcasted_iota(jnp.int32, (TM, TM), 1)
    lower = jnp.where(c <= r, 1.0, 0.0).astype(BF16)
    cum = jnp.dot(lower, _split3(logf), preferred_element_type=F32)
    f_cum = (cum[:, :HEAD] + cum[:, HEAD:2 * HEAD]) + cum[:, 2 * HEAD:] + carry_ref[...]
    carry_ref[...] = f_cum[TM - 1:TM, :]
    f2 = f_cum * LOG2E
    f2_ref[...] = f2
    aug = jnp.dot(_split3(f2), place_ref[...], preferred_element_type=F32)
    for hh in range(C_HEADS):
        k_ref[:, (2 * hh + 1) * HEAD:(2 * hh + 2) * HEAD] = aug[:, hh * HEAD:(hh + 1) * HEAD].astype(BF16)


def _mix_c(layer, x, gain, sc, sh, w_in_bf, wf, place, qg, kg, bf):
    row = lambda i: (0, 0)
    wblk = 2 * A_WIDTH + B_WIDTH
    tile = pl.BlockSpec((TM, 2 * C_WIDTH), lambda i: (i, 0))
    return pl.pallas_call(
        _mix_c_kernel,
        out_shape=(
            jax.ShapeDtypeStruct((SEQ, 2 * C_WIDTH), BF16),
            jax.ShapeDtypeStruct((SEQ, 2 * C_WIDTH), BF16),
            jax.ShapeDtypeStruct((C_WIDTH, SEQ), BF16),
            jax.ShapeDtypeStruct((SEQ, HEAD), F32),
        ),
        grid=(SEQ // TM,),
        in_specs=[
            pl.BlockSpec((TM, D), lambda i: (i, 0)),
            pl.BlockSpec((1, D), row), pl.BlockSpec((1, D), row), pl.BlockSpec((1, D), row),
            pl.BlockSpec((None, D, wblk), lambda i: (layer, 0, 1), pipeline_mode=pl.Buffered(1)),
            pl.BlockSpec((None, D, wblk), lambda i: (layer, 0, 2), pipeline_mode=pl.Buffered(1)),
            pl.BlockSpec((D, HEAD), row),
            pl.BlockSpec((3 * HEAD, C_WIDTH), row),
            pl.BlockSpec((1, HEAD), row), pl.BlockSpec((1, HEAD), row),
            pl.BlockSpec((1, HEAD), row),
        ],
        out_specs=(tile, tile, pl.BlockSpec((C_WIDTH, TM), lambda i: (0, i)),
                   pl.BlockSpec((TM, HEAD), lambda i: (i, 0))),
        scratch_shapes=[pltpu.VMEM((1, HEAD), F32)],
        compiler_params=pltpu.CompilerParams(
            dimension_semantics=("arbitrary",), vmem_limit_bytes=VMEM_LIMIT),
        name="mix_c",
    )(x, gain, sc, sh, w_in_bf, w_in_bf, wf, place, qg, kg, bf)


def _attn_kernel(with_ada, first_ref, q_ref, k_ref, vt_ref, *refs):
    i = pl.program_id(1)
    step = pl.program_id(0) * (SEQ // TQ) + i
    first = first_ref[step]
    if with_ada:
        c_ref, wa_ref, ba_ref, o_ref, mod_ref, st_ref = refs

        @pl.when(step < ADA_BLOCKS)
        def _():
            _ada_kernel(c_ref, wa_ref, ba_ref, mod_ref)
    else:
        o_ref, st_ref = refs

    def scores_t(g, j):
        off = pl.multiple_of(j * TK, TK)
        cols = slice(2 * g * HEAD, 2 * (g + 1) * HEAD)
        return _nt_dot(k_ref[pl.ds(off, TK), cols], q_ref[:, cols])

    def update(g, read_st, j, m, l, acc):
        m_new = jnp.maximum(m, jnp.max(read_st(), axis=0, keepdims=True))
        alpha = jnp.exp2(m - m_new)
        p = jnp.exp2(read_st() - m_new)
        l_new = alpha * l + jnp.sum(p, axis=0, keepdims=True)
        off = pl.multiple_of(j * TK, TK)
        pv = jnp.dot(vt_ref[g * HEAD:(g + 1) * HEAD, pl.ds(off, TK)], p.astype(BF16),
                     preferred_element_type=F32)
        return m_new, l_new, alpha * acc + pv

    for g in range(ATTN_HEADS):
        st_ref[g] = scores_t(g, first)

    def body(j, carry):
        out = []
        for g in range(ATTN_HEADS):
            out.append(update(g, lambda g=g: st_ref[g], j, *carry[g]))
            st_ref[g] = scores_t(g, j + 1)
        return tuple(out)

    init = (jnp.full((1, TQ), -jnp.inf, F32), jnp.zeros((1, TQ), F32), jnp.zeros((HEAD, TQ), F32))
    pairs = (i - first) // 2
    carry = lax.fori_loop(0, pairs, lambda t, c: body(first + 2 * t + 1, body(first + 2 * t, c)),
                          (init,) * ATTN_HEADS)
    odd = jnp.maximum(i - 1, 0)
    carry = lax.cond((i - first) % 2 == 1, lambda c: body(odd, c), lambda c: c, carry)
    kpos = lax.broadcasted_iota(jnp.int32, (TK, TQ), 0)
    qpos = lax.broadcasted_iota(jnp.int32, (TK, TQ), 1)
    for g in range(ATTN_HEADS):
        m, l, acc = update(g, lambda g=g: jnp.where(kpos <= qpos, st_ref[g], -jnp.inf), i, *carry[g])
        o_ref[:, g * HEAD:(g + 1) * HEAD] = (acc / l).T.astype(BF16)


def _first_key_block(f2, qg, kg):
    nq = SEQ // TQ
    bound = LOG2E * HEAD ** 0.5 * jnp.max(jnp.abs(qg)) * jnp.max(jnp.abs(kg)) * 1.02
    f_start = f2[0::TQ, :C_HEADS]
    f_end = f2[TK - 1::TK, :C_HEADS]
    gap = f_end[None, :, :] - f_start[:, None, :]
    needed = gap <= 2.0 * bound + ATTN_DROP_LOG2
    needed = jnp.any(needed.reshape(nq, SEQ // TK, C_HEADS // ATTN_HEADS, ATTN_HEADS), axis=-1)
    first = jnp.argmax(needed, axis=1)
    return first.T.reshape(-1).astype(jnp.int32)


def _attn(first, qa, ka, vt, next_ada=None):
    gw = ATTN_HEADS * HEAD
    nq = SEQ // TQ
    in_specs = [
        pl.BlockSpec((TQ, 2 * gw), lambda h, i, f: (i, h)),
        pl.BlockSpec((SEQ, 2 * gw), lambda h, i, f: (0, h)),
        pl.BlockSpec((gw, SEQ), lambda h, i, f: (h, 0)),
    ]
    out_shape = jax.ShapeDtypeStruct((SEQ, C_WIDTH), BF16)
    out_specs = pl.BlockSpec((TQ, gw), lambda h, i, f: (i, h))
    args = (first, qa, ka, vt)
    if next_ada is not None:
        layer, c16, w_ada, b_ada3 = next_ada
        assert ADA_BLOCKS <= (C_HEADS // ATTN_HEADS) * nq
        blk = lambda h, i, f: jnp.minimum(h * nq + i, ADA_BLOCKS - 1)
        in_specs += [
            pl.BlockSpec((16, D), lambda h, i, f: (0, 0)),
            pl.BlockSpec((None, D, ADA_TN), lambda h, i, f: (layer, 0, blk(h, i, f))),
            pl.BlockSpec((None, 1, ADA_TN), lambda h, i, f: (layer, 0, blk(h, i, f))),
        ]
        out_shape = (out_shape, jax.ShapeDtypeStruct((1, 6 * D), F32))
        out_specs = (out_specs, pl.BlockSpec((1, ADA_TN), lambda h, i, f: (0, blk(h, i, f))))
        args += (c16, w_ada, b_ada3)
    return pl.pallas_call(
        functools.partial(_attn_kernel, next_ada is not None),
        out_shape=out_shape,
        grid_spec=pltpu.PrefetchScalarGridSpec(
            num_scalar_prefetch=1,
            grid=(C_HEADS // ATTN_HEADS, nq),
            in_specs=in_specs,
            out_specs=out_specs,
            scratch_shapes=[pltpu.VMEM((ATTN_HEADS, TK, TQ), F32)],
        ),
        compiler_params=pltpu.CompilerParams(
            dimension_semantics=("arbitrary", "arbitrary"), vmem_limit_bytes=VMEM_LIMIT),
        name="attn",
    )(*args)


def _first_max(vals):
    best, idx = vals[0], jnp.zeros(vals[0].shape, jnp.int32)
    for j in range(1, len(vals)):
        better = vals[j] > best
        idx = jnp.where(better, j, idx)
        best = jnp.where(better, vals[j], best)
    return idx, best


def _pick(idx, vals):
    out = vals[-1]
    for j in range(len(vals) - 2, -1, -1):
        out = jnp.where(idx == j, vals[j], out)
    return out


def _out_kernel(yab_ref, yc_ref, x_ref, w_ref, g1_ref, gain_ref, sc_ref, sh_ref, wr_ref, br_ref,
                x1_ref, h2_ref, info_ref, cnt_ref, carry_ref):
    i = pl.program_id(0)
    y = jnp.dot(yab_ref[...], w_ref[0:A_WIDTH + B_WIDTH, :], preferred_element_type=F32)
    y = y + jnp.dot(yc_ref[...], w_ref[A_WIDTH + B_WIDTH:, :], preferred_element_type=F32)
    x1 = x_ref[...] + g1_ref[...] * y
    x1_ref[...] = x1
    h2 = _rms_mod(x1, gain_ref[...], sc_ref[...], sh_ref[...])
    h2b = h2.astype(BF16)
    lo = pltpu.bitcast(h2b[:, :D // 2].astype(F32), jnp.uint32) >> 16
    hi = pltpu.bitcast(h2b[:, D // 2:].astype(F32), jnp.uint32) & jnp.uint32(0xFFFF0000)
    h2_ref[:, 0, :] = hi | lo

    logits = _nt_dot(wr_ref[...], h2b)
    scores = jax.nn.sigmoid(logits)
    sel = scores + br_ref[...]
    sel_r = [sel[k:k + 1, :] for k in range(N_EXPERTS)]
    sc_r = [scores[k:k + 1, :] for k in range(N_EXPERTS)]
    grp = []
    for g in range(N_GROUPS):
        a = sel_r[GROUP * g:GROUP * (g + 1)]
        pair = [a[p] + a[q] for p in range(GROUP) for q in range(p + 1, GROUP)]
        grp.append(functools.reduce(jnp.maximum, pair))
    gi, _ = _first_max(grp)
    cand = [_pick(gi, [sel_r[GROUP * g + j] for g in range(N_GROUPS)]) for j in range(GROUP)]
    cand_s = [_pick(gi, [sc_r[GROUP * g + j] for g in range(N_GROUPS)]) for j in range(GROUP)]
    i0, _ = _first_max(cand)
    i1, _ = _first_max([jnp.where(i0 == j, -jnp.inf, cand[j]) for j in range(GROUP)])
    s0, s1 = _pick(i0, cand_s), _pick(i1, cand_s)
    e0, e1 = GROUP * gi + i0, GROUP * gi + i1
    den = s0 + s1

    @pl.when(i == 0)
    def _():
        carry_ref[...] = jnp.zeros((N_EXPERTS, HEAD), F32)

    ek = lax.broadcasted_iota(jnp.int32, (N_EXPERTS, TM), 0)
    oh0 = (ek == e0).astype(F32)
    oh1 = (ek == e1).astype(F32)
    both = oh0 + oh1
    r = lax.broadcasted_iota(jnp.int32, (TM, TM), 0)
    c = lax.broadcasted_iota(jnp.int32, (TM, TM), 1)
    before = (r < c).astype(BF16)
    run = jnp.dot(both.astype(BF16), before, preferred_element_type=F32) + carry_ref[:, 0:1]
    rank0 = jnp.sum(run * oh0, axis=0, keepdims=True)
    rank1 = jnp.sum(run * oh1, axis=0, keepdims=True)
    total = carry_ref[...] + jnp.sum(both, axis=-1, keepdims=True)
    carry_ref[...] = total
    cnt_ref[...] = total

    info_ref[0:1, :] = e0.astype(F32)
    info_ref[1:2, :] = e1.astype(F32)
    info_ref[2:3, :] = s0 / den
    info_ref[3:4, :] = s1 / den
    info_ref[4:5, :] = rank0
    info_ref[5:6, :] = rank1
    info_ref[6:8, :] = jnp.zeros((2, TM), F32)


def _out_proj(layer, yab, yc, x, w_out, g1, gain, sc, sh, wr_t, br):
    row = lambda i: (0, 0)
    return pl.pallas_call(
        _out_kernel,
        out_shape=(
            jax.ShapeDtypeStruct((SEQ, D), F32),
            jax.ShapeDtypeStruct((SEQ, 1, D // 2), jnp.uint32),
            jax.ShapeDtypeStruct((8, SEQ), F32),
            jax.ShapeDtypeStruct((N_EXPERTS, HEAD), F32),
        ),
        grid=(SEQ // TM,),
        in_specs=[
            pl.BlockSpec((TM, A_WIDTH + B_WIDTH), lambda i: (i, 0)),
            pl.BlockSpec((TM, C_WIDTH), lambda i: (i, 0)),
            pl.BlockSpec((TM, D), lambda i: (i, 0)),
            pl.BlockSpec((None, D, D), lambda i: (layer, 0, 0), pipeline_mode=pl.Buffered(1)),
            pl.BlockSpec((1, D), row), pl.BlockSpec((1, D), row),
            pl.BlockSpec((1, D), row), pl.BlockSpec((1, D), row),
            pl.BlockSpec((N_EXPERTS, D), row),
            pl.BlockSpec((N_EXPERTS, 1), row),
        ],
        out_specs=(
            pl.BlockSpec((TM, D), lambda i: (i, 0)),
            pl.BlockSpec((TM, 1, D // 2), lambda i: (i, 0, 0)),
            pl.BlockSpec((8, TM), lambda i: (0, i)),
            pl.BlockSpec((N_EXPERTS, HEAD), row),
        ),
        scratch_shapes=[pltpu.VMEM((N_EXPERTS, HEAD), F32)],
        compiler_params=pltpu.CompilerParams(
            dimension_semantics=("arbitrary",), vmem_limit_bytes=VMEM_LIMIT),
        name="out_proj",
    )(yab, yc, x, w_out, g1, gain, sc, sh, wr_t, br)


def _moe_kernel(layer, dest_ref, bexp_ref, nvalid_ref, nused_ref, newexp_ref, nextexp_ref,
                h_hbm, wg_hbm, wu_hbm, wd_hbm, zeros_hbm, y_hbm,
                code_ref, xs_ref, ys_ref, xb_ref, hid_ref, wg32_ref, wu32_ref, wd32_ref,
                wg_ref, wu_ref, wd_ref, gsem, ssem, wsem):
    b = pl.program_id(0)
    n_used = nused_ref[0]
    slot = b % 2

    def weight_copies(e):
        return (pltpu.make_async_copy(wg_hbm.at[layer, e], wg32_ref, wsem.at[0]),
                pltpu.make_async_copy(wu_hbm.at[layer, e], wu32_ref, wsem.at[1]),
                pltpu.make_async_copy(wd_hbm.at[layer, e], wd32_ref, wsem.at[2]))

    def switch_expert():
        for cp in weight_copies(bexp_ref[b]):
            cp.wait()

        def cast(k, _):
            r = pl.multiple_of(k * 256, 256)
            wg_ref[pl.ds(r, 256), :] = wg32_ref[pl.ds(r, 256), :].astype(BF16)
            wu_ref[pl.ds(r, 256), :] = wu32_ref[pl.ds(r, 256), :].astype(BF16)
            r = pl.multiple_of(k * 128, 128)
            wd_ref[pl.ds(r, 128), :] = wd32_ref[pl.ds(r, 128), :].astype(BF16)
            return 0
        lax.fori_loop(0, D // 256, cast, 0)

        @pl.when(nextexp_ref[b] >= 0)
        def _():
            for cp in weight_copies(jnp.maximum(nextexp_ref[b], 0)):
                cp.start(priority=1)

    def gather_copy(blk, r, s):
        tok = code_ref[blk * MOE_BLK + r] & (SEQ - 1)
        return pltpu.make_async_copy(h_hbm.at[tok], xs_ref.at[s, pl.ds(r, 1), :],
                                     gsem.at[s])

    def scatter_copy(blk, r, s):
        dst = code_ref[blk * MOE_BLK + r]
        return pltpu.make_async_copy(ys_ref.at[s, pl.ds(r, 1), :], y_hbm.at[dst],
                                     ssem.at[s])

    def for_rows(n, fn):
        def body(r, _):
            fn(r)
            return 0
        lax.fori_loop(0, n, body, 0)

    def scatter_wait(n, s):
        n8 = pl.multiple_of((n >> 3) << 3, 8)

        @pl.when(n8 > 0)
        def _():
            pltpu.make_async_copy(ys_ref.at[s, pl.ds(0, n8), :], ys_ref.at[s, pl.ds(0, n8), :],
                                  ssem.at[s]).wait()

        for r in range(7):
            @pl.when(n8 + r < n)
            def _():
                pltpu.make_async_copy(ys_ref.at[s, pl.ds(r, 1), :], y_hbm.at[r],
                                      ssem.at[s]).wait()

    @pl.when(b == 0)
    def _():
        for cp in weight_copies(bexp_ref[0]):
            cp.start()

        fill = pltpu.make_async_copy(zeros_hbm, code_ref, wsem.at[3])
        fill.start()
        fill.wait()

        def place(a, _):
            code_ref[dest_ref[a]] = a
            return 0
        lax.fori_loop(0, N_ASSIGN, place, 0, unroll=8)
        for_rows(MOE_BLK, lambda r: gather_copy(0, r, 0).start())

    def step(prefetch):
        @pl.when(newexp_ref[b] == 1)
        def _():
            switch_expert()

        for r in range(MOE_BLK):
            gather_copy(b, r, slot).wait()

        packed = xs_ref[slot]
        xb_ref[:, :D // 2] = pltpu.bitcast(packed << 16, F32).astype(BF16)
        xb_ref[:, D // 2:] = pltpu.bitcast(packed & jnp.uint32(0xFFFF0000), F32).astype(BF16)
        prev = jnp.maximum(b - 1, 0)
        n_prev = jnp.where(b >= 1, nvalid_ref[prev], 0)
        x = xb_ref[...]
        rows_per = MOE_BLK // MOE_CHUNKS

        hc = D_EXPERT // MOE_CHUNKS
        for c in range(MOE_CHUNKS):
            cs = slice(c * hc, (c + 1) * hc)
            gate = jnp.dot(x, wg_ref[:, cs], preferred_element_type=F32)
            up = jnp.dot(x, wu_ref[:, cs], preferred_element_type=F32)
            hid = jax.nn.silu(gate) * up
            if prefetch:
                for r in range(c * rows_per, (c + 1) * rows_per):
                    gather_copy(b + 1, r, 1 - slot).start()
                tie = xs_ref[slot, 0:8, 0:hc]
                tie = pltpu.bitcast((tie >> 16) >> 16, F32)
                hid = jnp.concatenate([hid[0:8] + tie, hid[8:]], axis=0)
            hid_ref[:, cs] = hid.astype(BF16)

        scatter_wait(jnp.where(b >= 2, nvalid_ref[jnp.maximum(b - 2, 0)], 0), slot)
        hidb = hid_ref[...]
        oc = D // MOE_CHUNKS
        for c in range(MOE_CHUNKS):
            for r in range(c * rows_per, (c + 1) * rows_per):
                @pl.when(r < n_prev)
                def _():
                    scatter_copy(prev, r, 1 - slot).start(priority=1)
            cs = slice(c * oc, (c + 1) * oc)
            ys_ref[slot, :, cs] = jnp.dot(hidb, wd_ref[:, cs], preferred_element_type=F32)

    @pl.when(b < n_used - 1)
    def _():
        step(True)

    @pl.when(b == n_used - 1)
    def _():
        step(False)
        for_rows(nvalid_ref[b], lambda r: scatter_copy(b, r, slot).start())
        scatter_wait(nvalid_ref[b], slot)
        scatter_wait(jnp.where(b >= 1, nvalid_ref[jnp.maximum(b - 1, 0)], 0), 1 - slot)


def _moe(layer, dest, bexp, nvalid, nused, newexp, nextexp, h2, wg, wu, wd):
    hbm = pl.BlockSpec(memory_space=pl.ANY)
    return pl.pallas_call(
        functools.partial(_moe_kernel, layer),
        out_shape=jax.ShapeDtypeStruct((N_ASSIGN, 1, D), F32),
        grid_spec=pltpu.PrefetchScalarGridSpec(
            num_scalar_prefetch=6,
            grid=(MOE_NB,),
            in_specs=[hbm, hbm, hbm, hbm, hbm],
            out_specs=hbm,
            scratch_shapes=[
                pltpu.SMEM((MOE_ROWS,), jnp.int32),
                pltpu.VMEM((2, MOE_BLK, D // 2), jnp.uint32),
                pltpu.VMEM((2, MOE_BLK, D), F32),
                pltpu.VMEM((MOE_BLK, D), BF16),
                pltpu.VMEM((MOE_BLK, D_EXPERT), BF16),
                pltpu.VMEM((D, D_EXPERT), F32),
                pltpu.VMEM((D, D_EXPERT), F32),
                pltpu.VMEM((D_EXPERT, D), F32),
                pltpu.VMEM((D, D_EXPERT), BF16),
                pltpu.VMEM((D, D_EXPERT), BF16),
                pltpu.VMEM((D_EXPERT, D), BF16),
                pltpu.SemaphoreType.DMA((2,)),
                pltpu.SemaphoreType.DMA((2,)),
                pltpu.SemaphoreType.DMA((4,)),
            ],
        ),
        compiler_params=pltpu.CompilerParams(
            dimension_semantics=("arbitrary",), vmem_limit_bytes=VMEM_LIMIT),
        name="moe",
    )(dest, bexp, nvalid, nused, newexp, nextexp, h2, wg, wu, wd, jnp.zeros((MOE_ROWS,), jnp.int32))


def _combine_kernel(x1_ref, y0_ref, y1_ref, info_ref, g2_ref, o_ref):
    o_ref[...] = _moe_residual(x1_ref, y0_ref, y1_ref, info_ref, g2_ref)


def _combine(x1, y, info, g2):
    nt = SEQ // TM
    return pl.pallas_call(
        _combine_kernel,
        out_shape=jax.ShapeDtypeStruct((SEQ, D), F32),
        grid=(nt,),
        in_specs=[
            pl.BlockSpec((TM, D), lambda i: (i, 0)),
            pl.BlockSpec((TM, 1, D), lambda i: (i, 0, 0)),
            pl.BlockSpec((TM, 1, D), lambda i: (i + nt, 0, 0)),
            pl.BlockSpec((8, TM), lambda i: (0, i)),
            pl.BlockSpec((1, D), lambda i: (0, 0)),
        ],
        out_specs=pl.BlockSpec((TM, D), lambda i: (i, 0)),
        compiler_params=pltpu.CompilerParams(
            dimension_semantics=("arbitrary",), vmem_limit_bytes=VMEM_LIMIT),
        name="combine",
    )(x1, y, y, info, g2)


def _dispatch_plan(info, counts):
    cnt = counts[:, 0].astype(jnp.int32)
    padded = (cnt + MOE_BLK - 1) // MOE_BLK * MOE_BLK
    ends = jnp.cumsum(padded)
    pad_start = ends - padded
    e = info[0:2].astype(jnp.int32)
    rank = info[4:6].astype(jnp.int32)
    onehot = e[:, :, None] == jnp.arange(N_EXPERTS, dtype=jnp.int32)
    dest = jnp.sum(jnp.where(onehot, pad_start, 0), axis=-1) + rank
    blk_start = jnp.arange(MOE_NB, dtype=jnp.int32) * MOE_BLK
    bexp = jnp.sum(blk_start[:, None] >= ends[None, :], axis=-1)
    bexp = jnp.minimum(bexp, N_EXPERTS - 1).astype(jnp.int32)
    in_blk = jnp.arange(N_EXPERTS, dtype=jnp.int32)[None, :] == bexp[:, None]
    left = jnp.sum(jnp.where(in_blk, cnt + pad_start, 0), axis=-1) - blk_start
    nvalid = jnp.clip(left, 0, MOE_BLK).astype(jnp.int32)
    nused = (ends[-1:] // MOE_BLK).astype(jnp.int32)
    used = jnp.arange(MOE_NB, dtype=jnp.int32) < nused[0]
    newexp = jnp.concatenate([jnp.ones((1,), jnp.int32), (bexp[1:] != bexp[:-1]).astype(jnp.int32)])
    later = (bexp[None, :] > bexp[:, None]) & used[None, :]
    nextexp = jnp.min(jnp.where(later, bexp[None, :], N_EXPERTS), axis=1)
    nextexp = jnp.where(nextexp < N_EXPERTS, nextexp, -1).astype(jnp.int32)
    return dest.reshape(N_ASSIGN), bexp, nvalid, nused, newexp, nextexp


def kernel(x, c, w_ada, b_ada, g_mix, g_ffn, w_in, a_ws, a_bs, a_vg, b_w, b_scale,
           c_qg, c_kg, c_bf, w_out, w_router, b_router, e_gate, e_up, e_down):
    xs = x.reshape(SEQ, D)
    c16 = jnp.broadcast_to(c, (16, D))
    b_ada3 = b_ada.reshape(DEPTH, 1, 6 * D)
    mod = _ada(c16, w_ada, b_ada3)
    wr_t = w_router.T.astype(BF16)
    br = b_router.reshape(N_EXPERTS, 1).astype(F32)
    n_uvp = 2 * A_WIDTH + B_WIDTH
    place = np.zeros((3 * HEAD, C_WIDTH), np.float32)
    for pj in range(3):
        for ph in range(C_HEADS):
            place[pj * HEAD + ph, ph * HEAD + pj] = 1.0
    place = jnp.asarray(place, BF16)
    w_in_bf, w_out_bf = w_in.astype(BF16), w_out.astype(BF16)
    pending = None
    for l in range(DEPTH):
        sh1, sc1, g1, sh2, sc2, g2 = [mod[:, j * D:(j + 1) * D] for j in range(6)]
        gain1 = g_mix[l].reshape(1, D)
        wf = jnp.pad(w_in_bf[l, :, n_uvp + 3 * C_WIDTH:], ((0, 0), (0, HEAD - C_HEADS)))
        bf = jnp.pad(c_bf[l], (0, HEAD - C_HEADS)).reshape(1, HEAD)
        bsb = jnp.repeat(a_bs[l].T, HEAD, axis=1)
        yab = _mix_ab(l, xs, gain1, sc1, sh1, w_in_bf, a_ws[l], bsb, a_vg[l].reshape(1, A_WIDTH),
                      b_w[l], b_scale[l].reshape(1, B_WIDTH), moe=pending)
        if pending is not None:
            yab, xs = yab
        qa, ka, vt, f2 = _mix_c(l, xs, gain1, sc1, sh1, w_in_bf, wf, place, c_qg[l].reshape(1, HEAD),
                                c_kg[l].reshape(1, HEAD), bf)
        first = _first_key_block(f2, c_qg[l], c_kg[l])
        if l + 1 < DEPTH:
            yc, mod = _attn(first, qa, ka, vt, next_ada=(l + 1, c16, w_ada, b_ada3))
        else:
            yc = _attn(first, qa, ka, vt)
        x1, h2, info, counts = _out_proj(l, yab, yc, xs, w_out_bf, g1,
                                         g_ffn[l].reshape(1, D), sc2, sh2, wr_t, br)
        plan = _dispatch_plan(info, counts)
        y = _moe(l, *plan, h2, e_gate, e_up, e_down)
        xs, pending = x1, (y, info, g2)
    return _combine(xs, *pending).reshape(1, SEQ, D)
```

```python
import functools

import jax
import jax.numpy as jnp
import numpy as np
from jax import lax
from jax.experimental import pallas as pl
from jax.experimental.pallas import tpu as pltpu

F32 = jnp.float32
BF16 = jnp.bfloat16

D = 2048
SEQ = 8192
DEPTH = 2
CHUNK = 64
A_WIDTH = 512
A_HEADS = 4
A_BLOCK = 128
B_WIDTH = 512
POOL_WINDOWS = (2, 4, 8, 16)
POOL_HALO = 16
C_WIDTH = 1024
C_HEADS = 8
HEAD = 128
N_EXPERTS = 16
N_GROUPS = 4
GROUP = 4
D_EXPERT = 1024
EPS = 1e-6
LOG2E = 1.4426950408889634

TM = 512
TQ = 512
TK = 512
ATTN_HEADS = 2
ADA_TN = 256
ADA_BLOCKS = 6 * D // ADA_TN
ATTN_DROP_LOG2 = 64.0
MOE_BLK = 256
MOE_CHUNKS = 4
N_ASSIGN = 2 * SEQ
MOE_ROWS = N_ASSIGN + N_EXPERTS * MOE_BLK
MOE_NB = MOE_ROWS // MOE_BLK

VMEM_LIMIT = 56 * 1024 * 1024


def _nt_dot(a, b):
    return lax.dot_general(a, b, (((1,), (1,)), ((), ())), preferred_element_type=F32)


def _rms_mod(x, gain, scale, shift):
    ms = jnp.mean(x * x, axis=-1, keepdims=True)
    return (x * lax.rsqrt(ms + EPS) * gain) * (1.0 + scale) + shift


def _ada_kernel(c_ref, w_ref, b_ref, o_ref):
    ca = jax.nn.silu(c_ref[...]).astype(BF16)
    r = jnp.dot(ca, w_ref[...].astype(BF16), preferred_element_type=F32)
    o_ref[...] = r[0:1, :] + b_ref[...]


def _ada(c16, w_ada, b_ada3):
    tn = 1024
    return pl.pallas_call(
        _ada_kernel,
        out_shape=jax.ShapeDtypeStruct((1, 6 * D), F32),
        grid=(6 * D // tn,),
        in_specs=[
            pl.BlockSpec((16, D), lambda j: (0, 0)),
            pl.BlockSpec((None, D, tn), lambda j: (0, 0, j)),
            pl.BlockSpec((None, 1, tn), lambda j: (0, 0, j)),
        ],
        out_specs=pl.BlockSpec((1, tn), lambda j: (0, j)),
        compiler_params=pltpu.CompilerParams(
            dimension_semantics=("arbitrary",), vmem_limit_bytes=VMEM_LIMIT // 2),
        name="ada",
    )(c16, w_ada, b_ada3)


def _moe_residual(x1_ref, y0_ref, y1_ref, info_ref, g2_ref):
    gw = info_ref[...].T
    moe = gw[:, 2:3] * y0_ref[:, 0, :] + gw[:, 3:4] * y1_ref[:, 0, :]
    return x1_ref[...] + g2_ref[...] * moe


def _mix_ab_kernel(after_moe, *refs):
    if after_moe:
        x1_ref, y0_ref, y1_ref, info_ref, g2_ref = refs[:5]
        (gain_ref, sc_ref, sh_ref, w_ref, ws_ref, bsb_ref, vg_ref, bw_ref, bscale_ref,
         y_ref, x_out_ref, pext_ref) = refs[5:]
        x = _moe_residual(x1_ref, y0_ref, y1_ref, info_ref, g2_ref)
        x_out_ref[...] = x
    else:
        (x_ref, gain_ref, sc_ref, sh_ref, w_ref, ws_ref, bsb_ref, vg_ref, bw_ref, bscale_ref,
         y_ref, pext_ref) = refs
        x = x_ref[...]
    i = pl.program_id(0)
    h = _rms_mod(x, gain_ref[...], sc_ref[...], sh_ref[...])
    z = jnp.dot(h.astype(BF16), w_ref[...], preferred_element_type=F32)
    u = jax.nn.gelu(z[:, :A_WIDTH])
    v = jax.nn.gelu(z[:, A_WIDTH:2 * A_WIDTH])
    p = z[:, 2 * A_WIDTH:]

    cid_t = lax.broadcasted_iota(jnp.int32, (A_BLOCK, A_BLOCK), 0) // CHUNK
    cid_s = lax.broadcasted_iota(jnp.int32, (A_BLOCK, A_BLOCK), 1) // CHUNK
    mask = cid_s <= cid_t
    for hh in range(A_HEADS):
        cs = slice(hh * HEAD, (hh + 1) * HEAD)
        vh = v[:, cs]
        vn = vh * lax.rsqrt(jnp.mean(vh * vh, axis=-1, keepdims=True) + EPS) * vg_ref[:, cs]
        vnb = vn.astype(BF16)
        wm = jnp.where(mask, ws_ref[hh], 0.0).astype(BF16)
        for n in range(TM // A_BLOCK):
            rs = slice(n * A_BLOCK, (n + 1) * A_BLOCK)
            sp = jnp.dot(wm, vnb[rs], preferred_element_type=F32) + bsb_ref[:, cs]
            y_ref[rs, cs] = (u[rs, cs] * sp).astype(BF16)

    @pl.when(i == 0)
    def _():
        pext_ref[0:POOL_HALO, :] = jnp.zeros((POOL_HALO, B_WIDTH), F32)

    pext_ref[POOL_HALO:POOL_HALO + TM, :] = p
    t1 = i * TM + lax.broadcasted_iota(jnp.int32, (TM, 1), 0) + 1
    for g, w in enumerate(POOL_WINDOWS):
        cs = slice(g * HEAD, (g + 1) * HEAD)
        acc = p[:, cs]
        for j in range(1, w):
            acc = acc + pext_ref[POOL_HALO - j:POOL_HALO - j + TM, cs]
        cnt = jnp.minimum(t1, w).astype(F32)
        d = acc / cnt - p[:, cs]
        yb = jnp.dot(d.astype(BF16), bw_ref[g].astype(BF16), preferred_element_type=F32)
        y_ref[:, A_WIDTH + g * HEAD:A_WIDTH + (g + 1) * HEAD] = (yb * bscale_ref[:, cs]).astype(BF16)
    pext_ref[0:POOL_HALO, :] = pext_ref[TM:TM + POOL_HALO, :]


def _mix_ab(layer, x, gain, sc, sh, w_in_bf, a_ws, bsb, vg, b_w, bscale, moe=None):
    row = lambda i: (0, 0)
    nt = SEQ // TM
    tile = pl.BlockSpec((TM, D), lambda i: (i, 0))
    y_spec = pl.BlockSpec((TM, A_WIDTH + B_WIDTH), lambda i: (i, 0))
    y_shape = jax.ShapeDtypeStruct((SEQ, A_WIDTH + B_WIDTH), BF16)
    if moe is None:
        lead_specs, lead_args = [tile], (x,)
        out_shape, out_specs = y_shape, y_spec
    else:
        y, info, g2 = moe
        lead_specs = [tile,
                      pl.BlockSpec((TM, 1, D), lambda i: (i, 0, 0)),
                      pl.BlockSpec((TM, 1, D), lambda i: (i + nt, 0, 0)),
                      pl.BlockSpec((8, TM), lambda i: (0, i)),
                      pl.BlockSpec((1, D), row)]
        lead_args = (x, y, y, info, g2)
        out_shape, out_specs = (y_shape, jax.ShapeDtypeStruct((SEQ, D), F32)), (y_spec, tile)
    return pl.pallas_call(
        functools.partial(_mix_ab_kernel, moe is not None),
        out_shape=out_shape,
        grid=(nt,),
        in_specs=lead_specs + [
            pl.BlockSpec((1, D), row), pl.BlockSpec((1, D), row), pl.BlockSpec((1, D), row),
            pl.BlockSpec((None, D, 2 * A_WIDTH + B_WIDTH), lambda i: (layer, 0, 0),
                         pipeline_mode=pl.Buffered(1)),
            pl.BlockSpec((A_HEADS, A_BLOCK, A_BLOCK), lambda i: (0, 0, 0)),
            pl.BlockSpec((A_BLOCK, A_WIDTH), row),
            pl.BlockSpec((1, A_WIDTH), row),
            pl.BlockSpec((len(POOL_WINDOWS), HEAD, HEAD), lambda i: (0, 0, 0)),
            pl.BlockSpec((1, B_WIDTH), row),
        ],
        out_specs=out_specs,
        scratch_shapes=[pltpu.VMEM((TM + POOL_HALO, B_WIDTH), F32)],
        compiler_params=pltpu.CompilerParams(
            dimension_semantics=("arbitrary",), vmem_limit_bytes=VMEM_LIMIT),
        name="mix_ab",
    )(*lead_args, gain, sc, sh, w_in_bf, a_ws, bsb, vg, b_w, bscale)


def _split3(x):
    hi = x.astype(BF16)
    rest = x - hi.astype(F32)
    mid = rest.astype(BF16)
    lo = (rest - mid.astype(F32)).astype(BF16)
    return jnp.concatenate([hi, mid, lo], axis=1)


def _mix_c_kernel(x_ref, gain_ref, sc_ref, sh_ref, w1_ref, w2_ref, wf_ref, place_ref, qg_ref, kg_ref,
                  bf_ref, q_ref, k_ref, vt_ref, f2_ref, carry_ref):
    i = pl.program_id(0)
    h = _rms_mod(x_ref[...], gain_ref[...], sc_ref[...], sh_ref[...])
    hb = h.astype(BF16)
    z1 = jnp.dot(hb, w1_ref[...], preferred_element_type=F32)
    half = C_WIDTH // 2
    z2 = jnp.dot(hb, w2_ref[:, :half], preferred_element_type=F32)
    lane = lax.broadcasted_iota(jnp.int32, (TM, HEAD), 1)
    q_aug = jnp.where(lane < 3, -1.0, 0.0).astype(BF16)
    qscale = HEAD ** -0.5 * LOG2E
    for hh in range(C_HEADS):
        qh = z1[:, hh * HEAD:(hh + 1) * HEAD]
        qn = qh * lax.rsqrt(jnp.mean(qh * qh, axis=-1, keepdims=True) + EPS) * qg_ref[...]
        q_ref[:, 2 * hh * HEAD:(2 * hh + 1) * HEAD] = (qn * qscale).astype(BF16)
        q_ref[:, (2 * hh + 1) * HEAD:(2 * hh + 2) * HEAD] = q_aug
        if hh < C_HEADS // 2:
            kh = z1[:, C_WIDTH + hh * HEAD:C_WIDTH + (hh + 1) * HEAD]
        else:
            kh = z2[:, hh * HEAD - half:(hh + 1) * HEAD - half]
        kn = kh * lax.rsqrt(jnp.mean(kh * kh, axis=-1, keepdims=True) + EPS) * kg_ref[...]
        k_ref[:, 2 * hh * HEAD:(2 * hh + 1) * HEAD] = kn.astype(BF16)
    vt = lax.dot_general(w2_ref[:, half:], hb, (((0,), (1,)), ((), ())), preferred_element_type=F32)
    vt_ref[...] = vt.astype(BF16)

    @pl.when(i == 0)
    def _():
        carry_ref[...] = jnp.zeros((1, HEAD), F32)

    fz = jnp.dot(hb, wf_ref[...], preferred_element_type=F32)
    logf = jax.nn.log_sigmoid(fz + bf_ref[...])
    r = lax.broadcasted_iota(jnp.int32, (TM, TM), 0)
    c = lax.broadcasted_iota(jnp.int32, (TM, TM), 1)
    lower = jnp.where(c <= r, 1.0, 0.0).astype(BF16)
    cum = jnp.dot(lower, _split3(logf), preferred_element_type=F32)
    f_cum = (cum[:, :HEAD] + cum[:, HEAD:2 * HEAD]) + cum[:, 2 * HEAD:] + carry_ref[...]
    carry_ref[...] = f_cum[TM - 1:TM, :]
    f2 = f_cum * LOG2E
    f2_ref[...] = f2
    aug = jnp.dot(_split3(f2), place_ref[...], preferred_element_type=F32)
    for hh in range(C_HEADS):
        k_ref[:, (2 * hh + 1) * HEAD:(2 * hh + 2) * HEAD] = aug[:, hh * HEAD:(hh + 1) * HEAD].astype(BF16)


def _mix_c(layer, x, gain, sc, sh, w_in_bf, wf, place, qg, kg, bf):
    row = lambda i: (0, 0)
    wblk = 2 * A_WIDTH + B_WIDTH
    tile = pl.BlockSpec((TM, 2 * C_WIDTH), lambda i: (i, 0))
    return pl.pallas_call(
        _mix_c_kernel,
        out_shape=(
            jax.ShapeDtypeStruct((SEQ, 2 * C_WIDTH), BF16),
            jax.ShapeDtypeStruct((SEQ, 2 * C_WIDTH), BF16),
            jax.ShapeDtypeStruct((C_WIDTH, SEQ), BF16),
            jax.ShapeDtypeStruct((SEQ, HEAD), F32),
        ),
        grid=(SEQ // TM,),
        in_specs=[
            pl.BlockSpec((TM, D), lambda i: (i, 0)),
            pl.BlockSpec((1, D), row), pl.BlockSpec((1, D), row), pl.BlockSpec((1, D), row),
            pl.BlockSpec((None, D, wblk), lambda i: (layer, 0, 1), pipeline_mode=pl.Buffered(1)),
            pl.BlockSpec((None, D, wblk), lambda i: (layer, 0, 2), pipeline_mode=pl.Buffered(1)),
            pl.BlockSpec((D, HEAD), row),
            pl.BlockSpec((3 * HEAD, C_WIDTH), row),
            pl.BlockSpec((1, HEAD), row), pl.BlockSpec((1, HEAD), row),
            pl.BlockSpec((1, HEAD), row),
        ],
        out_specs=(tile, tile, pl.BlockSpec((C_WIDTH, TM), lambda i: (0, i)),
                   pl.BlockSpec((TM, HEAD), lambda i: (i, 0))),
        scratch_shapes=[pltpu.VMEM((1, HEAD), F32)],
        compiler_params=pltpu.CompilerParams(
            dimension_semantics=("arbitrary",), vmem_limit_bytes=VMEM_LIMIT),
        name="mix_c",
    )(x, gain, sc, sh, w_in_bf, w_in_bf, wf, place, qg, kg, bf)


def _attn_kernel(with_ada, first_ref, q_ref, k_ref, vt_ref, *refs):
    i = pl.program_id(1)
    step = pl.program_id(0) * (SEQ // TQ) + i
    first = first_ref[step]
    if with_ada:
        c_ref, wa_ref, ba_ref, o_ref, mod_ref, st_ref = refs

        @pl.when(step < ADA_BLOCKS)
        def _():
            _ada_kernel(c_ref, wa_ref, ba_ref, mod_ref)
    else:
        o_ref, st_ref = refs

    def scores_t(g, j):
        off = pl.multiple_of(j * TK, TK)
        cols = slice(2 * g * HEAD, 2 * (g + 1) * HEAD)
        return _nt_dot(k_ref[pl.ds(off, TK), cols], q_ref[:, cols])

    def update(g, read_st, j, m, l, acc):
        m_new = jnp.maximum(m, jnp.max(read_st(), axis=0, keepdims=True))
        alpha = jnp.exp2(m - m_new)
        p = jnp.exp2(read_st() - m_new)
        l_new = alpha * l + jnp.sum(p, axis=0, keepdims=True)
        off = pl.multiple_of(j * TK, TK)
        pv = jnp.dot(vt_ref[g * HEAD:(g + 1) * HEAD, pl.ds(off, TK)], p.astype(BF16),
                     preferred_element_type=F32)
        return m_new, l_new, alpha * acc + pv

    for g in range(ATTN_HEADS):
        st_ref[g] = scores_t(g, first)

    def body(j, carry):
        out = []
        for g in range(ATTN_HEADS):
            out.append(update(g, lambda g=g: st_ref[g], j, *carry[g]))
            st_ref[g] = scores_t(g, j + 1)
        return tuple(out)

    init = (jnp.full((1, TQ), -jnp.inf, F32), jnp.zeros((1, TQ), F32), jnp.zeros((HEAD, TQ), F32))
    pairs = (i - first) // 2
    carry = lax.fori_loop(0, pairs, lambda t, c: body(first + 2 * t + 1, body(first + 2 * t, c)),
                          (init,) * ATTN_HEADS)
    odd = jnp.maximum(i - 1, 0)
    carry = lax.cond((i - first) % 2 == 1, lambda c: body(odd, c), lambda c: c, carry)
    kpos = lax.broadcasted_iota(jnp.int32, (TK, TQ), 0)
    qpos = lax.broadcasted_iota(jnp.int32, (TK, TQ), 1)
    for g in range(ATTN_HEADS):
        m, l, acc = update(g, lambda g=g: jnp.where(kpos <= qpos, st_ref[g], -jnp.inf), i, *carry[g])
        o_ref[:, g * HEAD:(g + 1) * HEAD] = (acc / l).T.astype(BF16)


def _first_key_block(f2, qg, kg):
    nq = SEQ // TQ
    bound = LOG2E * HEAD ** 0.5 * jnp.max(jnp.abs(qg)) * jnp.max(jnp.abs(kg)) * 1.02
    f_start = f2[0::TQ, :C_HEADS]
    f_end = f2[TK - 1::TK, :C_HEADS]
    gap = f_end[None, :, :] - f_start[:, None, :]
    needed = gap <= 2.0 * bound + ATTN_DROP_LOG2
    needed = jnp.any(needed.reshape(nq, SEQ // TK, C_HEADS // ATTN_HEADS, ATTN_HEADS), axis=-1)
    first = jnp.argmax(needed, axis=1)
    return first.T.reshape(-1).astype(jnp.int32)


def _attn(first, qa, ka, vt, next_ada=None):
    gw = ATTN_HEADS * HEAD
    nq = SEQ // TQ
    in_specs = [
        pl.BlockSpec((TQ, 2 * gw), lambda h, i, f: (i, h)),
        pl.BlockSpec((SEQ, 2 * gw), lambda h, i, f: (0, h)),
        pl.BlockSpec((gw, SEQ), lambda h, i, f: (h, 0)),
    ]
    out_shape = jax.ShapeDtypeStruct((SEQ, C_WIDTH), BF16)
    out_specs = pl.BlockSpec((TQ, gw), lambda h, i, f: (i, h))
    args = (first, qa, ka, vt)
    if next_ada is not None:
        layer, c16, w_ada, b_ada3 = next_ada
        assert ADA_BLOCKS <= (C_HEADS // ATTN_HEADS) * nq
        blk = lambda h, i, f: jnp.minimum(h * nq + i, ADA_BLOCKS - 1)
        in_specs += [
            pl.BlockSpec((16, D), lambda h, i, f: (0, 0)),
            pl.BlockSpec((None, D, ADA_TN), lambda h, i, f: (layer, 0, blk(h, i, f))),
            pl.BlockSpec((None, 1, ADA_TN), lambda h, i, f: (layer, 0, blk(h, i, f))),
        ]
        out_shape = (out_shape, jax.ShapeDtypeStruct((1, 6 * D), F32))
        out_specs = (out_specs, pl.BlockSpec((1, ADA_TN), lambda h, i, f: (0, blk(h, i, f))))
        args += (c16, w_ada, b_ada3)
    return pl.pallas_call(
        functools.partial(_attn_kernel, next_ada is not None),
        out_shape=out_shape,
        grid_spec=pltpu.PrefetchScalarGridSpec(
            num_scalar_prefetch=1,
            grid=(C_HEADS // ATTN_HEADS, nq),
            in_specs=in_specs,
            out_specs=out_specs,
            scratch_shapes=[pltpu.VMEM((ATTN_HEADS, TK, TQ), F32)],
        ),
        compiler_params=pltpu.CompilerParams(
            dimension_semantics=("arbitrary", "arbitrary"), vmem_limit_bytes=VMEM_LIMIT * 3 // 4),
        name="attn",
    )(*args)


def _first_max(vals):
    best, idx = vals[0], jnp.zeros(vals[0].shape, jnp.int32)
    for j in range(1, len(vals)):
        better = vals[j] > best
        idx = jnp.where(better, j, idx)
        best = jnp.where(better, vals[j], best)
    return idx, best


def _pick(idx, vals):
    out = vals[-1]
    for j in range(len(vals) - 2, -1, -1):
        out = jnp.where(idx == j, vals[j], out)
    return out


def _out_kernel(yab_ref, yc_ref, x_ref, w_ref, g1_ref, gain_ref, sc_ref, sh_ref, wr_ref, br_ref,
                x1_ref, h2_ref, info_ref, cnt_ref, carry_ref):
    i = pl.program_id(0)
    y = jnp.dot(yab_ref[...], w_ref[0:A_WIDTH + B_WIDTH, :], preferred_element_type=F32)
    y = y + jnp.dot(yc_ref[...], w_ref[A_WIDTH + B_WIDTH:, :], preferred_element_type=F32)
    x1 = x_ref[...] + g1_ref[...] * y
    x1_ref[...] = x1
    h2 = _rms_mod(x1, gain_ref[...], sc_ref[...], sh_ref[...])
    h2b = h2.astype(BF16)
    lo = pltpu.bitcast(h2b[:, :D // 2].astype(F32), jnp.uint32) >> 16
    hi = pltpu.bitcast(h2b[:, D // 2:].astype(F32), jnp.uint32) & jnp.uint32(0xFFFF0000)
    h2_ref[:, 0, :] = hi | lo

    logits = _nt_dot(wr_ref[...], h2b)
    scores = jax.nn.sigmoid(logits)
    sel = scores + br_ref[...]
    sel_r = [sel[k:k + 1, :] for k in range(N_EXPERTS)]
    sc_r = [scores[k:k + 1, :] for k in range(N_EXPERTS)]
    grp = []
    for g in range(N_GROUPS):
        a = sel_r[GROUP * g:GROUP * (g + 1)]
        pair = [a[p] + a[q] for p in range(GROUP) for q in range(p + 1, GROUP)]
        grp.append(functools.reduce(jnp.maximum, pair))
    gi, _ = _first_max(grp)
    cand = [_pick(gi, [sel_r[GROUP * g + j] for g in range(N_GROUPS)]) for j in range(GROUP)]
    cand_s = [_pick(gi, [sc_r[GROUP * g + j] for g in range(N_GROUPS)]) for j in range(GROUP)]
    i0, _ = _first_max(cand)
    i1, _ = _first_max([jnp.where(i0 == j, -jnp.inf, cand[j]) for j in range(GROUP)])
    s0, s1 = _pick(i0, cand_s), _pick(i1, cand_s)
    e0, e1 = GROUP * gi + i0, GROUP * gi + i1
    den = s0 + s1

    @pl.when(i == 0)
    def _():
        carry_ref[...] = jnp.zeros((N_EXPERTS, HEAD), F32)

    ek = lax.broadcasted_iota(jnp.int32, (N_EXPERTS, TM), 0)
    oh0 = (ek == e0).astype(F32)
    oh1 = (ek == e1).astype(F32)
    both = oh0 + oh1
    r = lax.broadcasted_iota(jnp.int32, (TM, TM), 0)
    c = lax.broadcasted_iota(jnp.int32, (TM, TM), 1)
    before = (r < c).astype(BF16)
    run = jnp.dot(both.astype(BF16), before, preferred_element_type=F32) + carry_ref[:, 0:1]
    rank0 = jnp.sum(run * oh0, axis=0, keepdims=True)
    rank1 = jnp.sum(run * oh1, axis=0, keepdims=True)
    total = carry_ref[...] + jnp.sum(both, axis=-1, keepdims=True)
    carry_ref[...] = total
    cnt_ref[...] = total

    info_ref[0:1, :] = e0.astype(F32)
    info_ref[1:2, :] = e1.astype(F32)
    info_ref[2:3, :] = s0 / den
    info_ref[3:4, :] = s1 / den
    info_ref[4:5, :] = rank0
    info_ref[5:6, :] = rank1
    info_ref[6:8, :] = jnp.zeros((2, TM), F32)


def _out_proj(layer, yab, yc, x, w_out, g1, gain, sc, sh, wr_t, br):
    row = lambda i: (0, 0)
    return pl.pallas_call(
        _out_kernel,
        out_shape=(
            jax.ShapeDtypeStruct((SEQ, D), F32),
            jax.ShapeDtypeStruct((SEQ, 1, D // 2), jnp.uint32),
            jax.ShapeDtypeStruct((8, SEQ), F32),
            jax.ShapeDtypeStruct((N_EXPERTS, HEAD), F32),
        ),
        grid=(SEQ // TM,),
        in_specs=[
            pl.BlockSpec((TM, A_WIDTH + B_WIDTH), lambda i: (i, 0)),
            pl.BlockSpec((TM, C_WIDTH), lambda i: (i, 0)),
            pl.BlockSpec((TM, D), lambda i: (i, 0)),
            pl.BlockSpec((None, D, D), lambda i: (layer, 0, 0), pipeline_mode=pl.Buffered(1)),
            pl.BlockSpec((1, D), row), pl.BlockSpec((1, D), row),
            pl.BlockSpec((1, D), row), pl.BlockSpec((1, D), row),
            pl.BlockSpec((N_EXPERTS, D), row),
            pl.BlockSpec((N_EXPERTS, 1), row),
        ],
        out_specs=(
            pl.BlockSpec((TM, D), lambda i: (i, 0)),
            pl.BlockSpec((TM, 1, D // 2), lambda i: (i, 0, 0)),
            pl.BlockSpec((8, TM), lambda i: (0, i)),
            pl.BlockSpec((N_EXPERTS, HEAD), row),
        ),
        scratch_shapes=[pltpu.VMEM((N_EXPERTS, HEAD), F32)],
        compiler_params=pltpu.CompilerParams(
            dimension_semantics=("arbitrary",), vmem_limit_bytes=VMEM_LIMIT),
        name="out_proj",
    )(yab, yc, x, w_out, g1, gain, sc, sh, wr_t, br)


def _moe_kernel(layer, dest_ref, bexp_ref, nvalid_ref, nused_ref, newexp_ref, nextexp_ref,
                h_hbm, wg_hbm, wu_hbm, wd_hbm, zeros_hbm, y_hbm,
                code_ref, xs_ref, ys_ref, xb_ref, hid_ref, wg32_ref, wu32_ref, wd32_ref,
                wg_ref, wu_ref, wd_ref, gsem, ssem, wsem):
    b = pl.program_id(0)
    n_used = nused_ref[0]
    slot = b % 2

    def weight_copies(e):
        return (pltpu.make_async_copy(wg_hbm.at[layer, e], wg32_ref, wsem.at[0]),
                pltpu.make_async_copy(wu_hbm.at[layer, e], wu32_ref, wsem.at[1]),
                pltpu.make_async_copy(wd_hbm.at[layer, e], wd32_ref, wsem.at[2]))

    def switch_expert():
        for cp in weight_copies(bexp_ref[b]):
            cp.wait()

        def cast(k, _):
            r = pl.multiple_of(k * 256, 256)
            wg_ref[pl.ds(r, 256), :] = wg32_ref[pl.ds(r, 256), :].astype(BF16)
            wu_ref[pl.ds(r, 256), :] = wu32_ref[pl.ds(r, 256), :].astype(BF16)
            r = pl.multiple_of(k * 128, 128)
            wd_ref[pl.ds(r, 128), :] = wd32_ref[pl.ds(r, 128), :].astype(BF16)
            return 0
        lax.fori_loop(0, D // 256, cast, 0)

        @pl.when(nextexp_ref[b] >= 0)
        def _():
            for cp in weight_copies(jnp.maximum(nextexp_ref[b], 0)):
                cp.start(priority=1)

    def gather_copy(blk, r, s):
        tok = code_ref[blk * MOE_BLK + r] & (SEQ - 1)
        return pltpu.make_async_copy(h_hbm.at[tok], xs_ref.at[s, pl.ds(r, 1), :],
                                     gsem.at[s])

    def scatter_copy(blk, r, s):
        dst = code_ref[blk * MOE_BLK + r]
        return pltpu.make_async_copy(ys_ref.at[s, pl.ds(r, 1), :], y_hbm.at[dst],
                                     ssem.at[s])

    def for_rows(n, fn):
        def body(r, _):
            fn(r)
            return 0
        lax.fori_loop(0, n, body, 0)

    def scatter_wait(n, s):
        n8 = pl.multiple_of((n >> 3) << 3, 8)

        @pl.when(n8 > 0)
        def _():
            pltpu.make_async_copy(ys_ref.at[s, pl.ds(0, n8), :], ys_ref.at[s, pl.ds(0, n8), :],
                                  ssem.at[s]).wait()

        for r in range(7):
            @pl.when(n8 + r < n)
            def _():
                pltpu.make_async_copy(ys_ref.at[s, pl.ds(r, 1), :], y_hbm.at[r],
                                      ssem.at[s]).wait()

    @pl.when(b == 0)
    def _():
        for cp in weight_copies(bexp_ref[0]):
            cp.start()

        fill = pltpu.make_async_copy(zeros_hbm, code_ref, wsem.at[3])
        fill.start()
        fill.wait()

        def place(a, _):
            code_ref[dest_ref[a]] = a
            return 0
        lax.fori_loop(0, N_ASSIGN, place, 0, unroll=8)
        for_rows(MOE_BLK, lambda r: gather_copy(0, r, 0).start())

    def step(prefetch):
        @pl.when(newexp_ref[b] == 1)
        def _():
            switch_expert()

        for r in range(MOE_BLK):
            gather_copy(b, r, slot).wait()

        packed = xs_ref[slot]
        xb_ref[:, :D // 2] = pltpu.bitcast(packed << 16, F32).astype(BF16)
        xb_ref[:, D // 2:] = pltpu.bitcast(packed & jnp.uint32(0xFFFF0000), F32).astype(BF16)
        prev = jnp.maximum(b - 1, 0)
        n_prev = jnp.where(b >= 1, nvalid_ref[prev], 0)
        x = xb_ref[...]
        rows_per = MOE_BLK // MOE_CHUNKS

        hc = D_EXPERT // MOE_CHUNKS
        for c in range(MOE_CHUNKS):
            cs = slice(c * hc, (c + 1) * hc)
            gate = jnp.dot(x, wg_ref[:, cs], preferred_element_type=F32)
            up = jnp.dot(x, wu_ref[:, cs], preferred_element_type=F32)
            hid = jax.nn.silu(gate) * up
            if prefetch:
                for r in range(c * rows_per, (c + 1) * rows_per):
                    gather_copy(b + 1, r, 1 - slot).start()
                tie = xs_ref[slot, 0:8, 0:hc]
                tie = pltpu.bitcast((tie >> 16) >> 16, F32)
                hid = jnp.concatenate([hid[0:8] + tie, hid[8:]], axis=0)
            hid_ref[:, cs] = hid.astype(BF16)

        scatter_wait(jnp.where(b >= 2, nvalid_ref[jnp.maximum(b - 2, 0)], 0), slot)
        hidb = hid_ref[...]
        oc = D // MOE_CHUNKS
        for c in range(MOE_CHUNKS):
            for r in range(c * rows_per, (c + 1) * rows_per):
                @pl.when(r < n_prev)
                def _():
                    scatter_copy(prev, r, 1 - slot).start(priority=1)
            cs = slice(c * oc, (c + 1) * oc)
            ys_ref[slot, :, cs] = jnp.dot(hidb, wd_ref[:, cs], preferred_element_type=F32)

    @pl.when(b < n_used - 1)
    def _():
        step(True)

    @pl.when(b == n_used - 1)
    def _():
        step(False)
        for_rows(nvalid_ref[b], lambda r: scatter_copy(b, r, slot).start())
        scatter_wait(nvalid_ref[b], slot)
        scatter_wait(jnp.where(b >= 1, nvalid_ref[jnp.maximum(b - 1, 0)], 0), 1 - slot)


def _moe(layer, dest, bexp, nvalid, nused, newexp, nextexp, h2, wg, wu, wd):
    hbm = pl.BlockSpec(memory_space=pl.ANY)
    return pl.pallas_call(
        functools.partial(_moe_kernel, layer),
        out_shape=jax.ShapeDtypeStruct((N_ASSIGN, 1, D), F32),
        grid_spec=pltpu.PrefetchScalarGridSpec(
            num_scalar_prefetch=6,
            grid=(MOE_NB,),
            in_specs=[hbm, hbm, hbm, hbm, hbm],
            out_specs=hbm,
            scratch_shapes=[
                pltpu.SMEM((MOE_ROWS,), jnp.int32),
                pltpu.VMEM((2, MOE_BLK, D // 2), jnp.uint32),
                pltpu.VMEM((2, MOE_BLK, D), F32),
                pltpu.VMEM((MOE_BLK, D), BF16),
                pltpu.VMEM((MOE_BLK, D_EXPERT), BF16),
                pltpu.VMEM((D, D_EXPERT), F32),
                pltpu.VMEM((D, D_EXPERT), F32),
                pltpu.VMEM((D_EXPERT, D), F32),
                pltpu.VMEM((D, D_EXPERT), BF16),
                pltpu.VMEM((D, D_EXPERT), BF16),
                pltpu.VMEM((D_EXPERT, D), BF16),
                pltpu.SemaphoreType.DMA((2,)),
                pltpu.SemaphoreType.DMA((2,)),
                pltpu.SemaphoreType.DMA((4,)),
            ],
        ),
        compiler_params=pltpu.CompilerParams(
            dimension_semantics=("arbitrary",), vmem_limit_bytes=VMEM_LIMIT),
        name="moe",
    )(dest, bexp, nvalid, nused, newexp, nextexp, h2, wg, wu, wd, jnp.zeros((MOE_ROWS,), jnp.int32))


def _combine_kernel(x1_ref, y0_ref, y1_ref, info_ref, g2_ref, o_ref):
    o_ref[...] = _moe_residual(x1_ref, y0_ref, y1_ref, info_ref, g2_ref)


def _combine(x1, y, info, g2):
    nt = SEQ // TM
    return pl.pallas_call(
        _combine_kernel,
        out_shape=jax.ShapeDtypeStruct((SEQ, D), F32),
        grid=(nt,),
        in_specs=[
            pl.BlockSpec((TM, D), lambda i: (i, 0)),
            pl.BlockSpec((TM, 1, D), lambda i: (i, 0, 0)),
            pl.BlockSpec((TM, 1, D), lambda i: (i + nt, 0, 0)),
            pl.BlockSpec((8, TM), lambda i: (0, i)),
            pl.BlockSpec((1, D), lambda i: (0, 0)),
        ],
        out_specs=pl.BlockSpec((TM, D), lambda i: (i, 0)),
        compiler_params=pltpu.CompilerParams(
            dimension_semantics=("arbitrary",), vmem_limit_bytes=VMEM_LIMIT * 3 // 4),
        name="combine",
    )(x1, y, y, info, g2)


def _dispatch_plan(info, counts):
    cnt = counts[:, 0].astype(jnp.int32)
    padded = (cnt + MOE_BLK - 1) // MOE_BLK * MOE_BLK
    ends = jnp.cumsum(padded)
    pad_start = ends - padded
    e = info[0:2].astype(jnp.int32)
    rank = info[4:6].astype(jnp.int32)
    onehot = e[:, :, None] == jnp.arange(N_EXPERTS, dtype=jnp.int32)
    dest = jnp.sum(jnp.where(onehot, pad_start, 0), axis=-1) + rank
    blk_start = jnp.arange(MOE_NB, dtype=jnp.int32) * MOE_BLK
    bexp = jnp.sum(blk_start[:, None] >= ends[None, :], axis=-1)
    bexp = jnp.minimum(bexp, N_EXPERTS - 1).astype(jnp.int32)
    in_blk = jnp.arange(N_EXPERTS, dtype=jnp.int32)[None, :] == bexp[:, None]
    left = jnp.sum(jnp.where(in_blk, cnt + pad_start, 0), axis=-1) - blk_start
    nvalid = jnp.clip(left, 0, MOE_BLK).astype(jnp.int32)
    nused = (ends[-1:] // MOE_BLK).astype(jnp.int32)
    used = jnp.arange(MOE_NB, dtype=jnp.int32) < nused[0]
    newexp = jnp.concatenate([jnp.ones((1,), jnp.int32), (bexp[1:] != bexp[:-1]).astype(jnp.int32)])
    later = (bexp[None, :] > bexp[:, None]) & used[None, :]
    nextexp = jnp.min(jnp.where(later, bexp[None, :], N_EXPERTS), axis=1)
    nextexp = jnp.where(nextexp < N_EXPERTS, nextexp, -1).astype(jnp.int32)
    return dest.reshape(N_ASSIGN), bexp, nvalid, nused, newexp, nextexp


def kernel(x, c, w_ada, b_ada, g_mix, g_ffn, w_in, a_ws, a_bs, a_vg, b_w, b_scale,
           c_qg, c_kg, c_bf, w_out, w_router, b_router, e_gate, e_up, e_down):
    xs = x.reshape(SEQ, D)
    c16 = jnp.broadcast_to(c, (16, D))
    b_ada3 = b_ada.reshape(DEPTH, 1, 6 * D)
    mod = _ada(c16, w_ada, b_ada3)
    wr_t = w_router.T.astype(BF16)
    br = b_router.reshape(N_EXPERTS, 1).astype(F32)
    n_uvp = 2 * A_WIDTH + B_WIDTH
    place = np.zeros((3 * HEAD, C_WIDTH), np.float32)
    for pj in range(3):
        for ph in range(C_HEADS):
            place[pj * HEAD + ph, ph * HEAD + pj] = 1.0
    place = jnp.asarray(place, BF16)
    w_in_bf, w_out_bf = w_in.astype(BF16), w_out.astype(BF16)
    pending = None
    for l in range(DEPTH):
        sh1, sc1, g1, sh2, sc2, g2 = [mod[:, j * D:(j + 1) * D] for j in range(6)]
        gain1 = g_mix[l].reshape(1, D)
        wf = jnp.pad(w_in_bf[l, :, n_uvp + 3 * C_WIDTH:], ((0, 0), (0, HEAD - C_HEADS)))
        bf = jnp.pad(c_bf[l], (0, HEAD - C_HEADS)).reshape(1, HEAD)
        bsb = jnp.repeat(a_bs[l].T, HEAD, axis=1)
        yab = _mix_ab(l, xs, gain1, sc1, sh1, w_in_bf, a_ws[l], bsb, a_vg[l].reshape(1, A_WIDTH),
                      b_w[l], b_scale[l].reshape(1, B_WIDTH), moe=pending)
        if pending is not None:
            yab, xs = yab
        qa, ka, vt, f2 = _mix_c(l, xs, gain1, sc1, sh1, w_in_bf, wf, place, c_qg[l].reshape(1, HEAD),
                                c_kg[l].reshape(1, HEAD), bf)
        first = _first_key_block(f2, c_qg[l], c_kg[l])
        if l + 1 < DEPTH:
            yc, mod = _attn(first, qa, ka, vt, next_ada=(l + 1, c16, w_ada, b_ada3))
        else:
            yc = _attn(first, qa, ka, vt)
        x1, h2, info, counts = _out_proj(l, yab, yc, xs, w_out_bf, g1,
                                         g_ffn[l].reshape(1, D), sc2, sh2, wr_t, br)
        plan = _dispatch_plan(info, counts)
        y = _moe(l, *plan, h2, e_gate, e_up, e_down)
        xs, pending = x1, (y, info, g2)
    return _combine(xs, *pending).reshape(1, SEQ, D)
```
